```python
import math
import jax, jax.numpy as jnp
from jax import lax
import numpy as np

D_MODEL = 1024
BATCH = 8
SEQ = 2048
DEPTH = 2
DEC_BATCH = 32
DEC_SEQ = 8
PAST_LEN = 8192
PAGE_SIZE = 128

N_MIXERS = 2
N_CONV_LAYERS = (DEPTH + 1) // 2
N_ATTN_LAYERS = DEPTH // 2
N_HEADS = 16
HEAD_DIM = D_MODEL // N_HEADS
D_CONV = D_MODEL
CONV_WIDTH = 31
D_FF = ((8 * D_MODEL // 3 + 127) // 128) * 128
MOBA_BLOCK = 256
MOBA_TOP_K = 3
QUERY_CHUNK = 4
NUM_BUCKETS = 32
MAX_DISTANCE = 128
NORM_EPS = 1e-6

kernel_name = "hybrid_conformer_moba_decoder_step"


def rms_norm(x, g):
    xf = x.astype(jnp.float32)
    n = xf * lax.rsqrt(jnp.mean(xf * xf, axis=-1, keepdims=True) + NORM_EPS)
    return (n * g).astype(x.dtype)


def modulated_rms_norm(x, g, shift, scale):
    xf = x.astype(jnp.float32)
    n = xf * lax.rsqrt(jnp.mean(xf * xf, axis=-1, keepdims=True) + NORM_EPS)
    return (n * g * (1.0 + scale[:, None, :]) + shift[:, None, :]).astype(x.dtype)


def layer_norm(x, g, b):
    xf = x.astype(jnp.float32)
    mu = jnp.mean(xf, axis=-1, keepdims=True)
    var = jnp.mean(jnp.square(xf - mu), axis=-1, keepdims=True)
    return ((xf - mu) * lax.rsqrt(var + NORM_EPS) * g + b).astype(x.dtype)


def swiglu(h, wg, wu, wd):
    return (jax.nn.silu(h @ wg) * (h @ wu)) @ wd


def rel_bucket(rel):
    n = jnp.maximum(rel, 0)
    max_exact = NUM_BUCKETS // 2
    nf = jnp.maximum(n, max_exact).astype(jnp.float32)
    large = max_exact + (jnp.log(nf / max_exact) / math.log(MAX_DISTANCE / max_exact)
                         * (NUM_BUCKETS - max_exact)).astype(jnp.int32)
    large = jnp.minimum(large, NUM_BUCKETS - 1)
    return jnp.where(n < max_exact, n, large)


def conv_mixer(h, hist, w_pw1, b_pw1, w_dw, b_dw, ln_g, ln_b, w_pw2, b_pw2):
    a = h @ w_pw1 + b_pw1
    u = a[..., :D_CONV] * jax.nn.sigmoid(a[..., D_CONV:])
    full = jnp.concatenate([hist.astype(u.dtype), u], axis=1)
    y = lax.conv_general_dilated(full, w_dw[:, None, :].astype(full.dtype), (1,), 'VALID',
                                 dimension_numbers=('NWC', 'WIO', 'NWC'),
                                 feature_group_count=D_CONV) + b_dw
    y = jax.nn.silu(layer_norm(y, ln_g, ln_b))
    out = y @ w_pw2 + b_pw2
    return out, full[:, -(CONV_WIDTH - 1):]


def moba_attention(q, k, v, q_start, rel_bias):
    B, Sq, H, HD = q.shape
    Lk = k.shape[1]
    nb = -(-Lk // MOBA_BLOCK)
    pad = nb * MOBA_BLOCK - Lk
    k = jnp.pad(k, ((0, 0), (0, pad), (0, 0), (0, 0)))
    v = jnp.pad(v, ((0, 0), (0, pad), (0, 0), (0, 0)))
    kb = k.reshape(B, nb, MOBA_BLOCK, H, HD).transpose(0, 3, 1, 2, 4)
    vb = v.reshape(B, nb, MOBA_BLOCK, H, HD).transpose(0, 3, 1, 2, 4)
    kmean = jnp.mean(kb.astype(jnp.float32), axis=3)
    topk = min(MOBA_TOP_K, nb)
    qc = math.gcd(Sq, QUERY_CHUNK)
    n_chunks = Sq // qc
    qs = q.transpose(0, 2, 1, 3).reshape(B, H, n_chunks, qc, HD).transpose(2, 0, 1, 3, 4)
    pos = (q_start + jnp.arange(Sq, dtype=jnp.int32)).reshape(n_chunks, qc)
    bidx = jnp.arange(B)[:, None, None, None]
    hidx = jnp.arange(H)[None, :, None, None]
    blk_ids = jnp.arange(nb, dtype=jnp.int32)
    offs = jnp.arange(MOBA_BLOCK, dtype=jnp.int32)
    scale = 1.0 / math.sqrt(HD)

    def chunk(args):
        qq, p = args
        own = p // MOBA_BLOCK
        gate = jnp.einsum('bhqd,bhnd->bhqn', qq.astype(jnp.float32), kmean)
        gate = jnp.where(blk_ids[None, :] < own[:, None], gate, -jnp.inf)
        _, sel = lax.top_k(gate, topk)
        sel_ok = jnp.arange(topk)[None, :] < own[:, None]
        ok = jnp.concatenate([sel_ok, jnp.ones((qc, 1), bool)], axis=-1)
        blocks = jnp.concatenate(
            [sel.astype(jnp.int32), jnp.broadcast_to(own[None, None, :, None], (B, H, qc, 1))], axis=-1)
        kg = kb[bidx, hidx, blocks]
        vg = vb[bidx, hidx, blocks]
        kpos = blocks[..., None] * MOBA_BLOCK + offs
        logits = jnp.einsum('bhqd,bhqskd->bhqsk', qq, kg).astype(jnp.float32) * scale
        pq = p[None, None, :, None, None]
        logits = logits + rel_bias[rel_bucket(pq - kpos), hidx[..., None]].astype(jnp.float32)
        mask = ok[None, None, :, :, None] & (kpos <= pq)
        logits = jnp.where(mask, logits, -jnp.inf)
        nsel = logits.shape[3]
        w = jax.nn.softmax(logits.reshape(B, H, qc, nsel * MOBA_BLOCK), axis=-1)
        w = w.reshape(B, H, qc, nsel, MOBA_BLOCK).astype(vg.dtype)
        return jnp.einsum('bhqsk,bhqskd->bhqd', w, vg)

    outs = lax.map(chunk, (qs, pos))
    return outs.transpose(1, 0, 3, 2, 4).reshape(B, Sq, H * HD)


def attn_mixer(h, past_k, past_v, q_start, w_qkv, w_o, rel_bias):
    B, S, _ = h.shape
    qkv = (h @ w_qkv).reshape(B, S, 3, N_HEADS, HEAD_DIM)
    q, k, v = qkv[:, :, 0], qkv[:, :, 1], qkv[:, :, 2]
    if past_k is None:
        kf, vf = k, v
    else:
        kf = jnp.concatenate([past_k.astype(k.dtype), k], axis=1)
        vf = jnp.concatenate([past_v.astype(v.dtype), v], axis=1)
    out = moba_attention(q, kf, vf, q_start, rel_bias)
    return out @ w_o, k, v


def trunk(x, c, conv_hist, past_k, past_v, q_start, norm_g, ada_w, ada_b, ffn_w_gate, ffn_w_up, ffn_w_down,
          conv_w_pw1, conv_b_pw1, conv_w_dw, conv_b_dw, conv_ln_g, conv_ln_b, conv_w_pw2, conv_b_pw2,
          attn_w_qkv, attn_w_o, rel_bias, final_norm_g):
    B = x.shape[0]
    sc = jax.nn.silu(c)
    hists, new_k, new_v = [], [], []
    for layer in range(DEPTH):
        mod = (sc @ ada_w[layer] + ada_b[layer]).astype(x.dtype).reshape(B, 3, 3, D_MODEL)
        h = modulated_rms_norm(x, norm_g[layer, 0], mod[:, 0, 0], mod[:, 0, 1])
        x = x + 0.5 * mod[:, 0, 2][:, None, :] * swiglu(h, ffn_w_gate[layer, 0], ffn_w_up[layer, 0], ffn_w_down[layer, 0])
        h = modulated_rms_norm(x, norm_g[layer, 1], mod[:, 1, 0], mod[:, 1, 1])
        j = layer // N_MIXERS
        if layer % N_MIXERS == 0:
            y, hist = conv_mixer(h, conv_hist[j], conv_w_pw1[j], conv_b_pw1[j], conv_w_dw[j], conv_b_dw[j],
                                 conv_ln_g[j], conv_ln_b[j], conv_w_pw2[j], conv_b_pw2[j])
            hists.append(hist)
        else:
            pk = None if past_k is None else past_k[j]
            pv = None if past_v is None else past_v[j]
            y, k, v = attn_mixer(h, pk, pv, q_start, attn_w_qkv[j], attn_w_o[j], rel_bias)
            new_k.append(k)
            new_v.append(v)
        x = x + mod[:, 1, 2][:, None, :] * y
        h = modulated_rms_norm(x, norm_g[layer, 2], mod[:, 2, 0], mod[:, 2, 1])
        x = x + 0.5 * mod[:, 2, 2][:, None, :] * swiglu(h, ffn_w_gate[layer, 1], ffn_w_up[layer, 1], ffn_w_down[layer, 1])
    y_out = rms_norm(x, final_norm_g)
    return y_out, jnp.stack(hists), jnp.stack(new_k), jnp.stack(new_v)


def setup_inputs(seed: int = 0) -> dict:
    key = jax.random.key(seed)
    ks = jax.random.split(key, 32)

    def nrm(i, shape, s):
        return jax.random.normal(ks[i], shape, jnp.float32) * s

    n_pages = PAST_LEN // PAGE_SIZE
    n_used = DEC_BATCH * n_pages
    n_phys = n_used + max(n_used // 4, 1)
    page_table = jax.random.permutation(ks[0], n_phys)[:n_used].reshape(DEC_BATCH, n_pages).astype(jnp.int32)
    D = D_MODEL
    return {
        "x_prompt": nrm(1, (BATCH, SEQ, D), 1.0),
        "x_sample": nrm(2, (DEC_BATCH, DEC_SEQ, D), 1.0),
        "state_conv": nrm(3, (N_CONV_LAYERS, DEC_BATCH, CONV_WIDTH - 1, D_CONV), 0.5),
        "cache_k": nrm(4, (N_ATTN_LAYERS, n_phys, PAGE_SIZE, N_HEADS, HEAD_DIM), 1.0),
        "cache_v": nrm(5, (N_ATTN_LAYERS, n_phys, PAGE_SIZE, N_HEADS, HEAD_DIM), 1.0),
        "page_table": page_table,
        "c_prompt": nrm(6, (BATCH, D), 1.0),
        "c_sample": nrm(7, (DEC_BATCH, D), 1.0),
        "norm_g": 1.0 + nrm(8, (DEPTH, 3, D), 0.02),
        "ada_w": nrm(9, (DEPTH, D, 9 * D), D ** -0.5),
        "ada_b": nrm(10, (DEPTH, 9 * D), 0.02),
        "ffn_w_gate": nrm(11, (DEPTH, 2, D, D_FF), D ** -0.5),
        "ffn_w_up": nrm(12, (DEPTH, 2, D, D_FF), D ** -0.5),
        "ffn_w_down": nrm(13, (DEPTH, 2, D_FF, D), D_FF ** -0.5),
        "conv_w_pw1": nrm(14, (N_CONV_LAYERS, D, 2 * D_CONV), D ** -0.5),
        "conv_b_pw1": nrm(15, (N_CONV_LAYERS, 2 * D_CONV), 0.02),
        "conv_w_dw": nrm(16, (N_CONV_LAYERS, CONV_WIDTH, D_CONV), CONV_WIDTH ** -0.5),
        "conv_b_dw": nrm(17, (N_CONV_LAYERS, D_CONV), 0.02),
        "conv_ln_g": 1.0 + nrm(18, (N_CONV_LAYERS, D_CONV), 0.02),
        "conv_ln_b": nrm(19, (N_CONV_LAYERS, D_CONV), 0.02),
        "conv_w_pw2": nrm(20, (N_CONV_LAYERS, D_CONV, D), D_CONV ** -0.5),
        "conv_b_pw2": nrm(21, (N_CONV_LAYERS, D), 0.02),
        "attn_w_qkv": nrm(22, (N_ATTN_LAYERS, D, 3 * D), D ** -0.5),
        "attn_w_o": nrm(23, (N_ATTN_LAYERS, D, D), D ** -0.5),
        "rel_bias": nrm(24, (NUM_BUCKETS, N_HEADS), 0.5),
        "final_norm_g": 1.0 + nrm(25, (D,), 0.02),
    }


def reference(x_prompt, x_sample, state_conv, cache_k, cache_v, page_table, c_prompt, c_sample,
              norm_g, ada_w, ada_b, ffn_w_gate, ffn_w_up, ffn_w_down,
              conv_w_pw1, conv_b_pw1, conv_w_dw, conv_b_dw, conv_ln_g, conv_ln_b, conv_w_pw2, conv_b_pw2,
              attn_w_qkv, attn_w_o, rel_bias, final_norm_g):
    hist0 = jnp.zeros((N_CONV_LAYERS, x_prompt.shape[0], CONV_WIDTH - 1, D_CONV), x_prompt.dtype)
    y_prompt, conv_state_prompt, k_prompt, v_prompt = trunk(
        x_prompt, c_prompt, hist0, None, None, 0, norm_g, ada_w, ada_b, ffn_w_gate, ffn_w_up, ffn_w_down,
        conv_w_pw1, conv_b_pw1, conv_w_dw, conv_b_dw, conv_ln_g, conv_ln_b, conv_w_pw2, conv_b_pw2,
        attn_w_qkv, attn_w_o, rel_bias, final_norm_g)
    n_seq, n_pages = page_table.shape
    past_len = n_pages * PAGE_SIZE
    past_k = cache_k[:, page_table].reshape(N_ATTN_LAYERS, n_seq, past_len, N_HEADS, HEAD_DIM)
    past_v = cache_v[:, page_table].reshape(N_ATTN_LAYERS, n_seq, past_len, N_HEADS, HEAD_DIM)
    y_sample, conv_state_sample, k_sample, v_sample = trunk(
        x_sample, c_sample, state_conv, past_k, past_v, past_len, norm_g, ada_w, ada_b, ffn_w_gate, ffn_w_up,
        ffn_w_down, conv_w_pw1, conv_b_pw1, conv_w_dw, conv_b_dw, conv_ln_g, conv_ln_b, conv_w_pw2, conv_b_pw2,
        attn_w_qkv, attn_w_o, rel_bias, final_norm_g)
    return (y_prompt, y_sample, conv_state_prompt, conv_state_sample, k_prompt, v_prompt, k_sample, v_sample)
```

```python
import functools
import math

import numpy as np
import jax
import jax.numpy as jnp
from jax import lax
from jax.experimental import pallas as pl
from jax.experimental.pallas import tpu as pltpu

F32 = jnp.float32
BF16 = jnp.bfloat16

HEAD_DIM = 64
CONV_WIDTH = 31
MOBA_BLOCK = 256
MOBA_TOP_K = 3
PAGE_SIZE = 128
NUM_BUCKETS = 32
MAX_DISTANCE = 128
NORM_EPS = 1e-6

LANES = 128
SUBLANES = 8
HIST_PAD = 32
VMEM_LIMIT = 56 * 1024 * 1024
PAGES_PER_STEP = 8


def _cparams(sem):
    return pltpu.CompilerParams(dimension_semantics=sem, vmem_limit_bytes=VMEM_LIMIT)


def _resident(block_shape, index_map):
    return pl.BlockSpec(block_shape, index_map, pipeline_mode=pl.Buffered(1))


def _mod_norm(x, g, shift, scale):
    ms = jnp.mean(x * x, axis=-1, keepdims=True)
    n = x * lax.rsqrt(ms + NORM_EPS)
    return n * g * (1.0 + scale) + shift


def _dot(a, b):
    return jnp.dot(a, b, preferred_element_type=F32)


def _dot_nt(a, b):
    return lax.dot_general(a, b, (((1,), (1,)), ((), ())), preferred_element_type=F32)


def _dot_tn(a, b):
    return lax.dot_general(a, b, (((0,), (0,)), ((), ())), preferred_element_type=F32)


def _mod_kernel(c_ref, w_ref, b_ref, o_ref):
    sc = jax.nn.silu(c_ref[...]).astype(BF16)
    o_ref[...] = _dot(sc, w_ref[...].astype(BF16)) + b_ref[...]


def _ada_mod(c_all, ada_w, ada_b):
    ns, d = c_all.shape
    depth = ada_w.shape[0]
    n_mod = ada_w.shape[2] // d
    return pl.pallas_call(
        _mod_kernel,
        grid=(depth, n_mod),
        in_specs=[
            pl.BlockSpec((ns, d), lambda l, j: (0, 0)),
            pl.BlockSpec((None, d, d), lambda l, j: (l, 0, j)),
            pl.BlockSpec((None, None, 1, d), lambda l, j: (l, j, 0, 0)),
        ],
        out_specs=pl.BlockSpec((None, None, ns, d), lambda l, j: (l, j, 0, 0)),
        out_shape=jax.ShapeDtypeStruct((depth, n_mod, ns, d), F32),
        compiler_params=_cparams(("arbitrary", "arbitrary")),
        name="ada_mod",
    )(c_all, ada_w, ada_b.reshape(depth, n_mod, 1, d))


def _ffn_kernel(*refs, nb, t, ff_chunk, pre_attn, final_norm):
    it = iter(refs)
    x_ref, mod_ref, g_ref = next(it), next(it), next(it)
    if pre_attn:
        a_ref, wo_ref, gate1_ref = next(it), next(it), next(it)
    wg_ref, wu_ref, wd_ref = next(it), next(it), next(it)
    if final_norm:
        fg_ref = next(it)
    o_ref, act_ref = next(it), next(it)

    x = x_ref[...]
    d = x.shape[-1]
    d_ff = wg_ref.shape[-1]
    if pre_attn:
        y = _dot(a_ref[...].reshape(nb * t, d), wo_ref[...])
        x = x + gate1_ref[0] * y.reshape(nb, t, d)
    h = _mod_norm(x, g_ref[...], mod_ref[0], mod_ref[1])
    hb = h.reshape(nb * t, d).astype(BF16)
    for j in range(d_ff // ff_chunk):
        sl = slice(j * ff_chunk, (j + 1) * ff_chunk)
        gj = _dot(hb, wg_ref[:, sl])
        uj = _dot(hb, wu_ref[:, sl])
        act_ref[:, sl] = (jax.nn.silu(gj) * uj).astype(BF16)
    y = _dot(act_ref[...], wd_ref[...])
    out = x + 0.5 * mod_ref[2] * y.reshape(nb, t, d)
    if final_norm:
        ms = jnp.mean(out * out, axis=-1, keepdims=True)
        out = out * lax.rsqrt(ms + NORM_EPS) * fg_ref[...]
    o_ref[...] = out


def _ffn(x, mod, norm_g, wg, wu, wd, *, layer, sub, nb, t, attn=None, wo=None, final_g=None):
    nseq, s, d = x.shape
    d_ff = wg.shape[-1]
    pre_attn = attn is not None
    final_norm = final_g is not None
    grid = (nseq // nb, s // t)
    xspec = pl.BlockSpec((nb, t, d), lambda b, i: (b, i, 0))
    in_specs = [
        xspec,
        pl.BlockSpec((None, 3, nb, 1, d), lambda b, i: (layer, sub, b, 0, 0)),
        pl.BlockSpec((None, None, 1, d), lambda b, i: (layer, sub, 0, 0)),
    ]
    args = [x, mod, norm_g]
    if pre_attn:
        in_specs += [
            xspec,
            _resident((d, d), lambda b, i: (0, 0)),
            pl.BlockSpec((None, 1, nb, 1, d), lambda b, i: (layer, 3 * (sub - 1) + 2, b, 0, 0)),
        ]
        args += [attn, wo, mod]
    in_specs += [
        _resident((d, d_ff), lambda b, i: (0, 0)),
        _resident((d, d_ff), lambda b, i: (0, 0)),
        _resident((d_ff, d), lambda b, i: (0, 0)),
    ]
    args += [wg, wu, wd]
    if final_norm:
        in_specs.append(pl.BlockSpec((1, d), lambda b, i: (0, 0)))
        args.append(final_g.reshape(1, d))
    ff_chunk = 256 if d_ff % 256 == 0 else LANES
    return pl.pallas_call(
        functools.partial(_ffn_kernel, nb=nb, t=t, ff_chunk=ff_chunk, pre_attn=pre_attn,
                          final_norm=final_norm),
        grid=grid,
        in_specs=in_specs,
        out_specs=xspec,
        out_shape=jax.ShapeDtypeStruct(x.shape, F32),
        scratch_shapes=[pltpu.VMEM((nb * t, d_ff), BF16)],
        compiler_params=_cparams(("arbitrary", "arbitrary")),
        name="ffn",
    )(*args)


def _conv_kernel(x_ref, mod_ref, g_ref, hist_ref, w1_ref, b1_ref, wdw_ref, bdw_ref, lng_ref,
                 lnb_ref, w2_ref, b2_ref, o_ref, st_ref, full_ref, y_ref, *, nb, t, n_tiles,
                 row_blk, lane_blk):
    c = w2_ref.shape[0]
    d = x_ref.shape[-1]
    first = HIST_PAD - (CONV_WIDTH - 1)

    @pl.when(pl.program_id(1) == 0)
    def _():
        full_ref[:, 0:HIST_PAD, :] = hist_ref[...]

    x = x_ref[...]
    h = _mod_norm(x, g_ref[...], mod_ref[0], mod_ref[1])
    a = _dot(h.reshape(nb * t, d).astype(BF16), w1_ref[...]) + b1_ref[...]
    u = a[:, :c] * jax.nn.sigmoid(a[:, c:])
    full_ref[:, HIST_PAD:HIST_PAD + t, :] = u.reshape(nb, t, c)

    for r0 in range(0, t, row_blk):
        for c0 in range(0, c, lane_blk):
            acc = jnp.zeros((nb, row_blk, lane_blk), F32)
            for w in range(CONV_WIDTH):
                acc = acc + (full_ref[:, first + w + r0:first + w + r0 + row_blk, c0:c0 + lane_blk]
                             * wdw_ref[w:w + 1, c0:c0 + lane_blk])
            y_ref[:, r0:r0 + row_blk, c0:c0 + lane_blk] = acc + bdw_ref[:, c0:c0 + lane_blk]

    y = y_ref[...].reshape(nb * t, c)
    mu = jnp.mean(y, axis=-1, keepdims=True)
    yc = y - mu
    var = jnp.mean(yc * yc, axis=-1, keepdims=True)
    yn = yc * lax.rsqrt(var + NORM_EPS) * lng_ref[...] + lnb_ref[...]
    out = _dot(jax.nn.silu(yn).astype(BF16), w2_ref[...]) + b2_ref[...]
    o_ref[...] = x + mod_ref[2] * out.reshape(nb, t, d)

    st_ref[...] = full_ref[:, t + first:t + HIST_PAD, :]
    if n_tiles > 1:
        full_ref[:, 0:HIST_PAD, :] = full_ref[:, t:t + HIST_PAD, :]


def _conv_mixer(x, mod, norm_g, hist, w1, b1, wdw, bdw, lng, lnb, w2, b2, *, layer, nb, t):
    nseq, s, d = x.shape
    c = w2.shape[0]
    n_tiles = s // t
    row_blk = min(t, 32)
    lane_blk = max(LANES, min(c, (16 * SUBLANES * LANES) // (nb * row_blk) // LANES * LANES))
    xspec = pl.BlockSpec((nb, t, d), lambda b, i: (b, i, 0))
    const = lambda b, i: (0, 0)
    hist32 = jnp.pad(hist, ((0, 0), (HIST_PAD - (CONV_WIDTH - 1), 0), (0, 0)))
    return pl.pallas_call(
        functools.partial(_conv_kernel, nb=nb, t=t, n_tiles=n_tiles, row_blk=row_blk,
                          lane_blk=lane_blk),
        grid=(nseq // nb, n_tiles),
        in_specs=[
            xspec,
            pl.BlockSpec((None, 3, nb, 1, d), lambda b, i: (layer, 1, b, 0, 0)),
            pl.BlockSpec((None, None, 1, d), lambda b, i: (layer, 1, 0, 0)),
            pl.BlockSpec((nb, HIST_PAD, c), lambda b, i: (b, 0, 0)),
            _resident((d, 2 * c), const),
            pl.BlockSpec((1, 2 * c), const),
            pl.BlockSpec((CONV_WIDTH, c), const),
            pl.BlockSpec((1, c), const),
            pl.BlockSpec((1, c), const),
            pl.BlockSpec((1, c), const),
            _resident((c, d), const),
            pl.BlockSpec((1, d), const),
        ],
        out_specs=[xspec, pl.BlockSpec((nb, CONV_WIDTH - 1, c), lambda b, i: (b, 0, 0))],
        out_shape=[jax.ShapeDtypeStruct(x.shape, F32),
                   jax.ShapeDtypeStruct((nseq, CONV_WIDTH - 1, c), F32)],
        scratch_shapes=[pltpu.VMEM((nb, HIST_PAD + t, c), F32), pltpu.VMEM((nb, t, c), F32)],
        compiler_params=_cparams(("arbitrary", "arbitrary")),
        name="conv_mixer",
    )(x, mod, norm_g, hist32, w1, b1.reshape(1, -1), wdw, bdw.reshape(1, -1), lng.reshape(1, -1),
      lnb.reshape(1, -1), w2, b2.reshape(1, -1))


def _qkv_kernel(x_ref, mod_ref, g_ref, w_ref, q_ref, k_ref, v_ref, *, nb, t):
    x = x_ref[...]
    d = x.shape[-1]
    h = _mod_norm(x, g_ref[...], mod_ref[0], mod_ref[1])
    hb = h.reshape(nb * t, d).astype(BF16)
    q = _dot(hb, w_ref[:, 0:d]) * (1.0 / math.sqrt(HEAD_DIM))
    q_ref[...] = q.reshape(nb, t, d).astype(q_ref.dtype)
    k_ref[...] = _dot(hb, w_ref[:, d:2 * d]).reshape(nb, t, d)
    v_ref[...] = _dot(hb, w_ref[:, 2 * d:3 * d]).reshape(nb, t, d)


def _qkv(x, mod, norm_g, w, *, layer, nb, t, q_dtype):
    nseq, s, d = x.shape
    xspec = pl.BlockSpec((nb, t, d), lambda b, i: (b, i, 0))
    return pl.pallas_call(
        functools.partial(_qkv_kernel, nb=nb, t=t),
        grid=(nseq // nb, s // t),
        in_specs=[
            xspec,
            pl.BlockSpec((None, 3, nb, 1, d), lambda b, i: (layer, 1, b, 0, 0)),
            pl.BlockSpec((None, None, 1, d), lambda b, i: (layer, 1, 0, 0)),
            _resident((d, 3 * d), lambda b, i: (0, 0)),
        ],
        out_specs=[xspec, xspec, xspec],
        out_shape=[jax.ShapeDtypeStruct(x.shape, q_dtype), jax.ShapeDtypeStruct(x.shape, F32),
                   jax.ShapeDtypeStruct(x.shape, F32)],
        compiler_params=_cparams(("arbitrary", "arbitrary")),
        name="qkv",
    )(x, mod, norm_g, w)


def _bucket_of_distance(n_dist):
    dist = np.arange(n_dist)
    max_exact = NUM_BUCKETS // 2
    nf = np.maximum(dist, max_exact).astype(np.float32)
    large = max_exact + (np.log(nf / max_exact) / math.log(MAX_DISTANCE / max_exact)
                         * (NUM_BUCKETS - max_exact)).astype(np.int32)
    large = np.minimum(large, NUM_BUCKETS - 1)
    return np.where(dist < max_exact, dist, large).astype(np.int32)


_FAR_DISTANCE = int(np.argmax(_bucket_of_distance(4 * MAX_DISTANCE) == NUM_BUCKETS - 1))
assert _FAR_DISTANCE <= MOBA_BLOCK


def _rank_select(gate, n_valid, axis):
    idx = lax.broadcasted_iota(jnp.int32, gate.shape, axis)
    cnt = jnp.zeros(gate.shape, F32)
    for j in range(n_valid):
        gj = lax.slice_in_dim(gate, j, j + 1, axis=axis)
        tie = jnp.where(idx > j, 1.0, 0.0)
        cnt = cnt + jnp.where(gj > gate, 1.0, jnp.where(gj == gate, tie, 0.0))
    return cnt < float(MOBA_TOP_K)


def _moba_prompt_kernel(far_ref, q_ref, k_ref, v_ref, bias_ref, o_ref, *, n_blk):
    pair = pl.program_id(0)
    blk = MOBA_BLOCK
    kf = k_ref[...]
    kb = kf.astype(BF16)
    vb = v_ref[...].astype(BF16)
    kmean = jnp.mean(kf.reshape(n_blk, blk, LANES), axis=1)
    kmean = jnp.concatenate([kmean, jnp.zeros((LANES - n_blk, LANES), F32)], axis=0).astype(BF16)
    lane = lax.broadcasted_iota(jnp.int32, (1, LANES), 1)
    for i in range(n_blk):
        qi = q_ref[i * blk:(i + 1) * blk, :]
        outs = []
        for hh in range(2):
            hmask = (lane // HEAD_DIM) == hh
            qh = jnp.where(hmask, qi, jnp.zeros_like(qi))
            far = far_ref[2 * pair + hh]
            if i > MOBA_TOP_K:
                sel = _rank_select(_dot_nt(qh, kmean), i, axis=1)
            s_list = []
            for j in range(i + 1):
                s = _dot_nt(qh, kb[j * blk:(j + 1) * blk, :])
                if j == i:
                    s = s + bias_ref[hh, 0]
                elif j == i - 1:
                    s = s + bias_ref[hh, 1]
                else:
                    s = s + far
                if i > MOBA_TOP_K and j < i:
                    s = jnp.where(sel[:, j:j + 1], s, -jnp.inf)
                s_list.append(s)
            m = functools.reduce(jnp.maximum, [jnp.max(s, axis=-1, keepdims=True) for s in s_list])
            l = jnp.zeros((blk, 1), F32)
            acc = jnp.zeros((blk, LANES), F32)
            for j, s in enumerate(s_list):
                p = jnp.exp(s - m)
                l = l + jnp.sum(p, axis=-1, keepdims=True)
                acc = acc + _dot(p.astype(BF16), vb[j * blk:(j + 1) * blk, :])
            outs.append(acc * (1.0 / l))
        o_ref[i * blk:(i + 1) * blk, :] = jnp.where((lane // HEAD_DIM) == 0, outs[0],
                                                    outs[1]).astype(o_ref.dtype)


def _moba_prompt(q, k, v, rel_bias):
    b, s, d = q.shape
    n_heads = d // HEAD_DIM
    blk = MOBA_BLOCK
    n_blk = s // blk
    assert s % blk == 0 and n_heads % 2 == 0
    table = rel_bias.T[:, _bucket_of_distance(2 * blk)]
    qq = np.arange(blk)[:, None]
    kk = np.arange(blk)[None, :]
    own = jnp.where(jnp.asarray(kk <= qq), table[:, np.maximum(qq - kk, 0)], -jnp.inf)
    prev = table[:, blk + qq - kk]
    bias = jnp.stack([own, prev], axis=1)
    far = rel_bias[NUM_BUCKETS - 1]
    hspec = pl.BlockSpec((None, s, LANES), lambda p, bi, far_ref: (bi, 0, p))
    return pl.pallas_call(
        functools.partial(_moba_prompt_kernel, n_blk=n_blk),
        grid_spec=pltpu.PrefetchScalarGridSpec(
            num_scalar_prefetch=1,
            grid=(n_heads // 2, b),
            in_specs=[hspec, hspec, hspec,
                      pl.BlockSpec((2, 2, blk, blk), lambda p, bi, far_ref: (p, 0, 0, 0))],
            out_specs=hspec,
        ),
        out_shape=jax.ShapeDtypeStruct((b, s, d), BF16),
        compiler_params=_cparams(("arbitrary", "arbitrary")),
        name="moba_prompt",
    )(far, q, k, v, bias)


def _moba_sample_kernel(pt_ref, *refs, n_groups, n_blk, n_heads, t):
    pp = PAGES_PER_STEP
    k_refs = refs[0:pp]
    v_refs = refs[pp:2 * pp]
    (q_ref, kn_ref, vn_ref, blast_ref, bfar_ref, bown_ref, o_ref,
     a_ref, s_ref, p_ref, l_ref, acc_ref) = refs[2 * pp:]
    del pt_ref
    step = pl.program_id(1)
    blk = MOBA_BLOCK
    d = q_ref.shape[-1]
    past = n_blk * blk
    cols = n_heads * t
    row_head = lax.broadcasted_iota(jnp.int32, (cols, d), 0) // t
    lane_head = lax.broadcasted_iota(jnp.int32, (cols, d), 1) // HEAD_DIM
    diag = row_head == lane_head

    def new_rows(ref):
        return jnp.concatenate([ref[...], jnp.zeros((PAGE_SIZE - t, ref.shape[-1]), F32)],
                               axis=0).astype(BF16)

    @pl.when(step == 0)
    def _():
        qt = jnp.tile(q_ref[...], (n_heads, 1))
        a_ref[...] = jnp.where(diag, qt, 0.0).astype(BF16)
        acc_ref[...] = jnp.zeros_like(acc_ref)

    @pl.when(step < n_groups)
    def _():
        for r in range(pp):
            row0 = pl.multiple_of((step * pp + r) * PAGE_SIZE, PAGE_SIZE)
            s_ref[pl.ds(row0, PAGE_SIZE), :] = _dot_nt(k_refs[r][...].astype(BF16), a_ref[...])

    @pl.when(step == n_groups - 1)
    def _():
        gate = jnp.concatenate(
            [jnp.sum(s_ref[n * blk:(n + 1) * blk, :], axis=0, keepdims=True) for n in range(n_blk)],
            axis=0)
        sel = _rank_select(gate, n_blk, axis=0)
        s_own = _dot_nt(new_rows(kn_ref), a_ref[...])[0:t, :] + bown_ref[...]
        m = jnp.max(s_own, axis=0, keepdims=True)
        for n in range(n_blk):
            bias = blast_ref[...] if n == n_blk - 1 else bfar_ref[...]
            sn = jnp.where(sel[n:n + 1, :], s_ref[n * blk:(n + 1) * blk, :] + bias, -jnp.inf)
            s_ref[n * blk:(n + 1) * blk, :] = sn
            m = jnp.maximum(m, jnp.max(sn, axis=0, keepdims=True))
        p_own = jnp.exp(s_own - m)
        l = jnp.sum(p_own, axis=0, keepdims=True)
        for n in range(n_blk):
            p = jnp.exp(s_ref[n * blk:(n + 1) * blk, :] - m)
            l = l + jnp.sum(p, axis=0, keepdims=True)
            p_ref[n * blk:(n + 1) * blk, :] = p.astype(BF16)
        p_ref[past:past + PAGE_SIZE, :] = jnp.concatenate(
            [p_own, jnp.zeros((PAGE_SIZE - t, cols), F32)], axis=0).astype(BF16)
        l_ref[...] = l

    @pl.when(step >= n_groups)
    def _():
        for r in range(pp):
            row0 = pl.multiple_of(((step - n_groups) * pp + r) * PAGE_SIZE, PAGE_SIZE)
            acc_ref[...] += _dot_tn(p_ref[pl.ds(row0, PAGE_SIZE), :], v_refs[r][...].astype(BF16))

    @pl.when(step == 2 * n_groups - 1)
    def _():
        acc = acc_ref[...] + _dot_tn(p_ref[past:past + PAGE_SIZE, :], new_rows(vn_ref))
        l_t = jnp.transpose(jnp.broadcast_to(l_ref[...], (cols, cols)))
        acc = acc * jnp.tile(1.0 / l_t, (1, d // cols))
        acc = jnp.where(diag, acc, 0.0)
        o_ref[...] = jnp.sum(acc.reshape(n_heads, t, d), axis=0).astype(o_ref.dtype)


def _moba_sample(q, k_new, v_new, cache_k, cache_v, page_table, rel_bias):
    nseq, t, d = q.shape
    n_heads = d // HEAD_DIM
    n_pages = page_table.shape[1]
    past = n_pages * PAGE_SIZE
    blk = MOBA_BLOCK
    n_blk = past // blk
    pp = PAGES_PER_STEP
    cols = n_heads * t
    assert past % blk == 0 and n_pages % pp == 0 and cols == LANES and t <= blk
    n_groups = n_pages // pp

    table = rel_bias.T[:, _bucket_of_distance(blk + t + 1)]
    ti = np.arange(t)
    kk = np.arange(blk)
    dist_last = (blk + ti[None, :] - kk[:, None])
    b_last = jnp.transpose(table[:, dist_last], (1, 0, 2)).reshape(blk, cols)
    b_far = jnp.repeat(rel_bias[NUM_BUCKETS - 1], t).reshape(1, cols)
    dist_own = ti[None, :] - ti[:, None]
    b_own = jnp.where(jnp.asarray(dist_own >= 0)[:, None, :],
                      jnp.transpose(table[:, np.maximum(dist_own, 0)], (1, 0, 2)), -jnp.inf)
    b_own = b_own.reshape(t, cols)

    def kmap(r):
        return lambda b, s, pt: (pt[b * n_pages + jnp.minimum(s, n_groups - 1) * pp + r], 0, 0)

    def vmap_(r):
        return lambda b, s, pt: (pt[b * n_pages + jnp.maximum(s - n_groups, 0) * pp + r], 0, 0)

    seq = pl.BlockSpec((None, t, d), lambda b, s, pt: (b, 0, 0))
    const = lambda b, s, pt: (0, 0)
    in_specs = ([pl.BlockSpec((None, PAGE_SIZE, d), kmap(r)) for r in range(pp)]
                + [pl.BlockSpec((None, PAGE_SIZE, d), vmap_(r)) for r in range(pp)]
                + [seq, seq, seq,
                   pl.BlockSpec((blk, cols), const), pl.BlockSpec((1, cols), const),
                   pl.BlockSpec((t, cols), const)])
    return pl.pallas_call(
        functools.partial(_moba_sample_kernel, n_groups=n_groups, n_blk=n_blk, n_heads=n_heads, t=t),
        grid_spec=pltpu.PrefetchScalarGridSpec(
            num_scalar_prefetch=1,
            grid=(nseq, 2 * n_groups),
            in_specs=in_specs,
            out_specs=seq,
            scratch_shapes=[
                pltpu.VMEM((cols, d), BF16),
                pltpu.VMEM((past, cols), F32),
                pltpu.VMEM((past + PAGE_SIZE, cols), BF16),
                pltpu.VMEM((1, cols), F32),
                pltpu.VMEM((cols, d), F32),
            ],
        ),
        out_shape=jax.ShapeDtypeStruct((nseq, t, d), BF16),
        compiler_params=_cparams(("arbitrary", "arbitrary")),
        name="moba_sample",
    )(page_table.reshape(-1), *([cache_k] * pp), *([cache_v] * pp), q, k_new, v_new,
      b_last, b_far, b_own)


def _tile_rows(s):
    for t in (512, 256, 128, 64, 32, 16, 8):
        if s % t == 0:
            return t
    raise ValueError(f"sequence length {s} is not a multiple of {SUBLANES}")


def _trunk(x, mod, hist, past, page_table, w, *, nb, t, nb_conv):
    nseq, s, d = x.shape
    n_heads = d // HEAD_DIM
    depth = w["norm_g"].shape[0]
    hists, new_k, new_v = [], [], []
    attn = None
    for layer in range(depth):
        jm = layer // 2
        ffn = functools.partial(_ffn, mod=mod, norm_g=w["norm_g"], layer=layer, nb=nb, t=t)
        x = ffn(x, wg=w["wg"][layer, 0], wu=w["wu"][layer, 0], wd=w["wd"][layer, 0], sub=0)
        if layer % 2 == 0:
            x, st = _conv_mixer(x, mod, w["norm_g"], hist[jm], w["pw1"][jm], w["b_pw1"][jm],
                                w["dw"][jm], w["b_dw"][jm], w["ln_g"][jm], w["ln_b"][jm],
                                w["pw2"][jm], w["b_pw2"][jm], layer=layer, nb=nb_conv, t=t)
            hists.append(st)
            wo = None
        else:
            q, k, v = _qkv(x, mod, w["norm_g"], w["qkv"][jm], layer=layer, nb=nb, t=t,
                           q_dtype=BF16 if past is None else F32)
            if past is None:
                attn = _moba_prompt(q, k, v, w["rel_bias"])
            else:
                n_phys = past[0].shape[1]
                ck = past[0][jm].reshape(n_phys, PAGE_SIZE, d)
                cv = past[1][jm].reshape(n_phys, PAGE_SIZE, d)
                attn = _moba_sample(q, k, v, ck, cv, page_table, w["rel_bias"])
            new_k.append(k.reshape(nseq, s, n_heads, HEAD_DIM))
            new_v.append(v.reshape(nseq, s, n_heads, HEAD_DIM))
            wo = w["wo"][jm]
        last = layer == depth - 1
        x = ffn(x, wg=w["wg"][layer, 1], wu=w["wu"][layer, 1], wd=w["wd"][layer, 1], sub=2,
                attn=attn if wo is not None else None, wo=wo,
                final_g=w["final_g"] if last else None)
    return x, jnp.stack(hists), jnp.stack(new_k), jnp.stack(new_v)


def kernel(x_prompt, x_sample, state_conv, cache_k, cache_v, page_table, c_prompt, c_sample, norm_g, ada_w, ada_b, ffn_w_gate, ffn_w_up, ffn_w_down, conv_w_pw1, conv_b_pw1, conv_w_dw, conv_b_dw, conv_ln_g, conv_ln_b, conv_w_pw2, conv_b_pw2, attn_w_qkv, attn_w_o, rel_bias, final_norm_g):
    b, s, d = x_prompt.shape
    nd, ds_, _ = x_sample.shape
    depth = norm_g.shape[0]
    w = dict(
        norm_g=norm_g.reshape(depth, 3, 1, d),
        wg=ffn_w_gate.astype(BF16), wu=ffn_w_up.astype(BF16), wd=ffn_w_down.astype(BF16),
        pw1=conv_w_pw1.astype(BF16), b_pw1=conv_b_pw1, dw=conv_w_dw, b_dw=conv_b_dw,
        ln_g=conv_ln_g, ln_b=conv_ln_b, pw2=conv_w_pw2.astype(BF16), b_pw2=conv_b_pw2,
        qkv=attn_w_qkv.astype(BF16), wo=attn_w_o.astype(BF16), rel_bias=rel_bias,
        final_g=final_norm_g,
    )
    mod = _ada_mod(jnp.concatenate([c_prompt, c_sample], axis=0), ada_w, ada_b)
    n_mod = mod.shape[1]
    mod_p = mod[:, :, :b].reshape(depth, n_mod, b, 1, d)
    mod_s = mod[:, :, b:].reshape(depth, n_mod, nd, 1, d)

    hist0 = jnp.zeros((conv_w_dw.shape[0], b, CONV_WIDTH - 1, conv_w_dw.shape[2]), F32)
    y_p, st_p, k_p, v_p = _trunk(x_prompt, mod_p, hist0, None, None, w,
                                 nb=1, t=_tile_rows(s), nb_conv=1)
    y_s, st_s, k_s, v_s = _trunk(x_sample, mod_s, state_conv, (cache_k, cache_v), page_table, w,
                                 nb=nd, t=ds_, nb_conv=SUBLANES)
    return (y_p, y_s, st_p, st_s, k_p, v_p, k_s, v_s)
```

```python
import functools
import math

import numpy as np
import jax
import jax.numpy as jnp
from jax import lax
from jax.experimental import pallas as pl
from jax.experimental.pallas import tpu as pltpu

F32 = jnp.float32
BF16 = jnp.bfloat16

HEAD_DIM = 64
CONV_WIDTH = 31
MOBA_BLOCK = 256
MOBA_TOP_K = 3
PAGE_SIZE = 128
NUM_BUCKETS = 32
MAX_DISTANCE = 128
NORM_EPS = 1e-6

LANES = 128
SUBLANES = 8
HIST_PAD = 32
VMEM_LIMIT = 56 * 1024 * 1024
PAGES_PER_STEP = 8


def _cparams(sem):
    return pltpu.CompilerParams(dimension_semantics=sem, vmem_limit_bytes=VMEM_LIMIT)


def _resident(block_shape, index_map):
    return pl.BlockSpec(block_shape, index_map, pipeline_mode=pl.Buffered(1))


def _mod_norm(x, g, shift, scale):
    ms = jnp.mean(x * x, axis=-1, keepdims=True)
    n = x * lax.rsqrt(ms + NORM_EPS)
    return n * g * (1.0 + scale) + shift


def _dot(a, b):
    return jnp.dot(a, b, preferred_element_type=F32)


def _dot_nt(a, b):
    return lax.dot_general(a, b, (((1,), (1,)), ((), ())), preferred_element_type=F32)


def _dot_tn(a, b):
    return lax.dot_general(a, b, (((0,), (0,)), ((), ())), preferred_element_type=F32)


def _mod_kernel(cp_ref, cs_ref, w_ref, b_ref, op_ref, os_ref):
    w = w_ref[...].astype(BF16)
    op_ref[...] = _dot(jax.nn.silu(cp_ref[...]).astype(BF16), w) + b_ref[...]
    os_ref[...] = _dot(jax.nn.silu(cs_ref[...]).astype(BF16), w) + b_ref[...]


def _ada_mod(c_prompt, c_sample, ada_w, ada_b):
    d = c_prompt.shape[1]
    depth = ada_w.shape[0]
    n_mod = ada_w.shape[2] // d
    outs = pl.pallas_call(
        _mod_kernel,
        grid=(depth, n_mod),
        in_specs=[
            pl.BlockSpec(c_prompt.shape, lambda l, j: (0, 0)),
            pl.BlockSpec(c_sample.shape, lambda l, j: (0, 0)),
            pl.BlockSpec((None, d, d), lambda l, j: (l, 0, j)),
            pl.BlockSpec((None, None, 1, d), lambda l, j: (l, j, 0, 0)),
        ],
        out_specs=[pl.BlockSpec((None, None) + c.shape, lambda l, j: (l, j, 0, 0))
                   for c in (c_prompt, c_sample)],
        out_shape=[jax.ShapeDtypeStruct((depth, n_mod) + c.shape, F32)
                   for c in (c_prompt, c_sample)],
        compiler_params=_cparams(("arbitrary", "arbitrary")),
        name="ada_mod",
    )(c_prompt, c_sample, ada_w, ada_b.reshape(depth, n_mod, 1, d))
    return [o.reshape(depth, n_mod, o.shape[2], 1, d) for o in outs]


def _ffn_kernel(*refs, nb, t, ff_chunk, pre_attn, final_norm):
    it = iter(refs)
    x_ref, mod_ref, g_ref = next(it), next(it), next(it)
    if pre_attn:
        a_ref, wo_ref, gate1_ref = next(it), next(it), next(it)
    wg_ref, wu_ref, wd_ref = next(it), next(it), next(it)
    if final_norm:
        fg_ref = next(it)
    o_ref, act_ref = next(it), next(it)

    x = x_ref[...]
    d = x.shape[-1]
    d_ff = wg_ref.shape[-1]
    if pre_attn:
        y = _dot(a_ref[...].reshape(nb * t, d), wo_ref[...])
        x = x + gate1_ref[0] * y.reshape(nb, t, d)
    h = _mod_norm(x, g_ref[...], mod_ref[0], mod_ref[1])
    hb = h.reshape(nb * t, d).astype(BF16)
    for j in range(d_ff // ff_chunk):
        sl = slice(j * ff_chunk, (j + 1) * ff_chunk)
        gj = _dot(hb, wg_ref[:, sl])
        uj = _dot(hb, wu_ref[:, sl])
        act_ref[:, sl] = (jax.nn.silu(gj) * uj).astype(BF16)
    y = _dot(act_ref[...], wd_ref[...])
    out = x + 0.5 * mod_ref[2] * y.reshape(nb, t, d)
    if final_norm:
        ms = jnp.mean(out * out, axis=-1, keepdims=True)
        out = out * lax.rsqrt(ms + NORM_EPS) * fg_ref[...]
    o_ref[...] = out


def _ffn(x, mod, norm_g, wg, wu, wd, *, layer, sub, nb, t, attn=None, wo=None, wo_idx=0,
         final_g=None):
    nseq, s, d = x.shape
    d_ff = wg.shape[-1]
    half = sub // 2
    pre_attn = attn is not None
    final_norm = final_g is not None
    grid = (nseq // nb, s // t)
    xspec = pl.BlockSpec((nb, t, d), lambda b, i: (b, i, 0))
    in_specs = [
        xspec,
        pl.BlockSpec((None, 3, nb, 1, d), lambda b, i: (layer, sub, b, 0, 0)),
        pl.BlockSpec((None, None, 1, d), lambda b, i: (layer, sub, 0, 0)),
    ]
    args = [x, mod, norm_g]
    if pre_attn:
        in_specs += [
            xspec,
            _resident((None, d, d), lambda b, i: (wo_idx, 0, 0)),
            pl.BlockSpec((None, 1, nb, 1, d), lambda b, i: (layer, 3 * (sub - 1) + 2, b, 0, 0)),
        ]
        args += [attn, wo, mod]
    in_specs += [
        _resident((None, None, d, d_ff), lambda b, i: (layer, half, 0, 0)),
        _resident((None, None, d, d_ff), lambda b, i: (layer, half, 0, 0)),
        _resident((None, None, d_ff, d), lambda b, i: (layer, half, 0, 0)),
    ]
    args += [wg, wu, wd]
    if final_norm:
        in_specs.append(pl.BlockSpec((1, d), lambda b, i: (0, 0)))
        args.append(final_g.reshape(1, d))
    ff_chunk = 256 if d_ff % 256 == 0 else LANES
    return pl.pallas_call(
        functools.partial(_ffn_kernel, nb=nb, t=t, ff_chunk=ff_chunk, pre_attn=pre_attn,
                          final_norm=final_norm),
        grid=grid,
        in_specs=in_specs,
        out_specs=xspec,
        out_shape=jax.ShapeDtypeStruct(x.shape, F32),
        scratch_shapes=[pltpu.VMEM((nb * t, d_ff), BF16)],
        compiler_params=_cparams(("arbitrary", "arbitrary")),
        name="ffn",
    )(*args)


def _conv_kernel(x_ref, mod_ref, g_ref, hist_ref, w1_ref, b1_ref, wdw_ref, bdw_ref, lng_ref,
                 lnb_ref, w2_ref, b2_ref, o_ref, st_ref, full_ref, y_ref, shift_ref, *, nb, t,
                 n_tiles, row_blk, lane_blk):
    c = w2_ref.shape[0]
    d = x_ref.shape[-1]
    first = HIST_PAD - (CONV_WIDTH - 1)
    shift_rows = shift_ref.shape[2]

    @pl.when(pl.program_id(1) == 0)
    def _():
        full_ref[:, 0:HIST_PAD, :] = hist_ref[...]

    x = x_ref[...]
    h = _mod_norm(x, g_ref[...], mod_ref[0], mod_ref[1])
    a = _dot(h.reshape(nb * t, d).astype(BF16), w1_ref[...]) + b1_ref[...]
    u = a[:, :c] * jax.nn.sigmoid(a[:, c:])
    full_ref[:, HIST_PAD:HIST_PAD + t, :] = u.reshape(nb, t, c)

    for c0 in range(0, c, lane_blk):
        lanes = slice(c0, c0 + lane_blk)
        for s in range(1, SUBLANES):
            shift_ref[s - 1] = full_ref[:, s:s + shift_rows, lanes]
        for r0 in range(0, t, row_blk):
            acc = jnp.zeros((nb, row_blk, lane_blk), F32)
            for w in range(CONV_WIDTH):
                base, s = divmod(first + w, SUBLANES)
                rows = slice(base * SUBLANES + r0, base * SUBLANES + r0 + row_blk)
                src = full_ref[:, rows, lanes] if s == 0 else shift_ref[s - 1, :, rows, :]
                acc = acc + src * wdw_ref[w:w + 1, lanes]
            y_ref[:, r0:r0 + row_blk, lanes] = acc + bdw_ref[:, lanes]

    y = y_ref[...].reshape(nb * t, c)
    mu = jnp.mean(y, axis=-1, keepdims=True)
    yc = y - mu
    var = jnp.mean(yc * yc, axis=-1, keepdims=True)
    yn = yc * lax.rsqrt(var + NORM_EPS) * lng_ref[...] + lnb_ref[...]
    out = _dot(jax.nn.silu(yn).astype(BF16), w2_ref[...]) + b2_ref[...]
    o_ref[...] = x + mod_ref[2] * out.reshape(nb, t, d)

    st_ref[...] = full_ref[:, t + first:t + HIST_PAD, :]
    if n_tiles > 1:
        full_ref[:, 0:HIST_PAD, :] = full_ref[:, t:t + HIST_PAD, :]


def _conv_mixer(x, mod, norm_g, hist, w1, b1, wdw, bdw, lng, lnb, w2, b2, *, layer, jm, nb, t):
    nseq, s, d = x.shape
    c = w2.shape[1]
    n_tiles = s // t
    acc_vregs = 16
    row_blk = min(t, 64)
    lane_blk = max(LANES, min(c, acc_vregs * SUBLANES * LANES // (nb * row_blk) // LANES * LANES))
    xspec = pl.BlockSpec((nb, t, d), lambda b, i: (b, i, 0))
    const = lambda b, i: (jm, 0, 0)
    vec = lambda a: a.reshape(a.shape[0], 1, a.shape[1])
    hist32 = jnp.pad(hist, ((0, 0), (HIST_PAD - (CONV_WIDTH - 1), 0), (0, 0)))
    return pl.pallas_call(
        functools.partial(_conv_kernel, nb=nb, t=t, n_tiles=n_tiles, row_blk=row_blk,
                          lane_blk=lane_blk),
        grid=(nseq // nb, n_tiles),
        in_specs=[
            xspec,
            pl.BlockSpec((None, 3, nb, 1, d), lambda b, i: (layer, 1, b, 0, 0)),
            pl.BlockSpec((None, None, 1, d), lambda b, i: (layer, 1, 0, 0)),
            pl.BlockSpec((nb, HIST_PAD, c), lambda b, i: (b, 0, 0)),
            _resident((None, d, 2 * c), const),
            pl.BlockSpec((None, 1, 2 * c), const),
            pl.BlockSpec((None, CONV_WIDTH, c), const),
            pl.BlockSpec((None, 1, c), const),
            pl.BlockSpec((None, 1, c), const),
            pl.BlockSpec((None, 1, c), const),
            _resident((None, c, d), const),
            pl.BlockSpec((None, 1, d), const),
        ],
        out_specs=[xspec, pl.BlockSpec((nb, CONV_WIDTH - 1, c), lambda b, i: (b, 0, 0))],
        out_shape=[jax.ShapeDtypeStruct(x.shape, F32),
                   jax.ShapeDtypeStruct((nseq, CONV_WIDTH - 1, c), F32)],
        scratch_shapes=[pltpu.VMEM((nb, HIST_PAD + t, c), F32), pltpu.VMEM((nb, t, c), F32),
                        pltpu.VMEM((SUBLANES - 1, nb, HIST_PAD - SUBLANES + t, lane_blk), F32)],
        compiler_params=_cparams(("arbitrary", "arbitrary")),
        name="conv_mixer",
    )(x, mod, norm_g, hist32, w1, vec(b1), wdw, vec(bdw), vec(lng), vec(lnb), w2, vec(b2))


def _qkv_kernel(x_ref, mod_ref, g_ref, w_ref, q_ref, k_ref, v_ref, *, nb, t):
    x = x_ref[...]
    d = x.shape[-1]
    h = _mod_norm(x, g_ref[...], mod_ref[0], mod_ref[1])
    hb = h.reshape(nb * t, d).astype(BF16)
    q = _dot(hb, w_ref[:, 0:d]) * (1.0 / math.sqrt(HEAD_DIM))
    q_ref[...] = q.reshape(nb, t, d).astype(q_ref.dtype)
    k_ref[...] = _dot(hb, w_ref[:, d:2 * d]).reshape(nb, t, d)
    v_ref[...] = _dot(hb, w_ref[:, 2 * d:3 * d]).reshape(nb, t, d)


def _qkv(x, mod, norm_g, w, *, layer, jm, nb, t, q_dtype):
    nseq, s, d = x.shape
    xspec = pl.BlockSpec((nb, t, d), lambda b, i: (b, i, 0))
    return pl.pallas_call(
        functools.partial(_qkv_kernel, nb=nb, t=t),
        grid=(nseq // nb, s // t),
        in_specs=[
            xspec,
            pl.BlockSpec((None, 3, nb, 1, d), lambda b, i: (layer, 1, b, 0, 0)),
            pl.BlockSpec((None, None, 1, d), lambda b, i: (layer, 1, 0, 0)),
            _resident((None, d, 3 * d), lambda b, i: (jm, 0, 0)),
        ],
        out_specs=[xspec, xspec, xspec],
        out_shape=[jax.ShapeDtypeStruct(x.shape, q_dtype), jax.ShapeDtypeStruct(x.shape, F32),
                   jax.ShapeDtypeStruct(x.shape, F32)],
        compiler_params=_cparams(("arbitrary", "arbitrary")),
        name="qkv",
    )(x, mod, norm_g, w)


def _bucket_of_distance(n_dist):
    dist = np.arange(n_dist)
    max_exact = NUM_BUCKETS // 2
    nf = np.maximum(dist, max_exact).astype(np.float32)
    large = max_exact + (np.log(nf / max_exact) / math.log(MAX_DISTANCE / max_exact)
                         * (NUM_BUCKETS - max_exact)).astype(np.int32)
    large = np.minimum(large, NUM_BUCKETS - 1)
    return np.where(dist < max_exact, dist, large).astype(np.int32)


_BUCKETS = _bucket_of_distance(4 * MAX_DISTANCE)
assert _BUCKETS[-1] == NUM_BUCKETS - 1 and np.all(np.diff(_BUCKETS) >= 0)
_BUCKET_START = [int(np.argmax(_BUCKETS >= b)) for b in range(NUM_BUCKETS)]
assert _BUCKET_START[NUM_BUCKETS - 1] <= MOBA_BLOCK


def _bias_of_distance(dist, table):
    bias = jnp.full(dist.shape, table[0], F32)
    for b in range(1, NUM_BUCKETS):
        bias = jnp.where(dist >= _BUCKET_START[b], table[b], bias)
    return bias


def _rank_select(gate, n_valid, axis):
    idx = lax.broadcasted_iota(jnp.int32, gate.shape, axis)
    cnt = jnp.zeros(gate.shape, F32)
    for j in range(n_valid):
        gj = lax.slice_in_dim(gate, j, j + 1, axis=axis)
        tie = jnp.where(idx > j, 1.0, 0.0)
        cnt = cnt + jnp.where(gj > gate, 1.0, jnp.where(gj == gate, tie, 0.0))
    return cnt < float(MOBA_TOP_K)


def _moba_prompt_kernel(rb_ref, q_ref, k_ref, v_ref, o_ref, bias_ref, *, n_blk, n_heads):
    pair = pl.program_id(0)
    blk = MOBA_BLOCK

    @pl.when(pl.program_id(1) == 0)
    def _():
        dist = (lax.broadcasted_iota(jnp.int32, (blk, blk), 0)
                - lax.broadcasted_iota(jnp.int32, (blk, blk), 1))
        for hh in range(2):
            table = [rb_ref[b * n_heads + 2 * pair + hh] for b in range(NUM_BUCKETS)]
            bias_ref[hh, 0] = jnp.where(dist >= 0, _bias_of_distance(dist, table), -jnp.inf)
            bias_ref[hh, 1] = _bias_of_distance(dist + blk, table)

    kf = k_ref[...]
    kb = kf.astype(BF16)
    vb = v_ref[...].astype(BF16)
    kmean = jnp.mean(kf.reshape(n_blk, blk, LANES), axis=1)
    kmean = jnp.concatenate([kmean, jnp.zeros((LANES - n_blk, LANES), F32)], axis=0).astype(BF16)
    lane = lax.broadcasted_iota(jnp.int32, (1, LANES), 1)
    for i in range(n_blk):
        qi = q_ref[i * blk:(i + 1) * blk, :]
        outs = []
        for hh in range(2):
            hmask = (lane // HEAD_DIM) == hh
            qh = jnp.where(hmask, qi, jnp.zeros_like(qi))
            far = rb_ref[(NUM_BUCKETS - 1) * n_heads + 2 * pair + hh]
            if i > MOBA_TOP_K:
                sel = _rank_select(_dot_nt(qh, kmean), i, axis=1)
            s_list = []
            for j in range(i + 1):
                s = _dot_nt(qh, kb[j * blk:(j + 1) * blk, :])
                if j == i:
                    s = s + bias_ref[hh, 0]
                elif j == i - 1:
                    s = s + bias_ref[hh, 1]
                else:
                    s = s + far
                if i > MOBA_TOP_K and j < i:
                    s = jnp.where(sel[:, j:j + 1], s, -jnp.inf)
                s_list.append(s)
            m = functools.reduce(jnp.maximum, [jnp.max(s, axis=-1, keepdims=True) for s in s_list])
            l = jnp.zeros((blk, 1), F32)
            acc = jnp.zeros((blk, LANES), F32)
            for j, s in enumerate(s_list):
                p = jnp.exp(s - m)
                l = l + jnp.sum(p, axis=-1, keepdims=True)
                acc = acc + _dot(p.astype(BF16), vb[j * blk:(j + 1) * blk, :])
            outs.append(acc * (1.0 / l))
        o_ref[i * blk:(i + 1) * blk, :] = jnp.where((lane // HEAD_DIM) == 0, outs[0],
                                                    outs[1]).astype(o_ref.dtype)


def _moba_prompt(q, k, v, rel_bias):
    b, s, d = q.shape
    n_heads = d // HEAD_DIM
    blk = MOBA_BLOCK
    n_blk = s // blk
    assert s % blk == 0 and n_heads % 2 == 0 and 2 * HEAD_DIM == LANES
    hspec = pl.BlockSpec((None, s, LANES), lambda p, bi, rb_ref: (bi, 0, p))
    return pl.pallas_call(
        functools.partial(_moba_prompt_kernel, n_blk=n_blk, n_heads=n_heads),
        grid_spec=pltpu.PrefetchScalarGridSpec(
            num_scalar_prefetch=1,
            grid=(n_heads // 2, b),
            in_specs=[hspec, hspec, hspec],
            out_specs=hspec,
            scratch_shapes=[pltpu.VMEM((2, 2, blk, blk), F32)],
        ),
        out_shape=jax.ShapeDtypeStruct((b, s, d), BF16),
        compiler_params=_cparams(("arbitrary", "arbitrary")),
        name="moba_prompt",
    )(rel_bias.reshape(-1), q, k, v)


def _moba_sample_kernel(pt_ref, rb_ref, *refs, n_groups, n_blk, n_heads, t):
    pp = PAGES_PER_STEP
    k_refs = refs[0:pp]
    v_refs = refs[pp:2 * pp]
    (q_ref, kn_ref, vn_ref, o_ref, a_ref, s_ref, p_ref, pown_ref, l_ref, acc_ref,
     blast_ref, bfar_ref, bown_ref) = refs[2 * pp:]
    del pt_ref
    seq = pl.program_id(0)
    step = pl.program_id(1)
    blk = MOBA_BLOCK
    ppb = blk // PAGE_SIZE
    d = q_ref.shape[-1]
    cols = n_heads * t
    row_head = lax.broadcasted_iota(jnp.int32, (cols, d), 0) // t
    lane_head = lax.broadcasted_iota(jnp.int32, (cols, d), 1) // HEAD_DIM
    diag = row_head == lane_head

    def new_rows(ref):
        return jnp.concatenate([ref[...], jnp.zeros((PAGE_SIZE - t, ref.shape[-1]), F32)],
                               axis=0).astype(BF16)

    def block_of(page_refs, r):
        return jnp.concatenate([page_refs[ppb * r + i][...] for i in range(ppb)],
                               axis=1).astype(BF16)

    @pl.when(jnp.logical_and(seq == 0, step == 0))
    def _():
        qi = lax.broadcasted_iota(jnp.int32, (t, blk), 0)
        ki = lax.broadcasted_iota(jnp.int32, (t, blk), 1)
        qo = lax.broadcasted_iota(jnp.int32, (t, PAGE_SIZE), 0)
        ko = lax.broadcasted_iota(jnp.int32, (t, PAGE_SIZE), 1)
        for h in range(n_heads):
            table = [rb_ref[b * n_heads + h] for b in range(NUM_BUCKETS)]
            rows = slice(h * t, (h + 1) * t)
            blast_ref[rows, :] = _bias_of_distance(blk + qi - ki, table)
            bfar_ref[rows, :] = jnp.full((t, blk), table[NUM_BUCKETS - 1], F32)
            bown_ref[rows, :] = jnp.where(ko <= qo, _bias_of_distance(qo - ko, table), -jnp.inf)

    @pl.when(step == 0)
    def _():
        qt = jnp.tile(q_ref[...], (n_heads, 1))
        a_ref[...] = jnp.where(diag, qt, 0.0).astype(BF16)
        acc_ref[...] = jnp.zeros_like(acc_ref)

    @pl.when(step < n_groups)
    def _():
        a = a_ref[...]
        for r in range(pp // ppb):
            s_ref[step * (pp // ppb) + r] = _dot(a, block_of(k_refs, r))

    @pl.when(step == n_groups - 1)
    def _():
        lane = lax.broadcasted_iota(jnp.int32, (cols, LANES), 1)
        gate = jnp.zeros((cols, LANES), F32)
        for n in range(n_blk):
            gate = jnp.where(lane == n, jnp.sum(s_ref[n], axis=1, keepdims=True), gate)
        sel = _rank_select(gate, n_blk, axis=1)
        s_own = _dot_nt(a_ref[...], new_rows(kn_ref)) + bown_ref[...]
        m = jnp.max(s_own, axis=1, keepdims=True)
        for n in range(n_blk):
            bias = blast_ref[...] if n == n_blk - 1 else bfar_ref[...]
            sn = jnp.where(sel[:, n:n + 1], s_ref[n] + bias, -jnp.inf)
            s_ref[n] = sn
            m = jnp.maximum(m, jnp.max(sn, axis=1, keepdims=True))
        p_own = jnp.exp(s_own - m)
        l = jnp.sum(p_own, axis=1, keepdims=True)
        for n in range(n_blk):
            p = jnp.exp(s_ref[n] - m)
            l = l + jnp.sum(p, axis=1, keepdims=True)
            p_ref[n] = p.astype(BF16)
        pown_ref[...] = p_own.astype(BF16)
        l_ref[...] = jnp.broadcast_to(l, (cols, LANES))

    @pl.when(step >= n_groups)
    def _():
        acc = acc_ref[...]
        for r in range(pp // ppb):
            acc = acc + _dot_nt(p_ref[(step - n_groups) * (pp // ppb) + r], block_of(v_refs, r))
        acc_ref[...] = acc

    @pl.when(step == 2 * n_groups - 1)
    def _():
        acc = acc_ref[...] + _dot(pown_ref[...], new_rows(vn_ref))
        acc = acc * jnp.tile(1.0 / l_ref[...], (1, d // LANES))
        acc = jnp.where(diag, acc, 0.0)
        o_ref[...] = jnp.sum(acc.reshape(n_heads, t, d), axis=0).astype(o_ref.dtype)


def _moba_sample(q, k_new, v_new, cache_k, cache_v, page_table, rel_bias):
    nseq, t, d = q.shape
    n_heads = d // HEAD_DIM
    n_pages = page_table.shape[1]
    past = n_pages * PAGE_SIZE
    blk = MOBA_BLOCK
    n_blk = past // blk
    pp = PAGES_PER_STEP
    cols = n_heads * t
    assert past % blk == 0 and blk % PAGE_SIZE == 0 and pp % (blk // PAGE_SIZE) == 0
    assert n_pages % pp == 0 and cols % SUBLANES == 0 and t % SUBLANES == 0 and t <= PAGE_SIZE
    assert n_blk <= LANES
    n_groups = n_pages // pp

    def kmap(r):
        return lambda b, s, pt, rb: (pt[b * n_pages + jnp.minimum(s, n_groups - 1) * pp + r], 0, 0)

    def vmap_(r):
        return lambda b, s, pt, rb: (pt[b * n_pages + jnp.maximum(s - n_groups, 0) * pp + r], 0, 0)

    seq = pl.BlockSpec((None, t, d), lambda b, s, pt, rb: (b, 0, 0))
    in_specs = ([pl.BlockSpec((None, d, PAGE_SIZE), kmap(r)) for r in range(pp)]
                + [pl.BlockSpec((None, d, PAGE_SIZE), vmap_(r)) for r in range(pp)]
                + [seq, seq, seq])
    return pl.pallas_call(
        functools.partial(_moba_sample_kernel, n_groups=n_groups, n_blk=n_blk, n_heads=n_heads, t=t),
        grid_spec=pltpu.PrefetchScalarGridSpec(
            num_scalar_prefetch=2,
            grid=(nseq, 2 * n_groups),
            in_specs=in_specs,
            out_specs=seq,
            scratch_shapes=[
                pltpu.VMEM((cols, d), BF16),
                pltpu.VMEM((n_blk, cols, blk), F32),
                pltpu.VMEM((n_blk, cols, blk), BF16),
                pltpu.VMEM((cols, PAGE_SIZE), BF16),
                pltpu.VMEM((cols, LANES), F32),
                pltpu.VMEM((cols, d), F32),
                pltpu.VMEM((cols, blk), F32),
                pltpu.VMEM((cols, blk), F32),
                pltpu.VMEM((cols, PAGE_SIZE), F32),
            ],
        ),
        out_shape=jax.ShapeDtypeStruct((nseq, t, d), BF16),
        compiler_params=_cparams(("arbitrary", "arbitrary")),
        name="moba_sample",
    )(page_table.reshape(-1), rel_bias.reshape(-1), *([cache_k] * pp), *([cache_v] * pp),
      q, k_new, v_new)


def _tile_rows(s):
    for t in (512, 256, 128, 64, 32, 16, 8):
        if s % t == 0:
            return t
    raise ValueError(f"sequence length {s} is not a multiple of {SUBLANES}")


def _trunk(x, mod, hist, past, page_table, w, *, nb, t, nb_conv):
    nseq, s, d = x.shape
    n_heads = d // HEAD_DIM
    depth = w["norm_g"].shape[0]
    hists, new_k, new_v = [], [], []
    attn = None
    for layer in range(depth):
        jm = layer // 2
        ffn = functools.partial(_ffn, mod=mod, norm_g=w["norm_g"], wg=w["wg"], wu=w["wu"],
                                wd=w["wd"], layer=layer, nb=nb, t=t)
        x = ffn(x, sub=0)
        if layer % 2 == 0:
            x, st = _conv_mixer(x, mod, w["norm_g"], hist[jm], w["pw1"], w["b_pw1"], w["dw"],
                                w["b_dw"], w["ln_g"], w["ln_b"], w["pw2"], w["b_pw2"],
                                layer=layer, jm=jm, nb=nb_conv, t=t)
            hists.append(st)
            attn = None
        else:
            q, k, v = _qkv(x, mod, w["norm_g"], w["qkv"], layer=layer, jm=jm, nb=nb, t=t,
                           q_dtype=BF16 if past is None else F32)
            if past is None:
                attn = _moba_prompt(q, k, v, w["rel_bias"])
            else:
                ck, cv, n_phys = past
                attn = _moba_sample(q, k, v, ck, cv, page_table + jm * n_phys, w["rel_bias"])
            new_k.append(k.reshape(nseq, s, n_heads, HEAD_DIM))
            new_v.append(v.reshape(nseq, s, n_heads, HEAD_DIM))
        last = layer == depth - 1
        x = ffn(x, sub=2, attn=attn, wo=w["wo"] if attn is not None else None, wo_idx=jm,
                final_g=w["final_g"] if last else None)
    return x, jnp.stack(hists), jnp.stack(new_k), jnp.stack(new_v)


def kernel(x_prompt, x_sample, state_conv, cache_k, cache_v, page_table, c_prompt, c_sample, norm_g, ada_w, ada_b, ffn_w_gate, ffn_w_up, ffn_w_down, conv_w_pw1, conv_b_pw1, conv_w_dw, conv_b_dw, conv_ln_g, conv_ln_b, conv_w_pw2, conv_b_pw2, attn_w_qkv, attn_w_o, rel_bias, final_norm_g):
    b, s, d = x_prompt.shape
    nd, ds_, _ = x_sample.shape
    depth = norm_g.shape[0]
    w = dict(
        norm_g=norm_g.reshape(depth, 3, 1, d),
        wg=ffn_w_gate.astype(BF16), wu=ffn_w_up.astype(BF16), wd=ffn_w_down.astype(BF16),
        pw1=conv_w_pw1.astype(BF16), b_pw1=conv_b_pw1, dw=conv_w_dw, b_dw=conv_b_dw,
        ln_g=conv_ln_g, ln_b=conv_ln_b, pw2=conv_w_pw2.astype(BF16), b_pw2=conv_b_pw2,
        qkv=attn_w_qkv.astype(BF16), wo=attn_w_o.astype(BF16), rel_bias=rel_bias,
        final_g=final_norm_g,
    )
    mod_p, mod_s = _ada_mod(c_prompt, c_sample, ada_w, ada_b)

    n_layers, n_phys = cache_k.shape[:2]
    pages = lambda c: jnp.transpose(c, (0, 1, 3, 4, 2)).reshape(n_layers * n_phys, d, PAGE_SIZE)

    hist0 = jnp.zeros((conv_w_dw.shape[0], b, CONV_WIDTH - 1, conv_w_dw.shape[2]), F32)
    y_p, st_p, k_p, v_p = _trunk(x_prompt, mod_p, hist0, None, None, w,
                                 nb=1, t=_tile_rows(s), nb_conv=1)
    y_s, st_s, k_s, v_s = _trunk(x_sample, mod_s, state_conv,
                                 (pages(cache_k), pages(cache_v), n_phys), page_table, w,
                                 nb=nd, t=ds_, nb_conv=SUBLANES)
    return (y_p, y_s, st_p, st_s, k_p, v_p, k_s, v_s)
```

```python
import functools
import math

import numpy as np
import jax
import jax.numpy as jnp
from jax import lax
from jax.experimental import pallas as pl
from jax.experimental.pallas import tpu as pltpu

F32 = jnp.float32
BF16 = jnp.bfloat16

HEAD_DIM = 64
CONV_WIDTH = 31
MOBA_BLOCK = 256
MOBA_TOP_K = 3
PAGE_SIZE = 128
NUM_BUCKETS = 32
MAX_DISTANCE = 128
NORM_EPS = 1e-6

LANES = 128
SUBLANES = 8
HIST_PAD = 32
VMEM_LIMIT = 56 * 1024 * 1024
PAGES_PER_STEP = 16


def _cparams(sem):
    return pltpu.CompilerParams(dimension_semantics=sem, vmem_limit_bytes=VMEM_LIMIT)


def _resident(block_shape, index_map):
    return pl.BlockSpec(block_shape, index_map, pipeline_mode=pl.Buffered(1))


def _mod_norm(x, g, shift, scale):
    ms = jnp.mean(x * x, axis=-1, keepdims=True)
    n = x * lax.rsqrt(ms + NORM_EPS)
    return n * g * (1.0 + scale) + shift


def _dot(a, b):
    return jnp.dot(a, b, preferred_element_type=F32)


def _dot_nt(a, b):
    return lax.dot_general(a, b, (((1,), (1,)), ((), ())), preferred_element_type=F32)


def _dot_tn(a, b):
    return lax.dot_general(a, b, (((0,), (0,)), ((), ())), preferred_element_type=F32)


def _mod_kernel(cp_ref, cs_ref, w_ref, b_ref, op_ref, os_ref):
    w = w_ref[...].astype(BF16)
    op_ref[...] = _dot(jax.nn.silu(cp_ref[...]).astype(BF16), w) + b_ref[...]
    os_ref[...] = _dot(jax.nn.silu(cs_ref[...]).astype(BF16), w) + b_ref[...]


def _ada_mod(c_prompt, c_sample, ada_w, ada_b):
    d = c_prompt.shape[1]
    depth = ada_w.shape[0]
    n_mod = ada_w.shape[2] // d
    outs = pl.pallas_call(
        _mod_kernel,
        grid=(depth, n_mod),
        in_specs=[
            pl.BlockSpec(c_prompt.shape, lambda l, j: (0, 0)),
            pl.BlockSpec(c_sample.shape, lambda l, j: (0, 0)),
            pl.BlockSpec((None, d, d), lambda l, j: (l, 0, j)),
            pl.BlockSpec((None, None, 1, d), lambda l, j: (l, j, 0, 0)),
        ],
        out_specs=[pl.BlockSpec((None, None) + c.shape, lambda l, j: (l, j, 0, 0))
                   for c in (c_prompt, c_sample)],
        out_shape=[jax.ShapeDtypeStruct((depth, n_mod) + c.shape, F32)
                   for c in (c_prompt, c_sample)],
        compiler_params=_cparams(("arbitrary", "arbitrary")),
        name="ada_mod",
    )(c_prompt, c_sample, ada_w, ada_b.reshape(depth, n_mod, 1, d))
    return [o.reshape(depth, n_mod, o.shape[2], 1, d) for o in outs]


def _ffn_kernel(*refs, nb, t, ff_chunk, pre_attn, final_norm):
    it = iter(refs)
    x_ref, mod_ref, g_ref = next(it), next(it), next(it)
    if pre_attn:
        a_ref, wo_ref, gate1_ref = next(it), next(it), next(it)
    wg_ref, wu_ref, wd_ref = next(it), next(it), next(it)
    if final_norm:
        fg_ref = next(it)
    o_ref, act_ref = next(it), next(it)

    x = x_ref[...]
    d = x.shape[-1]
    d_ff = wg_ref.shape[-1]
    if pre_attn:
        y = _dot(a_ref[...].reshape(nb * t, d), wo_ref[...])
        x = x + gate1_ref[0] * y.reshape(nb, t, d)
    h = _mod_norm(x, g_ref[...], mod_ref[0], mod_ref[1])
    hb = h.reshape(nb * t, d).astype(BF16)
    for j in range(d_ff // ff_chunk):
        sl = slice(j * ff_chunk, (j + 1) * ff_chunk)
        gj = _dot(hb, wg_ref[:, sl])
        uj = _dot(hb, wu_ref[:, sl])
        act_ref[:, sl] = (jax.nn.silu(gj) * uj).astype(BF16)
    y = _dot(act_ref[...], wd_ref[...])
    out = x + 0.5 * mod_ref[2] * y.reshape(nb, t, d)
    if final_norm:
        ms = jnp.mean(out * out, axis=-1, keepdims=True)
        out = out * lax.rsqrt(ms + NORM_EPS) * fg_ref[...]
    o_ref[...] = out


def _ffn(x, mod, norm_g, wg, wu, wd, *, layer, sub, nb, t, attn=None, wo=None, wo_idx=0,
         final_g=None):
    nseq, s, d = x.shape
    d_ff = wg.shape[-1]
    half = sub // 2
    pre_attn = attn is not None
    final_norm = final_g is not None
    grid = (nseq // nb, s // t)
    xspec = pl.BlockSpec((nb, t, d), lambda b, i: (b, i, 0))
    in_specs = [
        xspec,
        pl.BlockSpec((None, 3, nb, 1, d), lambda b, i: (layer, sub, b, 0, 0)),
        pl.BlockSpec((None, None, 1, d), lambda b, i: (layer, sub, 0, 0)),
    ]
    args = [x, mod, norm_g]
    if pre_attn:
        in_specs += [
            xspec,
            _resident((None, d, d), lambda b, i: (wo_idx, 0, 0)),
            pl.BlockSpec((None, 1, nb, 1, d), lambda b, i: (layer, 3 * (sub - 1) + 2, b, 0, 0)),
        ]
        args += [attn, wo, mod]
    in_specs += [
        _resident((None, None, d, d_ff), lambda b, i: (layer, half, 0, 0)),
        _resident((None, None, d, d_ff), lambda b, i: (layer, half, 0, 0)),
        _resident((None, None, d_ff, d), lambda b, i: (layer, half, 0, 0)),
    ]
    args += [wg, wu, wd]
    if final_norm:
        in_specs.append(pl.BlockSpec((1, d), lambda b, i: (0, 0)))
        args.append(final_g.reshape(1, d))
    ff_chunk = 256 if d_ff % 256 == 0 else LANES
    return pl.pallas_call(
        functools.partial(_ffn_kernel, nb=nb, t=t, ff_chunk=ff_chunk, pre_attn=pre_attn,
                          final_norm=final_norm),
        grid=grid,
        in_specs=in_specs,
        out_specs=xspec,
        out_shape=jax.ShapeDtypeStruct(x.shape, F32),
        scratch_shapes=[pltpu.VMEM((nb * t, d_ff), BF16)],
        compiler_params=_cparams(("arbitrary", "arbitrary")),
        name="ffn",
    )(*args)


def _conv_kernel(x_ref, mod_ref, g_ref, hist_ref, w1_ref, b1_ref, wdw_ref, bdw_ref, lng_ref,
                 lnb_ref, w2_ref, b2_ref, o_ref, st_ref, full_ref, y_ref, shift_ref, *, nb, t,
                 n_tiles, row_blk, lane_blk):
    c = w2_ref.shape[0]
    d = x_ref.shape[-1]
    first = HIST_PAD - (CONV_WIDTH - 1)
    shift_rows = shift_ref.shape[2]

    @pl.when(pl.program_id(1) == 0)
    def _():
        full_ref[:, 0:HIST_PAD, :] = hist_ref[...]

    x = x_ref[...]
    h = _mod_norm(x, g_ref[...], mod_ref[0], mod_ref[1])
    a = _dot(h.reshape(nb * t, d).astype(BF16), w1_ref[...]) + b1_ref[...]
    u = a[:, :c] * jax.nn.sigmoid(a[:, c:])
    full_ref[:, HIST_PAD:HIST_PAD + t, :] = u.reshape(nb, t, c)

    for c0 in range(0, c, lane_blk):
        lanes = slice(c0, c0 + lane_blk)
        for s in range(1, SUBLANES):
            shift_ref[s - 1] = full_ref[:, s:s + shift_rows, lanes]
        for r0 in range(0, t, row_blk):
            acc = jnp.zeros((nb * row_blk // SUBLANES, SUBLANES, lane_blk), F32)
            for w in range(CONV_WIDTH):
                base, s = divmod(first + w, SUBLANES)
                rows = slice(base * SUBLANES + r0, base * SUBLANES + r0 + row_blk)
                src = full_ref[:, rows, lanes] if s == 0 else shift_ref[s - 1, :, rows, :]
                acc = acc + src.reshape(acc.shape) * wdw_ref[w, :, lanes][None]
            y_ref[:, r0:r0 + row_blk, lanes] = (acc.reshape(nb, row_blk, lane_blk)
                                                + bdw_ref[:, lanes])

    y = y_ref[...].reshape(nb * t, c)
    mu = jnp.mean(y, axis=-1, keepdims=True)
    yc = y - mu
    var = jnp.mean(yc * yc, axis=-1, keepdims=True)
    yn = yc * lax.rsqrt(var + NORM_EPS) * lng_ref[...] + lnb_ref[...]
    out = _dot(jax.nn.silu(yn).astype(BF16), w2_ref[...]) + b2_ref[...]
    o_ref[...] = x + mod_ref[2] * out.reshape(nb, t, d)

    st_ref[...] = full_ref[:, t + first:t + HIST_PAD, :]
    if n_tiles > 1:
        full_ref[:, 0:HIST_PAD, :] = full_ref[:, t:t + HIST_PAD, :]


def _conv_mixer(x, mod, norm_g, hist, w1, b1, wdw, bdw, lng, lnb, w2, b2, *, layer, jm, nb, t):
    nseq, s, d = x.shape
    c = w2.shape[1]
    n_tiles = s // t
    acc_vregs = 16
    row_blk = min(t, 64)
    lane_blk = max(LANES, min(c, acc_vregs * SUBLANES * LANES // (nb * row_blk) // LANES * LANES))
    xspec = pl.BlockSpec((nb, t, d), lambda b, i: (b, i, 0))
    const = lambda b, i: (jm, 0, 0)
    vec = lambda a: a.reshape(a.shape[0], 1, a.shape[1])
    hist32 = jnp.pad(hist, ((0, 0), (HIST_PAD - (CONV_WIDTH - 1), 0), (0, 0)))
    return pl.pallas_call(
        functools.partial(_conv_kernel, nb=nb, t=t, n_tiles=n_tiles, row_blk=row_blk,
                          lane_blk=lane_blk),
        grid=(nseq // nb, n_tiles),
        in_specs=[
            xspec,
            pl.BlockSpec((None, 3, nb, 1, d), lambda b, i: (layer, 1, b, 0, 0)),
            pl.BlockSpec((None, None, 1, d), lambda b, i: (layer, 1, 0, 0)),
            pl.BlockSpec((nb, HIST_PAD, c), lambda b, i: (b, 0, 0)),
            _resident((None, d, 2 * c), const),
            pl.BlockSpec((None, 1, 2 * c), const),
            pl.BlockSpec((None, CONV_WIDTH, SUBLANES, c), lambda b, i: (jm, 0, 0, 0)),
            pl.BlockSpec((None, 1, c), const),
            pl.BlockSpec((None, 1, c), const),
            pl.BlockSpec((None, 1, c), const),
            _resident((None, c, d), const),
            pl.BlockSpec((None, 1, d), const),
        ],
        out_specs=[xspec, pl.BlockSpec((nb, CONV_WIDTH - 1, c), lambda b, i: (b, 0, 0))],
        out_shape=[jax.ShapeDtypeStruct(x.shape, F32),
                   jax.ShapeDtypeStruct((nseq, CONV_WIDTH - 1, c), F32)],
        scratch_shapes=[pltpu.VMEM((nb, HIST_PAD + t, c), F32), pltpu.VMEM((nb, t, c), F32),
                        pltpu.VMEM((SUBLANES - 1, nb, HIST_PAD - SUBLANES + t, lane_blk), F32)],
        compiler_params=_cparams(("arbitrary", "arbitrary")),
        name="conv_mixer",
    )(x, mod, norm_g, hist32, w1, vec(b1),
      jnp.broadcast_to(wdw[:, :, None, :], wdw.shape[:2] + (SUBLANES, c)),
      vec(bdw), vec(lng), vec(lnb), w2, vec(b2))


def _qkv_kernel(x_ref, mod_ref, g_ref, w_ref, q_ref, k_ref, v_ref, *, nb, t):
    x = x_ref[...]
    d = x.shape[-1]
    h = _mod_norm(x, g_ref[...], mod_ref[0], mod_ref[1])
    hb = h.reshape(nb * t, d).astype(BF16)
    q = _dot(hb, w_ref[:, 0:d]) * (1.0 / math.sqrt(HEAD_DIM))
    q_ref[...] = q.reshape(nb, t, d).astype(q_ref.dtype)
    k_ref[...] = _dot(hb, w_ref[:, d:2 * d]).reshape(nb, t, d)
    v_ref[...] = _dot(hb, w_ref[:, 2 * d:3 * d]).reshape(nb, t, d)


def _qkv(x, mod, norm_g, w, *, layer, jm, nb, t, q_dtype):
    nseq, s, d = x.shape
    xspec = pl.BlockSpec((nb, t, d), lambda b, i: (b, i, 0))
    return pl.pallas_call(
        functools.partial(_qkv_kernel, nb=nb, t=t),
        grid=(nseq // nb, s // t),
        in_specs=[
            xspec,
            pl.BlockSpec((None, 3, nb, 1, d), lambda b, i: (layer, 1, b, 0, 0)),
            pl.BlockSpec((None, None, 1, d), lambda b, i: (layer, 1, 0, 0)),
            _resident((None, d, 3 * d), lambda b, i: (jm, 0, 0)),
        ],
        out_specs=[xspec, xspec, xspec],
        out_shape=[jax.ShapeDtypeStruct(x.shape, q_dtype), jax.ShapeDtypeStruct(x.shape, F32),
                   jax.ShapeDtypeStruct(x.shape, F32)],
        compiler_params=_cparams(("arbitrary", "arbitrary")),
        name="qkv",
    )(x, mod, norm_g, w)


def _bucket_of_distance(n_dist):
    dist = np.arange(n_dist)
    max_exact = NUM_BUCKETS // 2
    nf = np.maximum(dist, max_exact).astype(np.float32)
    large = max_exact + (np.log(nf / max_exact) / math.log(MAX_DISTANCE / max_exact)
                         * (NUM_BUCKETS - max_exact)).astype(np.int32)
    large = np.minimum(large, NUM_BUCKETS - 1)
    return np.where(dist < max_exact, dist, large).astype(np.int32)


_BUCKETS = _bucket_of_distance(4 * MAX_DISTANCE)
assert _BUCKETS[-1] == NUM_BUCKETS - 1 and np.all(np.diff(_BUCKETS) >= 0)
_BUCKET_START = [int(np.argmax(_BUCKETS >= b)) for b in range(NUM_BUCKETS)]
assert _BUCKET_START[NUM_BUCKETS - 1] <= MOBA_BLOCK


def _bias_of_distance(dist, table):
    bias = jnp.full(dist.shape, table[0], F32)
    for b in range(1, NUM_BUCKETS):
        bias = jnp.where(dist >= _BUCKET_START[b], table[b], bias)
    return bias


def _rank_select(gate, n_valid, axis):
    idx = lax.broadcasted_iota(jnp.int32, gate.shape, axis)
    cnt = jnp.zeros(gate.shape, F32)
    for j in range(n_valid):
        gj = lax.slice_in_dim(gate, j, j + 1, axis=axis)
        tie = jnp.where(idx > j, 1.0, 0.0)
        cnt = cnt + jnp.where(gj > gate, 1.0, jnp.where(gj == gate, tie, 0.0))
    return cnt < float(MOBA_TOP_K)


def _moba_prompt_kernel(rb_ref, q_ref, k_ref, v_ref, o_ref, bias_ref, *, n_blk, n_heads):
    pair = pl.program_id(0)
    blk = MOBA_BLOCK

    @pl.when(pl.program_id(1) == 0)
    def _():
        dist = (lax.broadcasted_iota(jnp.int32, (blk, blk), 0)
                - lax.broadcasted_iota(jnp.int32, (blk, blk), 1))
        for hh in range(2):
            table = [rb_ref[b * n_heads + 2 * pair + hh] for b in range(NUM_BUCKETS)]
            far = table[NUM_BUCKETS - 1]
            bias_ref[hh, 0] = jnp.where(dist >= 0, _bias_of_distance(dist, table) - far, -jnp.inf)
            bias_ref[hh, 1] = _bias_of_distance(dist + blk, table) - far

    kf = k_ref[...]
    kb = kf.astype(BF16)
    vb = v_ref[...].astype(BF16)
    kmean = jnp.mean(kf.reshape(n_blk, blk, LANES), axis=1)
    kmean = jnp.concatenate([kmean, jnp.zeros((LANES - n_blk, LANES), F32)], axis=0).astype(BF16)
    lane = lax.broadcasted_iota(jnp.int32, (1, LANES), 1)
    for i in range(n_blk):
        qi = q_ref[i * blk:(i + 1) * blk, :]
        outs = []
        for hh in range(2):
            hmask = (lane // HEAD_DIM) == hh
            qh = jnp.where(hmask, qi, jnp.zeros_like(qi))
            if i > MOBA_TOP_K:
                sel = _rank_select(_dot_nt(qh, kmean), i, axis=1)
            s_list = []
            for j in range(i + 1):
                s = _dot_nt(qh, kb[j * blk:(j + 1) * blk, :])
                if j == i:
                    s = s + bias_ref[hh, 0]
                elif j == i - 1:
                    s = s + bias_ref[hh, 1]
                if i > MOBA_TOP_K and j < i:
                    s = jnp.where(sel[:, j:j + 1], s, -jnp.inf)
                s_list.append(s)
            m = functools.reduce(jnp.maximum, [jnp.max(s, axis=-1, keepdims=True) for s in s_list])
            l = jnp.zeros((blk, 1), F32)
            acc = jnp.zeros((blk, LANES), F32)
            for j, s in enumerate(s_list):
                p = jnp.exp(s - m)
                l = l + jnp.sum(p, axis=-1, keepdims=True)
                acc = acc + _dot(p.astype(BF16), vb[j * blk:(j + 1) * blk, :])
            outs.append(acc * (1.0 / l))
        o_ref[i * blk:(i + 1) * blk, :] = jnp.where((lane // HEAD_DIM) == 0, outs[0],
                                                    outs[1]).astype(o_ref.dtype)


def _moba_prompt(q, k, v, rel_bias):
    b, s, d = q.shape
    n_heads = d // HEAD_DIM
    blk = MOBA_BLOCK
    n_blk = s // blk
    assert s % blk == 0 and n_heads % 2 == 0 and 2 * HEAD_DIM == LANES
    hspec = pl.BlockSpec((None, s, LANES), lambda p, bi, rb_ref: (bi, 0, p))
    return pl.pallas_call(
        functools.partial(_moba_prompt_kernel, n_blk=n_blk, n_heads=n_heads),
        grid_spec=pltpu.PrefetchScalarGridSpec(
            num_scalar_prefetch=1,
            grid=(n_heads // 2, b),
            in_specs=[hspec, hspec, hspec],
            out_specs=hspec,
            scratch_shapes=[pltpu.VMEM((2, 2, blk, blk), F32)],
        ),
        out_shape=jax.ShapeDtypeStruct((b, s, d), BF16),
        compiler_params=_cparams(("arbitrary", "arbitrary")),
        name="moba_prompt",
    )(rel_bias.reshape(-1), q, k, v)


def _moba_sample_kernel(pt_ref, rb_ref, *refs, n_groups, n_blk, n_heads, t):
    pp = PAGES_PER_STEP
    k_refs = refs[0:pp]
    v_refs = refs[pp:2 * pp]
    (q_ref, kn_ref, vn_ref, o_ref, a_ref, s_ref, p_ref, pown_ref, l_ref, acc_ref,
     blast_ref, bown_ref, gate_ref, bmax_ref) = refs[2 * pp:]
    del pt_ref
    seq = pl.program_id(0)
    step = pl.program_id(1)
    blk = MOBA_BLOCK
    ppb = blk // PAGE_SIZE
    bps = pp // ppb
    d = q_ref.shape[-1]
    cols = n_heads * t
    row_head = lax.broadcasted_iota(jnp.int32, (cols, d), 0) // t
    lane_head = lax.broadcasted_iota(jnp.int32, (cols, d), 1) // HEAD_DIM
    diag = row_head == lane_head
    lane = lax.broadcasted_iota(jnp.int32, (cols, LANES), 1)

    def new_rows(ref):
        return jnp.concatenate([ref[...], jnp.zeros((PAGE_SIZE - t, ref.shape[-1]), F32)],
                               axis=0).astype(BF16)

    def block_of(page_refs, r):
        return jnp.concatenate([page_refs[ppb * r + i][...] for i in range(ppb)],
                               axis=1).astype(BF16)

    @pl.when(jnp.logical_and(seq == 0, step == 0))
    def _():
        qi = lax.broadcasted_iota(jnp.int32, (t, blk), 0)
        ki = lax.broadcasted_iota(jnp.int32, (t, blk), 1)
        qo = lax.broadcasted_iota(jnp.int32, (t, PAGE_SIZE), 0)
        ko = lax.broadcasted_iota(jnp.int32, (t, PAGE_SIZE), 1)
        for h in range(n_heads):
            table = [rb_ref[b * n_heads + h] for b in range(NUM_BUCKETS)]
            far = table[NUM_BUCKETS - 1]
            rows = slice(h * t, (h + 1) * t)
            blast_ref[rows, :] = _bias_of_distance(blk + qi - ki, table) - far
            bown_ref[rows, :] = jnp.where(ko <= qo, _bias_of_distance(qo - ko, table) - far,
                                          -jnp.inf)

    @pl.when(step == 0)
    def _():
        qt = jnp.tile(q_ref[...], (n_heads, 1))
        a_ref[...] = jnp.where(diag, qt, 0.0).astype(BF16)
        acc_ref[...] = jnp.zeros_like(acc_ref)
        gate_ref[...] = jnp.full_like(gate_ref, -jnp.inf)
        bmax_ref[...] = jnp.full_like(bmax_ref, -jnp.inf)

    @pl.when(step < n_groups)
    def _():
        a = a_ref[...]
        gate = gate_ref[...]
        bmax = bmax_ref[...]
        for r in range(bps):
            n = step * bps + r
            s = _dot(a, block_of(k_refs, r))
            s_ref[n] = s
            gate = jnp.where(lane == n, jnp.sum(s, axis=1, keepdims=True), gate)
            bmax = jnp.where(lane == n, jnp.max(s, axis=1, keepdims=True), bmax)
        gate_ref[...] = gate
        bmax_ref[...] = bmax

    @pl.when(step == n_groups)
    def _():
        lane_f = lane.astype(F32)
        g = gate_ref[...]
        picks = []
        for _ in range(MOBA_TOP_K):
            top = jnp.max(g, axis=1, keepdims=True)
            pick = jnp.min(jnp.where(g == top, lane_f, float(LANES)), axis=1, keepdims=True)
            picks.append(pick)
            g = jnp.where(lane_f == pick, -jnp.inf, g)

        def picked(n):
            hit = jnp.where(picks[0] == n, 1.0, 0.0)
            for pick in picks[1:]:
                hit = jnp.where(pick == n, 1.0, hit)
            return hit > 0.0

        last = n_blk - 1
        s_own = _dot_nt(a_ref[...], new_rows(kn_ref)) + bown_ref[...]
        bmax = jnp.where(lane == last,
                         jnp.max(s_ref[last] + blast_ref[...], axis=1, keepdims=True), bmax_ref[...])
        m = jnp.maximum(jnp.max(s_own, axis=1, keepdims=True),
                        jnp.max(jnp.where(picked(lane_f), bmax, -jnp.inf), axis=1, keepdims=True))
        p_own = jnp.exp(s_own - m)
        lsum = jnp.zeros((cols, blk), F32)
        for n in range(n_blk):
            sn = s_ref[n] + blast_ref[...] if n == last else s_ref[n]
            p = jnp.where(picked(float(n)), jnp.exp(sn - m), 0.0)
            lsum = lsum + p
            p_ref[n] = p.astype(BF16)
        l = jnp.sum(lsum, axis=1, keepdims=True) + jnp.sum(p_own, axis=1, keepdims=True)
        pown_ref[...] = p_own.astype(BF16)
        l_ref[...] = jnp.broadcast_to(l, (cols, LANES))

    @pl.when(step >= n_groups)
    def _():
        acc = acc_ref[...]
        for r in range(bps):
            acc = acc + _dot_nt(p_ref[(step - n_groups) * bps + r], block_of(v_refs, r))
        acc_ref[...] = acc

    @pl.when(step == 2 * n_groups - 1)
    def _():
        acc = acc_ref[...] + _dot(pown_ref[...], new_rows(vn_ref))
        acc = acc * jnp.tile(1.0 / l_ref[...], (1, d // LANES))
        acc = jnp.where(diag, acc, 0.0)
        o_ref[...] = jnp.sum(acc.reshape(n_heads, t, d), axis=0).astype(o_ref.dtype)


def _moba_sample(q, k_new, v_new, cache_k, cache_v, page_table, rel_bias):
    nseq, t, d = q.shape
    n_heads = d // HEAD_DIM
    n_pages = page_table.shape[1]
    past = n_pages * PAGE_SIZE
    blk = MOBA_BLOCK
    n_blk = past // blk
    pp = PAGES_PER_STEP
    cols = n_heads * t
    assert past % blk == 0 and blk % PAGE_SIZE == 0 and pp % (blk // PAGE_SIZE) == 0
    assert n_pages % pp == 0 and cols % SUBLANES == 0 and t % SUBLANES == 0 and t <= PAGE_SIZE
    assert MOBA_TOP_K <= n_blk <= LANES
    n_groups = n_pages // pp

    def kmap(r):
        return lambda b, s, pt, rb: (pt[b * n_pages + jnp.minimum(s, n_groups - 1) * pp + r], 0, 0)

    def vmap_(r):
        return lambda b, s, pt, rb: (pt[b * n_pages + jnp.maximum(s - n_groups, 0) * pp + r], 0, 0)

    seq = pl.BlockSpec((None, t, d), lambda b, s, pt, rb: (b, 0, 0))
    in_specs = ([pl.BlockSpec((None, d, PAGE_SIZE), kmap(r)) for r in range(pp)]
                + [pl.BlockSpec((None, d, PAGE_SIZE), vmap_(r)) for r in range(pp)]
                + [seq, seq, seq])
    return pl.pallas_call(
        functools.partial(_moba_sample_kernel, n_groups=n_groups, n_blk=n_blk, n_heads=n_heads, t=t),
        grid_spec=pltpu.PrefetchScalarGridSpec(
            num_scalar_prefetch=2,
            grid=(nseq, 2 * n_groups),
            in_specs=in_specs,
            out_specs=seq,
            scratch_shapes=[
                pltpu.VMEM((cols, d), BF16),
                pltpu.VMEM((n_blk, cols, blk), F32),
                pltpu.VMEM((n_blk, cols, blk), BF16),
                pltpu.VMEM((cols, PAGE_SIZE), BF16),
                pltpu.VMEM((cols, LANES), F32),
                pltpu.VMEM((cols, d), F32),
                pltpu.VMEM((cols, blk), F32),
                pltpu.VMEM((cols, PAGE_SIZE), F32),
                pltpu.VMEM((cols, LANES), F32),
                pltpu.VMEM((cols, LANES), F32),
            ],
        ),
        out_shape=jax.ShapeDtypeStruct((nseq, t, d), BF16),
        compiler_params=_cparams(("arbitrary", "arbitrary")),
        name="moba_sample",
    )(page_table.reshape(-1), rel_bias.reshape(-1), *([cache_k] * pp), *([cache_v] * pp),
      q, k_new, v_new)


def _tile_rows(s):
    for t in (512, 256, 128, 64, 32, 16, 8):
        if s % t == 0:
            return t
    raise ValueError(f"sequence length {s} is not a multiple of {SUBLANES}")


def _trunk(x, mod, hist, past, page_table, w, *, nb, t, nb_conv):
    nseq, s, d = x.shape
    n_heads = d // HEAD_DIM
    depth = w["norm_g"].shape[0]
    hists, new_k, new_v = [], [], []
    attn = None
    for layer in range(depth):
        jm = layer // 2
        ffn = functools.partial(_ffn, mod=mod, norm_g=w["norm_g"], wg=w["wg"], wu=w["wu"],
                                wd=w["wd"], layer=layer, nb=nb, t=t)
        x = ffn(x, sub=0)
        if layer % 2 == 0:
            x, st = _conv_mixer(x, mod, w["norm_g"], hist[jm], w["pw1"], w["b_pw1"], w["dw"],
                                w["b_dw"], w["ln_g"], w["ln_b"], w["pw2"], w["b_pw2"],
                                layer=layer, jm=jm, nb=nb_conv, t=t)
            hists.append(st)
            attn = None
        else:
            q, k, v = _qkv(x, mod, w["norm_g"], w["qkv"], layer=layer, jm=jm, nb=nb, t=t,
                           q_dtype=BF16 if past is None else F32)
            if past is None:
                attn = _moba_prompt(q, k, v, w["rel_bias"])
            else:
                ck, cv, n_phys = past
                attn = _moba_sample(q, k, v, ck, cv, page_table + jm * n_phys, w["rel_bias"])
            new_k.append(k.reshape(nseq, s, n_heads, HEAD_DIM))
            new_v.append(v.reshape(nseq, s, n_heads, HEAD_DIM))
        last = layer == depth - 1
        x = ffn(x, sub=2, attn=attn, wo=w["wo"] if attn is not None else None, wo_idx=jm,
                final_g=w["final_g"] if last else None)
    return x, jnp.stack(hists), jnp.stack(new_k), jnp.stack(new_v)


def kernel(x_prompt, x_sample, state_conv, cache_k, cache_v, page_table, c_prompt, c_sample, norm_g, ada_w, ada_b, ffn_w_gate, ffn_w_up, ffn_w_down, conv_w_pw1, conv_b_pw1, conv_w_dw, conv_b_dw, conv_ln_g, conv_ln_b, conv_w_pw2, conv_b_pw2, attn_w_qkv, attn_w_o, rel_bias, final_norm_g):
    b, s, d = x_prompt.shape
    nd, ds_, _ = x_sample.shape
    depth = norm_g.shape[0]
    w = dict(
        norm_g=norm_g.reshape(depth, 3, 1, d),
        wg=ffn_w_gate.astype(BF16), wu=ffn_w_up.astype(BF16), wd=ffn_w_down.astype(BF16),
        pw1=conv_w_pw1.astype(BF16), b_pw1=conv_b_pw1, dw=conv_w_dw, b_dw=conv_b_dw,
        ln_g=conv_ln_g, ln_b=conv_ln_b, pw2=conv_w_pw2.astype(BF16), b_pw2=conv_b_pw2,
        qkv=attn_w_qkv.astype(BF16), wo=attn_w_o.astype(BF16), rel_bias=rel_bias,
        final_g=final_norm_g,
    )
    mod_p, mod_s = _ada_mod(c_prompt, c_sample, ada_w, ada_b)

    n_layers, n_phys = cache_k.shape[:2]
    pages = lambda c: jnp.transpose(c, (0, 1, 3, 4, 2)).reshape(n_layers * n_phys, d, PAGE_SIZE)

    hist0 = jnp.zeros((conv_w_dw.shape[0], b, CONV_WIDTH - 1, conv_w_dw.shape[2]), F32)
    y_p, st_p, k_p, v_p = _trunk(x_prompt, mod_p, hist0, None, None, w,
                                 nb=1, t=_tile_rows(s), nb_conv=1)
    y_s, st_s, k_s, v_s = _trunk(x_sample, mod_s, state_conv,
                                 (pages(cache_k), pages(cache_v), n_phys), page_table, w,
                                 nb=nd, t=ds_, nb_conv=SUBLANES)
    return (y_p, y_s, st_p, st_s, k_p, v_p, k_s, v_s)
```

```python
import functools
import math

import numpy as np
import jax
import jax.numpy as jnp
from jax import lax
from jax.experimental import pallas as pl
from jax.experimental.pallas import tpu as pltpu

F32 = jnp.float32
BF16 = jnp.bfloat16

HEAD_DIM = 64
CONV_WIDTH = 31
MOBA_BLOCK = 256
MOBA_TOP_K = 3
PAGE_SIZE = 128
NUM_BUCKETS = 32
MAX_DISTANCE = 128
NORM_EPS = 1e-6

LANES = 128
SUBLANES = 8
HIST_PAD = 32
VMEM_LIMIT = 56 * 1024 * 1024
PAGES_PER_STEP = 16
LOG2E = math.log2(math.e)
MASKED = -1e30


def _cparams(sem):
    return pltpu.CompilerParams(dimension_semantics=sem, vmem_limit_bytes=VMEM_LIMIT)


def _resident(block_shape, index_map):
    return pl.BlockSpec(block_shape, index_map, pipeline_mode=pl.Buffered(1))


def _mod_norm(x, g, shift, scale):
    ms = jnp.mean(x * x, axis=-1, keepdims=True)
    n = x * lax.rsqrt(ms + NORM_EPS)
    return n * g * (1.0 + scale) + shift


def _dot(a, b):
    return jnp.dot(a, b, preferred_element_type=F32)


def _dot_nt(a, b):
    return lax.dot_general(a, b, (((1,), (1,)), ((), ())), preferred_element_type=F32)


def _dot_tn(a, b):
    return lax.dot_general(a, b, (((0,), (0,)), ((), ())), preferred_element_type=F32)


def _mod_kernel(cp_ref, cs_ref, w_ref, b_ref, op_ref, os_ref):
    w = w_ref[...].astype(BF16)
    op_ref[...] = _dot(jax.nn.silu(cp_ref[...]).astype(BF16), w) + b_ref[...]
    os_ref[...] = _dot(jax.nn.silu(cs_ref[...]).astype(BF16), w) + b_ref[...]


def _ada_mod(c_prompt, c_sample, ada_w, ada_b):
    d = c_prompt.shape[1]
    depth = ada_w.shape[0]
    n_mod = ada_w.shape[2] // d
    outs = pl.pallas_call(
        _mod_kernel,
        grid=(depth, n_mod),
        in_specs=[
            pl.BlockSpec(c_prompt.shape, lambda l, j: (0, 0)),
            pl.BlockSpec(c_sample.shape, lambda l, j: (0, 0)),
            pl.BlockSpec((None, d, d), lambda l, j: (l, 0, j)),
            pl.BlockSpec((None, None, 1, d), lambda l, j: (l, j, 0, 0)),
        ],
        out_specs=[pl.BlockSpec((None, None) + c.shape, lambda l, j: (l, j, 0, 0))
                   for c in (c_prompt, c_sample)],
        out_shape=[jax.ShapeDtypeStruct((depth, n_mod) + c.shape, F32)
                   for c in (c_prompt, c_sample)],
        compiler_params=_cparams(("arbitrary", "arbitrary")),
        name="ada_mod",
    )(c_prompt, c_sample, ada_w, ada_b.reshape(depth, n_mod, 1, d))
    return [o.reshape(depth, n_mod, o.shape[2], 1, d) for o in outs]


def _ffn_kernel(*refs, nb, t, ff_chunk, pre_attn, final_norm):
    it = iter(refs)
    x_ref, mod_ref, g_ref = next(it), next(it), next(it)
    if pre_attn:
        a_ref, wo_ref, gate1_ref = next(it), next(it), next(it)
    wg_ref, wu_ref, wd_ref = next(it), next(it), next(it)
    if final_norm:
        fg_ref = next(it)
    o_ref, act_ref = next(it), next(it)

    x = x_ref[...]
    d = x.shape[-1]
    d_ff = wg_ref.shape[-1]
    if pre_attn:
        y = _dot(a_ref[...].reshape(nb * t, d), wo_ref[...])
        x = x + gate1_ref[0] * y.reshape(nb, t, d)
    h = _mod_norm(x, g_ref[...], mod_ref[0], mod_ref[1])
    hb = h.reshape(nb * t, d).astype(BF16)
    for j in range(d_ff // ff_chunk):
        sl = slice(j * ff_chunk, (j + 1) * ff_chunk)
        gj = _dot(hb, wg_ref[:, sl])
        uj = _dot(hb, wu_ref[:, sl])
        act_ref[:, sl] = (jax.nn.silu(gj) * uj).astype(BF16)
    y = _dot(act_ref[...], wd_ref[...])
    out = x + 0.5 * mod_ref[2] * y.reshape(nb, t, d)
    if final_norm:
        ms = jnp.mean(out * out, axis=-1, keepdims=True)
        out = out * lax.rsqrt(ms + NORM_EPS) * fg_ref[...]
    o_ref[...] = out


def _ffn(x, mod, norm_g, wg, wu, wd, *, layer, sub, nb, t, attn=None, wo=None, wo_idx=0,
         final_g=None):
    nseq, s, d = x.shape
    d_ff = wg.shape[-1]
    half = sub // 2
    pre_attn = attn is not None
    final_norm = final_g is not None
    grid = (nseq // nb, s // t)
    xspec = pl.BlockSpec((nb, t, d), lambda b, i: (b, i, 0))
    in_specs = [
        xspec,
        pl.BlockSpec((None, 3, nb, 1, d), lambda b, i: (layer, sub, b, 0, 0)),
        pl.BlockSpec((None, None, 1, d), lambda b, i: (layer, sub, 0, 0)),
    ]
    args = [x, mod, norm_g]
    if pre_attn:
        in_specs += [
            xspec,
            _resident((None, d, d), lambda b, i: (wo_idx, 0, 0)),
            pl.BlockSpec((None, 1, nb, 1, d), lambda b, i: (layer, 3 * (sub - 1) + 2, b, 0, 0)),
        ]
        args += [attn, wo, mod]
    in_specs += [
        _resident((None, None, d, d_ff), lambda b, i: (layer, half, 0, 0)),
        _resident((None, None, d, d_ff), lambda b, i: (layer, half, 0, 0)),
        _resident((None, None, d_ff, d), lambda b, i: (layer, half, 0, 0)),
    ]
    args += [wg, wu, wd]
    if final_norm:
        in_specs.append(pl.BlockSpec((1, d), lambda b, i: (0, 0)))
        args.append(final_g.reshape(1, d))
    ff_chunk = 256 if d_ff % 256 == 0 else LANES
    return pl.pallas_call(
        functools.partial(_ffn_kernel, nb=nb, t=t, ff_chunk=ff_chunk, pre_attn=pre_attn,
                          final_norm=final_norm),
        grid=grid,
        in_specs=in_specs,
        out_specs=xspec,
        out_shape=jax.ShapeDtypeStruct(x.shape, F32),
        scratch_shapes=[pltpu.VMEM((nb * t, d_ff), BF16)],
        compiler_params=_cparams(("arbitrary", "arbitrary")),
        name="ffn",
    )(*args)


def _conv_kernel(x_ref, mod_ref, g_ref, hist_ref, w1_ref, b1_ref, wdw_ref, bdw_ref, lng_ref,
                 lnb_ref, w2_ref, b2_ref, o_ref, st_ref, full_ref, y_ref, shift_ref, *, nb, t,
                 n_tiles, n_parts, row_blk, lane_blk):
    c = w2_ref.shape[0]
    d = x_ref.shape[-1]
    first = HIST_PAD - (CONV_WIDTH - 1)
    shift_rows = shift_ref.shape[2]

    @pl.when(pl.program_id(1) == 0)
    def _():
        full_ref[:, 0:HIST_PAD, :] = hist_ref[...]

    part = t // n_parts

    def glu(p):
        xp = x_ref[:, p * part:(p + 1) * part, :]
        h = _mod_norm(xp, g_ref[...], mod_ref[0], mod_ref[1])
        a = _dot(h.reshape(nb * part, d).astype(BF16), w1_ref[...]) + b1_ref[...]
        u = a[:, :c] * jax.nn.sigmoid(a[:, c:])
        full_ref[:, HIST_PAD + p * part:HIST_PAD + (p + 1) * part, :] = u.reshape(nb, part, c)

    def depthwise(p):
        row0 = p * part
        for c0 in range(0, c, lane_blk):
            lanes = slice(c0, c0 + lane_blk)
            for s in range(1, SUBLANES):
                shift_ref[s - 1] = full_ref[:, row0 + s:row0 + s + shift_rows, lanes]
            for r0 in range(0, part, row_blk):
                acc = jnp.zeros((nb * row_blk // SUBLANES, SUBLANES, lane_blk), F32)
                for w in range(CONV_WIDTH):
                    base, s = divmod(first + w, SUBLANES)
                    lo = base * SUBLANES + r0
                    src = (full_ref[:, row0 + lo:row0 + lo + row_blk, lanes] if s == 0
                           else shift_ref[s - 1, :, lo:lo + row_blk, :])
                    acc = acc + src.reshape(acc.shape) * wdw_ref[w, :, lanes][None]
                y_ref[:, row0 + r0:row0 + r0 + row_blk, lanes] = (
                    acc.reshape(nb, row_blk, lane_blk) + bdw_ref[:, lanes])

    def project(p):
        rows = slice(p * part, (p + 1) * part)
        y = y_ref[:, rows, :].reshape(nb * part, c)
        mu = jnp.mean(y, axis=-1, keepdims=True)
        yc = y - mu
        var = jnp.mean(yc * yc, axis=-1, keepdims=True)
        yn = yc * lax.rsqrt(var + NORM_EPS) * lng_ref[...] + lnb_ref[...]
        out = _dot(jax.nn.silu(yn).astype(BF16), w2_ref[...]) + b2_ref[...]
        o_ref[:, rows, :] = x_ref[:, rows, :] + mod_ref[2] * out.reshape(nb, part, d)

    glu(0)
    for p in range(n_parts):
        if p + 1 < n_parts:
            glu(p + 1)
        depthwise(p)
        project(p)

    st_ref[...] = full_ref[:, t + first:t + HIST_PAD, :]
    if n_tiles > 1:
        full_ref[:, 0:HIST_PAD, :] = full_ref[:, t:t + HIST_PAD, :]


def _conv_mixer(x, mod, norm_g, hist, w1, b1, wdw, bdw, lng, lnb, w2, b2, *, layer, jm, nb, t):
    nseq, s, d = x.shape
    c = w2.shape[1]
    n_tiles = s // t
    acc_vregs = 16
    row_blk = min(t, 64)
    n_parts = 2 if t % (2 * row_blk) == 0 else 1
    part = t // n_parts
    lane_blk = max(LANES, min(c, acc_vregs * SUBLANES * LANES // (nb * row_blk) // LANES * LANES))
    xspec = pl.BlockSpec((nb, t, d), lambda b, i: (b, i, 0))
    const = lambda b, i: (jm, 0, 0)
    vec = lambda a: a.reshape(a.shape[0], 1, a.shape[1])
    hist32 = jnp.pad(hist, ((0, 0), (HIST_PAD - (CONV_WIDTH - 1), 0), (0, 0)))
    return pl.pallas_call(
        functools.partial(_conv_kernel, nb=nb, t=t, n_tiles=n_tiles, n_parts=n_parts,
                          row_blk=row_blk, lane_blk=lane_blk),
        grid=(nseq // nb, n_tiles),
        in_specs=[
            xspec,
            pl.BlockSpec((None, 3, nb, 1, d), lambda b, i: (layer, 1, b, 0, 0)),
            pl.BlockSpec((None, None, 1, d), lambda b, i: (layer, 1, 0, 0)),
            pl.BlockSpec((nb, HIST_PAD, c), lambda b, i: (b, 0, 0)),
            _resident((None, d, 2 * c), const),
            pl.BlockSpec((None, 1, 2 * c), const),
            pl.BlockSpec((None, CONV_WIDTH, SUBLANES, c), lambda b, i: (jm, 0, 0, 0)),
            pl.BlockSpec((None, 1, c), const),
            pl.BlockSpec((None, 1, c), const),
            pl.BlockSpec((None, 1, c), const),
            _resident((None, c, d), const),
            pl.BlockSpec((None, 1, d), const),
        ],
        out_specs=[xspec, pl.BlockSpec((nb, CONV_WIDTH - 1, c), lambda b, i: (b, 0, 0))],
        out_shape=[jax.ShapeDtypeStruct(x.shape, F32),
                   jax.ShapeDtypeStruct((nseq, CONV_WIDTH - 1, c), F32)],
        scratch_shapes=[pltpu.VMEM((nb, HIST_PAD + t, c), F32), pltpu.VMEM((nb, t, c), F32),
                        pltpu.VMEM((SUBLANES - 1, nb, HIST_PAD - SUBLANES + part, lane_blk), F32)],
        compiler_params=_cparams(("arbitrary", "arbitrary")),
        name="conv_mixer",
    )(x, mod, norm_g, hist32, w1, vec(b1),
      jnp.broadcast_to(wdw[:, :, None, :], wdw.shape[:2] + (SUBLANES, c)),
      vec(bdw), vec(lng), vec(lnb), w2, vec(b2))


def _qkv_kernel(x_ref, mod_ref, g_ref, w_ref, q_ref, k_ref, v_ref, *, nb, t, q_scale):
    x = x_ref[...]
    d = x.shape[-1]
    h = _mod_norm(x, g_ref[...], mod_ref[0], mod_ref[1])
    hb = h.reshape(nb * t, d).astype(BF16)
    q = _dot(hb, w_ref[:, 0:d]) * q_scale
    q_ref[...] = q.reshape(nb, t, d).astype(q_ref.dtype)
    k_ref[...] = _dot(hb, w_ref[:, d:2 * d]).reshape(nb, t, d)
    v_ref[...] = _dot(hb, w_ref[:, 2 * d:3 * d]).reshape(nb, t, d)


def _qkv(x, mod, norm_g, w, *, layer, jm, nb, t, q_dtype, q_scale):
    nseq, s, d = x.shape
    xspec = pl.BlockSpec((nb, t, d), lambda b, i: (b, i, 0))
    return pl.pallas_call(
        functools.partial(_qkv_kernel, nb=nb, t=t, q_scale=q_scale),
        grid=(nseq // nb, s // t),
        in_specs=[
            xspec,
            pl.BlockSpec((None, 3, nb, 1, d), lambda b, i: (layer, 1, b, 0, 0)),
            pl.BlockSpec((None, None, 1, d), lambda b, i: (layer, 1, 0, 0)),
            _resident((None, d, 3 * d), lambda b, i: (jm, 0, 0)),
        ],
        out_specs=[xspec, xspec, xspec],
        out_shape=[jax.ShapeDtypeStruct(x.shape, q_dtype), jax.ShapeDtypeStruct(x.shape, F32),
                   jax.ShapeDtypeStruct(x.shape, F32)],
        compiler_params=_cparams(("arbitrary", "arbitrary")),
        name="qkv",
    )(x, mod, norm_g, w)


def _bucket_of_distance(n_dist):
    dist = np.arange(n_dist)
    max_exact = NUM_BUCKETS // 2
    nf = np.maximum(dist, max_exact).astype(np.float32)
    large = max_exact + (np.log(nf / max_exact) / math.log(MAX_DISTANCE / max_exact)
                         * (NUM_BUCKETS - max_exact)).astype(np.int32)
    large = np.minimum(large, NUM_BUCKETS - 1)
    return np.where(dist < max_exact, dist, large).astype(np.int32)


_BUCKETS = _bucket_of_distance(4 * MAX_DISTANCE)
assert _BUCKETS[-1] == NUM_BUCKETS - 1 and np.all(np.diff(_BUCKETS) >= 0)
_BUCKET_START = [int(np.argmax(_BUCKETS >= b)) for b in range(NUM_BUCKETS)]
assert _BUCKET_START[NUM_BUCKETS - 1] <= MOBA_BLOCK


def _bias_of_distance(dist, table):
    bias = jnp.full(dist.shape, table[0], F32)
    for b in range(1, NUM_BUCKETS):
        bias = jnp.where(dist >= _BUCKET_START[b], table[b], bias)
    return bias


def _rank_select(gate, n_valid, axis, first=0):
    idx = lax.broadcasted_iota(jnp.int32, gate.shape, axis)
    cnt = jnp.zeros(gate.shape, F32)
    for j in range(first, first + n_valid):
        gj = lax.slice_in_dim(gate, j, j + 1, axis=axis)
        tie = jnp.where(idx > j, 1.0, 0.0)
        cnt = cnt + jnp.where(gj > gate, 1.0, jnp.where(gj == gate, tie, 0.0))
    return cnt < float(MOBA_TOP_K)


def _moba_prompt_kernel(rb_ref, q_ref, k_ref, v_ref, o_ref, bias_ref, *, n_blk, n_heads):
    pair = pl.program_id(0)
    blk = MOBA_BLOCK

    @pl.when(pl.program_id(1) == 0)
    def _():
        dist = (lax.broadcasted_iota(jnp.int32, (blk, blk), 0)
                - lax.broadcasted_iota(jnp.int32, (blk, blk), 1))
        for hh in range(2):
            table = [rb_ref[b * n_heads + 2 * pair + hh] for b in range(NUM_BUCKETS)]
            far = table[NUM_BUCKETS - 1]
            own = (_bias_of_distance(dist, table) - far) * LOG2E
            bias_ref[hh, 0] = jnp.where(dist >= 0, own, -jnp.inf)
            bias_ref[hh, 1] = (_bias_of_distance(dist + blk, table) - far) * LOG2E

    kf = k_ref[...]
    vf = v_ref[...]
    s_len = kf.shape[0]
    lane = lax.broadcasted_iota(jnp.int32, (1, LANES), 1)
    key_lane = lax.broadcasted_iota(jnp.int32, (s_len, LANES), 1)
    key_blk = lax.broadcasted_iota(jnp.int32, (s_len, LANES), 0) // blk
    km_row = lax.broadcasted_iota(jnp.int32, (LANES, LANES), 0)

    heads = []
    for hh in range(2):
        off = (1 - hh) * HEAD_DIM
        hmask = (lane // HEAD_DIM) == hh
        k_aug = jnp.where(hmask, kf, jnp.where(key_lane == key_blk + off, 1.0, 0.0)).astype(BF16)
        v_aug = jnp.where(hmask, vf, 1.0).astype(BF16)
        km = jnp.zeros((LANES, LANES), F32)
        for j in range(n_blk):
            km = jnp.where(km_row == off + j,
                           jnp.mean(kf[j * blk:(j + 1) * blk, :], axis=0, keepdims=True), km)
        heads.append((hh, off, hmask, k_aug, v_aug, km.astype(BF16)))

    def logits(i, head):
        hh, off, hmask, k_aug, _, km = head
        qi = q_ref[i * blk:(i + 1) * blk, :]
        if i > MOBA_TOP_K:
            gate = _dot_nt(jnp.where(hmask, qi, jnp.zeros_like(qi)), km)
            keep = _rank_select(gate, i, axis=1, first=off)
            past = jnp.logical_and(lane >= off, lane < off + i)
            drop = jnp.where(past, jnp.where(keep, 0.0, MASKED), 0.0)
            q_aug = jnp.where(hmask, qi, drop.astype(BF16))
        else:
            q_aug = jnp.where(hmask, qi, jnp.zeros_like(qi))
        for j in range(i + 1):
            s = _dot_nt(q_aug, k_aug[j * blk:(j + 1) * blk, :])
            if j == i:
                s = s + bias_ref[hh, 0]
            elif j == i - 1:
                s = s + bias_ref[hh, 1]
            yield s

    def weighted_values(s_list, head):
        v_aug = head[4]
        m = jnp.max(functools.reduce(jnp.maximum, s_list), axis=-1, keepdims=True)
        acc = jnp.zeros((blk, LANES), F32)
        for j, s in enumerate(s_list):
            acc = acc + _dot(jnp.exp2(s - m).astype(BF16), v_aug[j * blk:(j + 1) * blk, :])
        return acc

    units = [(i, head) for i in range(n_blk) for head in heads]
    low = lane < HEAD_DIM
    ahead = 2
    pending = [list(logits(*unit)) for unit in units[:ahead]]
    accs = []
    for u, (i, head) in enumerate(units):
        if u + ahead < len(units):
            pending.append(list(logits(*units[u + ahead])))
        accs.append(weighted_values(pending.pop(0), head))
        if len(accs) == 2:
            num = jnp.where(low, accs[0], accs[1])
            den = pltpu.roll(jnp.where(low, accs[1], accs[0]), HEAD_DIM, axis=1)
            o_ref[i * blk:(i + 1) * blk, :] = (num * (1.0 / den)).astype(o_ref.dtype)
            accs = []


def _moba_prompt(q, k, v, rel_bias):
    b, s, d = q.shape
    n_heads = d // HEAD_DIM
    blk = MOBA_BLOCK
    n_blk = s // blk
    assert s % blk == 0 and n_heads % 2 == 0 and 2 * HEAD_DIM == LANES
    hspec = pl.BlockSpec((None, s, LANES), lambda p, bi, rb_ref: (bi, 0, p))
    return pl.pallas_call(
        functools.partial(_moba_prompt_kernel, n_blk=n_blk, n_heads=n_heads),
        grid_spec=pltpu.PrefetchScalarGridSpec(
            num_scalar_prefetch=1,
            grid=(n_heads // 2, b),
            in_specs=[hspec, hspec, hspec],
            out_specs=hspec,
            scratch_shapes=[pltpu.VMEM((2, 2, blk, blk), F32)],
        ),
        out_shape=jax.ShapeDtypeStruct((b, s, d), BF16),
        compiler_params=_cparams(("arbitrary", "arbitrary")),
        name="moba_prompt",
    )(rel_bias.reshape(-1), q, k, v)


def _moba_sample_kernel(pt_ref, rb_ref, *refs, n_groups, n_blk, n_heads, t):
    pp = PAGES_PER_STEP
    k_refs = refs[0:pp]
    v_refs = refs[pp:2 * pp]
    (q_ref, kn_ref, vn_ref, o_ref, a_ref, s_ref, p_ref, pown_ref, l_ref, acc_ref,
     blast_ref, bown_ref, gate_ref, bmax_ref) = refs[2 * pp:]
    del pt_ref
    seq = pl.program_id(0)
    step = pl.program_id(1)
    blk = MOBA_BLOCK
    ppb = blk // PAGE_SIZE
    bps = pp // ppb
    d = q_ref.shape[-1]
    cols = n_heads * t
    row_head = lax.broadcasted_iota(jnp.int32, (cols, d), 0) // t
    lane_head = lax.broadcasted_iota(jnp.int32, (cols, d), 1) // HEAD_DIM
    diag = row_head == lane_head
    lane = lax.broadcasted_iota(jnp.int32, (cols, LANES), 1)

    def new_rows(ref):
        return jnp.concatenate([ref[...], jnp.zeros((PAGE_SIZE - t, ref.shape[-1]), F32)],
                               axis=0).astype(BF16)

    def block_of(page_refs, r):
        return jnp.concatenate([page_refs[ppb * r + i][...] for i in range(ppb)],
                               axis=1).astype(BF16)

    @pl.when(jnp.logical_and(seq == 0, step == 0))
    def _():
        qi = lax.broadcasted_iota(jnp.int32, (t, blk), 0)
        ki = lax.broadcasted_iota(jnp.int32, (t, blk), 1)
        qo = lax.broadcasted_iota(jnp.int32, (t, PAGE_SIZE), 0)
        ko = lax.broadcasted_iota(jnp.int32, (t, PAGE_SIZE), 1)
        for h in range(n_heads):
            table = [rb_ref[b * n_heads + h] for b in range(NUM_BUCKETS)]
            far = table[NUM_BUCKETS - 1]
            rows = slice(h * t, (h + 1) * t)
            blast_ref[rows, :] = _bias_of_distance(blk + qi - ki, table) - far
            bown_ref[rows, :] = jnp.where(ko <= qo, _bias_of_distance(qo - ko, table) - far,
                                          -jnp.inf)

    @pl.when(step == 0)
    def _():
        qt = jnp.tile(q_ref[...], (n_heads, 1))
        a_ref[...] = jnp.where(diag, qt, 0.0).astype(BF16)
        acc_ref[...] = jnp.zeros_like(acc_ref)
        gate_ref[...] = jnp.full_like(gate_ref, -jnp.inf)
        bmax_ref[...] = jnp.full_like(bmax_ref, -jnp.inf)

    @pl.when(step < n_groups)
    def _():
        a = a_ref[...]
        gate = gate_ref[...]
        bmax = bmax_ref[...]
        for r in range(bps):
            n = step * bps + r
            s = _dot(a, block_of(k_refs, r))
            s_ref[n] = s
            gate = jnp.where(lane == n, jnp.sum(s, axis=1, keepdims=True), gate)
            bmax = jnp.where(lane == n, jnp.max(s, axis=1, keepdims=True), bmax)
        gate_ref[...] = gate
        bmax_ref[...] = bmax

    @pl.when(step == n_groups)
    def _():
        lane_f = lane.astype(F32)
        g = gate_ref[...]
        picks = []
        for _ in range(MOBA_TOP_K):
            top = jnp.max(g, axis=1, keepdims=True)
            pick = jnp.min(jnp.where(g == top, lane_f, float(LANES)), axis=1, keepdims=True)
            picks.append(pick)
            g = jnp.where(lane_f == pick, -jnp.inf, g)

        def picked(n):
            hit = jnp.where(picks[0] == n, 1.0, 0.0)
            for pick in picks[1:]:
                hit = jnp.where(pick == n, 1.0, hit)
            return hit > 0.0

        last = n_blk - 1
        s_own = _dot_nt(a_ref[...], new_rows(kn_ref)) + bown_ref[...]
        bmax = jnp.where(lane == last,
                         jnp.max(s_ref[last] + blast_ref[...], axis=1, keepdims=True), bmax_ref[...])
        m = jnp.maximum(jnp.max(s_own, axis=1, keepdims=True),
                        jnp.max(jnp.where(picked(lane_f), bmax, -jnp.inf), axis=1, keepdims=True))
        p_own = jnp.exp(s_own - m)
        lsum = jnp.zeros((cols, blk), F32)
        for n in range(n_blk):
            sn = s_ref[n] + blast_ref[...] if n == last else s_ref[n]
            p = jnp.where(picked(float(n)), jnp.exp(sn - m), 0.0)
            lsum = lsum + p
            p_ref[n] = p.astype(BF16)
        l = jnp.sum(lsum, axis=1, keepdims=True) + jnp.sum(p_own, axis=1, keepdims=True)
        pown_ref[...] = p_own.astype(BF16)
        l_ref[...] = jnp.broadcast_to(l, (cols, LANES))

    @pl.when(step >= n_groups)
    def _():
        acc = acc_ref[...]
        for r in range(bps):
            acc = acc + _dot_nt(p_ref[(step - n_groups) * bps + r], block_of(v_refs, r))
        acc_ref[...] = acc

    @pl.when(step == 2 * n_groups - 1)
    def _():
        acc = acc_ref[...] + _dot(pown_ref[...], new_rows(vn_ref))
        acc = acc * jnp.tile(1.0 / l_ref[...], (1, d // LANES))
        acc = jnp.where(diag, acc, 0.0)
        o_ref[...] = jnp.sum(acc.reshape(n_heads, t, d), axis=0).astype(o_ref.dtype)


def _moba_sample(q, k_new, v_new, cache_k, cache_v, page_table, rel_bias):
    nseq, t, d = q.shape
    n_heads = d // HEAD_DIM
    n_pages = page_table.shape[1]
    past = n_pages * PAGE_SIZE
    blk = MOBA_BLOCK
    n_blk = past // blk
    pp = PAGES_PER_STEP
    cols = n_heads * t
    assert past % blk == 0 and blk % PAGE_SIZE == 0 and pp % (blk // PAGE_SIZE) == 0
    assert n_pages % pp == 0 and cols % SUBLANES == 0 and t % SUBLANES == 0 and t <= PAGE_SIZE
    assert MOBA_TOP_K <= n_blk <= LANES
    n_groups = n_pages // pp

    def kmap(r):
        return lambda b, s, pt, rb: (pt[b * n_pages + jnp.minimum(s, n_groups - 1) * pp + r], 0, 0)

    def vmap_(r):
        return lambda b, s, pt, rb: (pt[b * n_pages + jnp.maximum(s - n_groups, 0) * pp + r], 0, 0)

    seq = pl.BlockSpec((None, t, d), lambda b, s, pt, rb: (b, 0, 0))
    in_specs = ([pl.BlockSpec((None, d, PAGE_SIZE), kmap(r)) for r in range(pp)]
                + [pl.BlockSpec((None, d, PAGE_SIZE), vmap_(r)) for r in range(pp)]
                + [seq, seq, seq])
    return pl.pallas_call(
        functools.partial(_moba_sample_kernel, n_groups=n_groups, n_blk=n_blk, n_heads=n_heads, t=t),
        grid_spec=pltpu.PrefetchScalarGridSpec(
            num_scalar_prefetch=2,
            grid=(nseq, 2 * n_groups),
            in_specs=in_specs,
            out_specs=seq,
            scratch_shapes=[
                pltpu.VMEM((cols, d), BF16),
                pltpu.VMEM((n_blk, cols, blk), F32),
                pltpu.VMEM((n_blk, cols, blk), BF16),
                pltpu.VMEM((cols, PAGE_SIZE), BF16),
                pltpu.VMEM((cols, LANES), F32),
                pltpu.VMEM((cols, d), F32),
                pltpu.VMEM((cols, blk), F32),
                pltpu.VMEM((cols, PAGE_SIZE), F32),
                pltpu.VMEM((cols, LANES), F32),
                pltpu.VMEM((cols, LANES), F32),
            ],
        ),
        out_shape=jax.ShapeDtypeStruct((nseq, t, d), BF16),
        compiler_params=_cparams(("arbitrary", "arbitrary")),
        name="moba_sample",
    )(page_table.reshape(-1), rel_bias.reshape(-1), *([cache_k] * pp), *([cache_v] * pp),
      q, k_new, v_new)


def _tile_rows(s):
    for t in (512, 256, 128, 64, 32, 16, 8):
        if s % t == 0:
            return t
    raise ValueError(f"sequence length {s} is not a multiple of {SUBLANES}")


def _trunk(x, mod, hist, past, page_table, w, *, nb, t, nb_conv):
    nseq, s, d = x.shape
    n_heads = d // HEAD_DIM
    depth = w["norm_g"].shape[0]
    hists, new_k, new_v = [], [], []
    attn = None
    for layer in range(depth):
        jm = layer // 2
        ffn = functools.partial(_ffn, mod=mod, norm_g=w["norm_g"], wg=w["wg"], wu=w["wu"],
                                wd=w["wd"], layer=layer, nb=nb, t=t)
        x = ffn(x, sub=0)
        if layer % 2 == 0:
            x, st = _conv_mixer(x, mod, w["norm_g"], hist[jm], w["pw1"], w["b_pw1"], w["dw"],
                                w["b_dw"], w["ln_g"], w["ln_b"], w["pw2"], w["b_pw2"],
                                layer=layer, jm=jm, nb=nb_conv, t=t)
            hists.append(st)
            attn = None
        else:
            scale = 1.0 / math.sqrt(HEAD_DIM)
            q, k, v = _qkv(x, mod, w["norm_g"], w["qkv"], layer=layer, jm=jm, nb=nb, t=t,
                           q_dtype=BF16 if past is None else F32,
                           q_scale=scale * LOG2E if past is None else scale)
            if past is None:
                attn = _moba_prompt(q, k, v, w["rel_bias"])
            else:
                ck, cv, n_phys = past
                attn = _moba_sample(q, k, v, ck, cv, page_table + jm * n_phys, w["rel_bias"])
            new_k.append(k.reshape(nseq, s, n_heads, HEAD_DIM))
            new_v.append(v.reshape(nseq, s, n_heads, HEAD_DIM))
        last = layer == depth - 1
        x = ffn(x, sub=2, attn=attn, wo=w["wo"] if attn is not None else None, wo_idx=jm,
                final_g=w["final_g"] if last else None)
    return x, jnp.stack(hists), jnp.stack(new_k), jnp.stack(new_v)


def kernel(x_prompt, x_sample, state_conv, cache_k, cache_v, page_table, c_prompt, c_sample, norm_g, ada_w, ada_b, ffn_w_gate, ffn_w_up, ffn_w_down, conv_w_pw1, conv_b_pw1, conv_w_dw, conv_b_dw, conv_ln_g, conv_ln_b, conv_w_pw2, conv_b_pw2, attn_w_qkv, attn_w_o, rel_bias, final_norm_g):
    b, s, d = x_prompt.shape
    nd, ds_, _ = x_sample.shape
    depth = norm_g.shape[0]
    w = dict(
        norm_g=norm_g.reshape(depth, 3, 1, d),
        wg=ffn_w_gate.astype(BF16), wu=ffn_w_up.astype(BF16), wd=ffn_w_down.astype(BF16),
        pw1=conv_w_pw1.astype(BF16), b_pw1=conv_b_pw1, dw=conv_w_dw, b_dw=conv_b_dw,
        ln_g=conv_ln_g, ln_b=conv_ln_b, pw2=conv_w_pw2.astype(BF16), b_pw2=conv_b_pw2,
        qkv=attn_w_qkv.astype(BF16), wo=attn_w_o.astype(BF16), rel_bias=rel_bias,
        final_g=final_norm_g,
    )
    mod_p, mod_s = _ada_mod(c_prompt, c_sample, ada_w, ada_b)

    n_layers, n_phys = cache_k.shape[:2]
    pages = lambda c: jnp.transpose(c, (0, 1, 3, 4, 2)).reshape(n_layers * n_phys, d, PAGE_SIZE)

    hist0 = jnp.zeros((conv_w_dw.shape[0], b, CONV_WIDTH - 1, conv_w_dw.shape[2]), F32)
    y_p, st_p, k_p, v_p = _trunk(x_prompt, mod_p, hist0, None, None, w,
                                 nb=1, t=_tile_rows(s), nb_conv=1)
    y_s, st_s, k_s, v_s = _trunk(x_sample, mod_s, state_conv,
                                 (pages(cache_k), pages(cache_v), n_phys), page_table, w,
                                 nb=nd, t=ds_, nb_conv=SUBLANES)
    return (y_p, y_s, st_p, st_s, k_p, v_p, k_s, v_s)
```

```python
import functools
import math

import numpy as np
import jax
import jax.numpy as jnp
from jax import lax
from jax.experimental import pallas as pl
from jax.experimental.pallas import tpu as pltpu

F32 = jnp.float32
BF16 = jnp.bfloat16

HEAD_DIM = 64
CONV_WIDTH = 31
MOBA_BLOCK = 256
MOBA_TOP_K = 3
PAGE_SIZE = 128
NUM_BUCKETS = 32
MAX_DISTANCE = 128
NORM_EPS = 1e-6

LANES = 128
SUBLANES = 8
HIST_PAD = 32
VMEM_LIMIT = 56 * 1024 * 1024
PAGES_PER_STEP = 8
LOG2E = math.log2(math.e)
MASKED = -1e30


def _cparams(sem):
    return pltpu.CompilerParams(dimension_semantics=sem, vmem_limit_bytes=VMEM_LIMIT)


def _resident(block_shape, index_map):
    return pl.BlockSpec(block_shape, index_map, pipeline_mode=pl.Buffered(1))


def _mod_norm(x, g, shift, scale):
    ms = jnp.mean(x * x, axis=-1, keepdims=True)
    n = x * lax.rsqrt(ms + NORM_EPS)
    return n * g * (1.0 + scale) + shift


def _dot(a, b):
    return jnp.dot(a, b, preferred_element_type=F32)


def _dot_nt(a, b):
    return lax.dot_general(a, b, (((1,), (1,)), ((), ())), preferred_element_type=F32)


def _dot_tn(a, b):
    return lax.dot_general(a, b, (((0,), (0,)), ((), ())), preferred_element_type=F32)


def _mod_kernel(cp_ref, cs_ref, w_ref, b_ref, op_ref, os_ref):
    w = w_ref[...].astype(BF16)
    op_ref[...] = _dot(jax.nn.silu(cp_ref[...]).astype(BF16), w) + b_ref[...]
    os_ref[...] = _dot(jax.nn.silu(cs_ref[...]).astype(BF16), w) + b_ref[...]


def _ada_mod(c_prompt, c_sample, ada_w, ada_b):
    d = c_prompt.shape[1]
    depth = ada_w.shape[0]
    n_mod = ada_w.shape[2] // d
    outs = pl.pallas_call(
        _mod_kernel,
        grid=(depth, n_mod),
        in_specs=[
            pl.BlockSpec(c_prompt.shape, lambda l, j: (0, 0)),
            pl.BlockSpec(c_sample.shape, lambda l, j: (0, 0)),
            pl.BlockSpec((None, d, d), lambda l, j: (l, 0, j)),
            pl.BlockSpec((None, None, 1, d), lambda l, j: (l, j, 0, 0)),
        ],
        out_specs=[pl.BlockSpec((None, None) + c.shape, lambda l, j: (l, j, 0, 0))
                   for c in (c_prompt, c_sample)],
        out_shape=[jax.ShapeDtypeStruct((depth, n_mod) + c.shape, F32)
                   for c in (c_prompt, c_sample)],
        compiler_params=_cparams(("arbitrary", "arbitrary")),
        name="ada_mod",
    )(c_prompt, c_sample, ada_w, ada_b.reshape(depth, n_mod, 1, d))
    return [o.reshape(depth, n_mod, o.shape[2], 1, d) for o in outs]


def _ffn_kernel(*refs, shapes, ff_chunk, pre_attn, final_norm):
    it = iter(refs)
    groups = []
    for _ in shapes:
        x_ref, mod_ref = next(it), next(it)
        a_ref, gate1_ref = (next(it), next(it)) if pre_attn else (None, None)
        groups.append((x_ref, mod_ref, a_ref, gate1_ref))
    g_ref = next(it)
    wo_ref = next(it) if pre_attn else None
    wg_ref, wu_ref, wd_ref = next(it), next(it), next(it)
    fg_ref = next(it) if final_norm else None
    o_refs = [next(it) for _ in shapes]
    act_ref = next(it)
    d_ff = wg_ref.shape[-1]

    def run(k):
        nb, t = shapes[k]
        x_ref, mod_ref, a_ref, gate1_ref = groups[k]
        x = x_ref[...]
        d = x.shape[-1]
        if pre_attn:
            y = _dot(a_ref[...].reshape(nb * t, d), wo_ref[...])
            x = x + gate1_ref[0] * y.reshape(nb, t, d)
        h = _mod_norm(x, g_ref[...], mod_ref[0], mod_ref[1])
        hb = h.reshape(nb * t, d).astype(BF16)
        for j in range(d_ff // ff_chunk):
            sl = slice(j * ff_chunk, (j + 1) * ff_chunk)
            gj = _dot(hb, wg_ref[:, sl])
            uj = _dot(hb, wu_ref[:, sl])
            act_ref[0:nb * t, sl] = (jax.nn.silu(gj) * uj).astype(BF16)
        y = _dot(act_ref[0:nb * t, :], wd_ref[...])
        out = x + 0.5 * mod_ref[2] * y.reshape(nb, t, d)
        if final_norm:
            ms = jnp.mean(out * out, axis=-1, keepdims=True)
            out = out * lax.rsqrt(ms + NORM_EPS) * fg_ref[...]
        o_refs[k][...] = out

    run(0)
    if len(shapes) > 1:
        first_step = jnp.logical_and(pl.program_id(0) == 0, pl.program_id(1) == 0)
        pl.when(first_step)(functools.partial(run, 1))


def _ffn(xs, mods, norm_g, wg, wu, wd, *, layer, sub, t, attns=None, wo=None, wo_idx=0,
         final_g=None):
    nseq, s, d = xs[0].shape
    d_ff = wg.shape[-1]
    half = sub // 2
    pre_attn = attns is not None
    final_norm = final_g is not None
    shapes = [(1, t)] + [x.shape[:2] for x in xs[1:]]
    tile = lambda b, i: (b, i, 0)
    whole = lambda b, i: (0, 0, 0)
    in_specs, args, out_specs = [], [], []
    for k, (x, mod) in enumerate(zip(xs, mods)):
        nb, rows = shapes[k]
        seq0 = (lambda b: b) if k == 0 else (lambda b: 0)
        xspec = pl.BlockSpec((nb, rows, d), tile if k == 0 else whole)
        in_specs += [xspec, pl.BlockSpec((None, 3, nb, 1, d),
                                         lambda b, i, seq0=seq0: (layer, sub, seq0(b), 0, 0))]
        args += [x, mod]
        if pre_attn:
            gate1 = 3 * (sub - 1) + 2
            in_specs += [xspec, pl.BlockSpec((None, 1, nb, 1, d),
                                             lambda b, i, seq0=seq0: (layer, gate1, seq0(b), 0, 0))]
            args += [attns[k], mod]
        out_specs.append(xspec)
    in_specs.append(pl.BlockSpec((None, None, 1, d), lambda b, i: (layer, sub, 0, 0)))
    args.append(norm_g)
    if pre_attn:
        in_specs.append(_resident((None, d, d), lambda b, i: (wo_idx, 0, 0)))
        args.append(wo)
    in_specs += [
        _resident((None, None, d, d_ff), lambda b, i: (layer, half, 0, 0)),
        _resident((None, None, d, d_ff), lambda b, i: (layer, half, 0, 0)),
        _resident((None, None, d_ff, d), lambda b, i: (layer, half, 0, 0)),
    ]
    args += [wg, wu, wd]
    if final_norm:
        in_specs.append(pl.BlockSpec((1, d), lambda b, i: (0, 0)))
        args.append(final_g.reshape(1, d))
    ff_chunk = 256 if d_ff % 256 == 0 else LANES
    return pl.pallas_call(
        functools.partial(_ffn_kernel, shapes=shapes, ff_chunk=ff_chunk, pre_attn=pre_attn,
                          final_norm=final_norm),
        grid=(nseq, s // t),
        in_specs=in_specs,
        out_specs=out_specs,
        out_shape=[jax.ShapeDtypeStruct(x.shape, F32) for x in xs],
        scratch_shapes=[pltpu.VMEM((max(nb * rows for nb, rows in shapes), d_ff), BF16)],
        compiler_params=_cparams(("arbitrary", "arbitrary")),
        name="ffn",
    )(*args)


def _conv_kernel(x_ref, mod_ref, g_ref, hist_ref, w1_ref, b1_ref, wdw_ref, bdw_ref, lng_ref,
                 lnb_ref, w2_ref, b2_ref, o_ref, st_ref, full_ref, y_ref, shift_ref, *, nb, t,
                 n_tiles, n_parts, row_blk, lane_blk):
    c = w2_ref.shape[0]
    d = x_ref.shape[-1]
    first = HIST_PAD - (CONV_WIDTH - 1)
    shift_rows = shift_ref.shape[2]

    @pl.when(pl.program_id(1) == 0)
    def _():
        full_ref[:, 0:HIST_PAD, :] = hist_ref[...]

    part = t // n_parts

    def glu(p):
        xp = x_ref[:, p * part:(p + 1) * part, :]
        h = _mod_norm(xp, g_ref[...], mod_ref[0], mod_ref[1])
        a = _dot(h.reshape(nb * part, d).astype(BF16), w1_ref[...]) + b1_ref[...]
        u = a[:, :c] * jax.nn.sigmoid(a[:, c:])
        full_ref[:, HIST_PAD + p * part:HIST_PAD + (p + 1) * part, :] = u.reshape(nb, part, c)

    def depthwise(p):
        row0 = p * part
        for c0 in range(0, c, lane_blk):
            lanes = slice(c0, c0 + lane_blk)
            for s in range(1, SUBLANES):
                shift_ref[s - 1] = full_ref[:, row0 + s:row0 + s + shift_rows, lanes]
            for r0 in range(0, part, row_blk):
                acc = jnp.zeros((nb * row_blk // SUBLANES, SUBLANES, lane_blk), F32)
                for w in range(CONV_WIDTH):
                    base, s = divmod(first + w, SUBLANES)
                    lo = base * SUBLANES + r0
                    src = (full_ref[:, row0 + lo:row0 + lo + row_blk, lanes] if s == 0
                           else shift_ref[s - 1, :, lo:lo + row_blk, :])
                    acc = acc + src.reshape(acc.shape) * wdw_ref[w, :, lanes][None]
                y_ref[:, row0 + r0:row0 + r0 + row_blk, lanes] = (
                    acc.reshape(nb, row_blk, lane_blk) + bdw_ref[:, lanes])

    def project(p):
        rows = slice(p * part, (p + 1) * part)
        y = y_ref[:, rows, :].reshape(nb * part, c)
        mu = jnp.mean(y, axis=-1, keepdims=True)
        yc = y - mu
        var = jnp.mean(yc * yc, axis=-1, keepdims=True)
        yn = yc * lax.rsqrt(var + NORM_EPS) * lng_ref[...] + lnb_ref[...]
        out = _dot(jax.nn.silu(yn).astype(BF16), w2_ref[...]) + b2_ref[...]
        o_ref[:, rows, :] = x_ref[:, rows, :] + mod_ref[2] * out.reshape(nb, part, d)

    glu(0)
    for p in range(n_parts):
        if p + 1 < n_parts:
            glu(p + 1)
        depthwise(p)
        project(p)

    st_ref[...] = full_ref[:, t + first:t + HIST_PAD, :]
    if n_tiles > 1:
        full_ref[:, 0:HIST_PAD, :] = full_ref[:, t:t + HIST_PAD, :]


def _conv_mixer(x, mod, norm_g, hist, w1, b1, wdw, bdw, lng, lnb, w2, b2, *, layer, jm, nb, t):
    nseq, s, d = x.shape
    c = w2.shape[1]
    n_tiles = s // t
    acc_vregs = 16
    row_blk = min(t, 64)
    n_parts = 2 if t % (2 * row_blk) == 0 else 1
    part = t // n_parts
    lane_blk = max(LANES, min(c, acc_vregs * SUBLANES * LANES // (nb * row_blk) // LANES * LANES))
    xspec = pl.BlockSpec((nb, t, d), lambda b, i: (b, i, 0))
    const = lambda b, i: (jm, 0, 0)
    vec = lambda a: a.reshape(a.shape[0], 1, a.shape[1])
    hist32 = jnp.pad(hist, ((0, 0), (HIST_PAD - (CONV_WIDTH - 1), 0), (0, 0)))
    return pl.pallas_call(
        functools.partial(_conv_kernel, nb=nb, t=t, n_tiles=n_tiles, n_parts=n_parts,
                          row_blk=row_blk, lane_blk=lane_blk),
        grid=(nseq // nb, n_tiles),
        in_specs=[
            xspec,
            pl.BlockSpec((None, 3, nb, 1, d), lambda b, i: (layer, 1, b, 0, 0)),
            pl.BlockSpec((None, None, 1, d), lambda b, i: (layer, 1, 0, 0)),
            pl.BlockSpec((nb, HIST_PAD, c), lambda b, i: (b, 0, 0)),
            _resident((None, d, 2 * c), const),
            pl.BlockSpec((None, 1, 2 * c), const),
            pl.BlockSpec((None, CONV_WIDTH, SUBLANES, c), lambda b, i: (jm, 0, 0, 0)),
            pl.BlockSpec((None, 1, c), const),
            pl.BlockSpec((None, 1, c), const),
            pl.BlockSpec((None, 1, c), const),
            _resident((None, c, d), const),
            pl.BlockSpec((None, 1, d), const),
        ],
        out_specs=[xspec, pl.BlockSpec((nb, CONV_WIDTH - 1, c), lambda b, i: (b, 0, 0))],
        out_shape=[jax.ShapeDtypeStruct(x.shape, F32),
                   jax.ShapeDtypeStruct((nseq, CONV_WIDTH - 1, c), F32)],
        scratch_shapes=[pltpu.VMEM((nb, HIST_PAD + t, c), F32), pltpu.VMEM((nb, t, c), F32),
                        pltpu.VMEM((SUBLANES - 1, nb, HIST_PAD - SUBLANES + part, lane_blk), F32)],
        compiler_params=_cparams(("arbitrary", "arbitrary")),
        name="conv_mixer",
    )(x, mod, norm_g, hist32, w1, vec(b1),
      jnp.broadcast_to(wdw[:, :, None, :], wdw.shape[:2] + (SUBLANES, c)),
      vec(bdw), vec(lng), vec(lnb), w2, vec(b2))


def _qkv_kernel(x_ref, mod_ref, g_ref, w_ref, q_ref, k_ref, v_ref, *, nb, t, q_scale):
    x = x_ref[...]
    d = x.shape[-1]
    h = _mod_norm(x, g_ref[...], mod_ref[0], mod_ref[1])
    hb = h.reshape(nb * t, d).astype(BF16)
    q = _dot(hb, w_ref[:, 0:d]) * q_scale
    q_ref[...] = q.reshape(nb, t, d).astype(q_ref.dtype)
    k_ref[...] = _dot(hb, w_ref[:, d:2 * d]).reshape(nb, t, d)
    v_ref[...] = _dot(hb, w_ref[:, 2 * d:3 * d]).reshape(nb, t, d)


def _qkv(x, mod, norm_g, w, *, layer, jm, nb, t, q_dtype, q_scale):
    nseq, s, d = x.shape
    xspec = pl.BlockSpec((nb, t, d), lambda b, i: (b, i, 0))
    return pl.pallas_call(
        functools.partial(_qkv_kernel, nb=nb, t=t, q_scale=q_scale),
        grid=(nseq // nb, s // t),
        in_specs=[
            xspec,
            pl.BlockSpec((None, 3, nb, 1, d), lambda b, i: (layer, 1, b, 0, 0)),
            pl.BlockSpec((None, None, 1, d), lambda b, i: (layer, 1, 0, 0)),
            _resident((None, d, 3 * d), lambda b, i: (jm, 0, 0)),
        ],
        out_specs=[xspec, xspec, xspec],
        out_shape=[jax.ShapeDtypeStruct(x.shape, q_dtype), jax.ShapeDtypeStruct(x.shape, F32),
                   jax.ShapeDtypeStruct(x.shape, F32)],
        compiler_params=_cparams(("arbitrary", "arbitrary")),
        name="qkv",
    )(x, mod, norm_g, w)


def _bucket_of_distance(n_dist):
    dist = np.arange(n_dist)
    max_exact = NUM_BUCKETS // 2
    nf = np.maximum(dist, max_exact).astype(np.float32)
    large = max_exact + (np.log(nf / max_exact) / math.log(MAX_DISTANCE / max_exact)
                         * (NUM_BUCKETS - max_exact)).astype(np.int32)
    large = np.minimum(large, NUM_BUCKETS - 1)
    return np.where(dist < max_exact, dist, large).astype(np.int32)


_BUCKETS = _bucket_of_distance(4 * MAX_DISTANCE)
assert _BUCKETS[-1] == NUM_BUCKETS - 1 and np.all(np.diff(_BUCKETS) >= 0)
_BUCKET_START = [int(np.argmax(_BUCKETS >= b)) for b in range(NUM_BUCKETS)]
assert _BUCKET_START[NUM_BUCKETS - 1] <= MOBA_BLOCK


def _bias_of_distance(dist, table):
    bias = jnp.full(dist.shape, table[0], F32)
    for b in range(1, NUM_BUCKETS):
        bias = jnp.where(dist >= _BUCKET_START[b], table[b], bias)
    return bias


def _rank_select(gate, n_valid, axis, first=0):
    idx = lax.broadcasted_iota(jnp.int32, gate.shape, axis)
    cnt = jnp.zeros(gate.shape, F32)
    for j in range(first, first + n_valid):
        gj = lax.slice_in_dim(gate, j, j + 1, axis=axis)
        tie = jnp.where(idx > j, 1.0, 0.0)
        cnt = cnt + jnp.where(gj > gate, 1.0, jnp.where(gj == gate, tie, 0.0))
    return cnt < float(MOBA_TOP_K)


def _moba_prompt_kernel(rb_ref, q_ref, k_ref, v_ref, o_ref, bias_ref, *, n_blk, n_heads):
    pair = pl.program_id(0)
    blk = MOBA_BLOCK

    @pl.when(pl.program_id(1) == 0)
    def _():
        dist = (lax.broadcasted_iota(jnp.int32, (blk, blk), 0)
                - lax.broadcasted_iota(jnp.int32, (blk, blk), 1))
        for hh in range(2):
            table = [rb_ref[b * n_heads + 2 * pair + hh] for b in range(NUM_BUCKETS)]
            far = table[NUM_BUCKETS - 1]
            own = (_bias_of_distance(dist, table) - far) * LOG2E
            bias_ref[hh, 0] = jnp.where(dist >= 0, own, -jnp.inf)
            bias_ref[hh, 1] = (_bias_of_distance(dist + blk, table) - far) * LOG2E

    kf = k_ref[...]
    vf = v_ref[...]
    s_len = kf.shape[0]
    lane = lax.broadcasted_iota(jnp.int32, (1, LANES), 1)
    key_lane = lax.broadcasted_iota(jnp.int32, (s_len, LANES), 1)
    key_blk = lax.broadcasted_iota(jnp.int32, (s_len, LANES), 0) // blk
    km_row = lax.broadcasted_iota(jnp.int32, (LANES, LANES), 0)

    heads = []
    for hh in range(2):
        off = (1 - hh) * HEAD_DIM
        hmask = (lane // HEAD_DIM) == hh
        k_aug = jnp.where(hmask, kf, jnp.where(key_lane == key_blk + off, 1.0, 0.0)).astype(BF16)
        v_aug = jnp.where(hmask, vf, 1.0).astype(BF16)
        km = jnp.zeros((LANES, LANES), F32)
        for j in range(n_blk):
            km = jnp.where(km_row == off + j,
                           jnp.mean(kf[j * blk:(j + 1) * blk, :], axis=0, keepdims=True), km)
        heads.append((hh, off, hmask, k_aug, v_aug, km.astype(BF16)))

    def logits(i, head):
        hh, off, hmask, k_aug, _, km = head
        qi = q_ref[i * blk:(i + 1) * blk, :]
        if i > MOBA_TOP_K:
            gate = _dot_nt(jnp.where(hmask, qi, jnp.zeros_like(qi)), km)
            keep = _rank_select(gate, i, axis=1, first=off)
            past = jnp.logical_and(lane >= off, lane < off + i)
            drop = jnp.where(past, jnp.where(keep, 0.0, MASKED), 0.0)
            q_aug = jnp.where(hmask, qi, drop.astype(BF16))
        else:
            q_aug = jnp.where(hmask, qi, jnp.zeros_like(qi))
        for j in range(i + 1):
            s = _dot_nt(q_aug, k_aug[j * blk:(j + 1) * blk, :])
            if j == i:
                s = s + bias_ref[hh, 0]
            elif j == i - 1:
                s = s + bias_ref[hh, 1]
            yield s

    def weighted_values(s_list, head):
        v_aug = head[4]
        m = jnp.max(functools.reduce(jnp.maximum, s_list), axis=-1, keepdims=True)
        acc = jnp.zeros((blk, LANES), F32)
        for j, s in enumerate(s_list):
            acc = acc + _dot(jnp.exp2(s - m).astype(BF16), v_aug[j * blk:(j + 1) * blk, :])
        return acc

    units = [(i, head) for i in range(n_blk) for head in heads]
    low = lane < HEAD_DIM
    ahead = 2
    pending = [list(logits(*unit)) for unit in units[:ahead]]
    accs = []
    for u, (i, head) in enumerate(units):
        if u + ahead < len(units):
            pending.append(list(logits(*units[u + ahead])))
        accs.append(weighted_values(pending.pop(0), head))
        if len(accs) == 2:
            num = jnp.where(low, accs[0], accs[1])
            den = pltpu.roll(jnp.where(low, accs[1], accs[0]), HEAD_DIM, axis=1)
            o_ref[i * blk:(i + 1) * blk, :] = (num * (1.0 / den)).astype(o_ref.dtype)
            accs = []


def _moba_prompt(q, k, v, rel_bias):
    b, s, d = q.shape
    n_heads = d // HEAD_DIM
    blk = MOBA_BLOCK
    n_blk = s // blk
    assert s % blk == 0 and n_heads % 2 == 0 and 2 * HEAD_DIM == LANES
    hspec = pl.BlockSpec((None, s, LANES), lambda p, bi, rb_ref: (bi, 0, p))
    return pl.pallas_call(
        functools.partial(_moba_prompt_kernel, n_blk=n_blk, n_heads=n_heads),
        grid_spec=pltpu.PrefetchScalarGridSpec(
            num_scalar_prefetch=1,
            grid=(n_heads // 2, b),
            in_specs=[hspec, hspec, hspec],
            out_specs=hspec,
            scratch_shapes=[pltpu.VMEM((2, 2, blk, blk), F32)],
        ),
        out_shape=jax.ShapeDtypeStruct((b, s, d), BF16),
        compiler_params=_cparams(("arbitrary", "arbitrary")),
        name="moba_prompt",
    )(rel_bias.reshape(-1), q, k, v)


def _moba_sample_kernel(pt_ref, rb_ref, *refs, n_groups, n_blk, n_heads, t):
    pp = PAGES_PER_STEP
    k_refs = refs[0:pp]
    v_even_refs = refs[pp:2 * pp]
    v_odd_refs = refs[2 * pp:3 * pp]
    (q_ref, kn_ref, vn_ref, o_ref, a_ref, s_ref, p_ref, pown_ref, l_ref, acc_ref,
     blast_ref, bown_ref, gate_ref, bmax_ref) = refs[3 * pp:]
    del pt_ref
    seq = pl.program_id(0)
    step = pl.program_id(1)
    blk = MOBA_BLOCK
    ppb = blk // PAGE_SIZE
    bps = pp // ppb
    d = q_ref.shape[-1]
    cols = n_heads * t
    row_head = lax.broadcasted_iota(jnp.int32, (cols, d), 0) // t
    lane_head = lax.broadcasted_iota(jnp.int32, (cols, d), 1) // HEAD_DIM
    diag = row_head == lane_head
    lane = lax.broadcasted_iota(jnp.int32, (cols, LANES), 1)

    def new_rows(ref):
        return jnp.concatenate([ref[...], jnp.zeros((PAGE_SIZE - t, ref.shape[-1]), F32)],
                               axis=0).astype(BF16)

    def block_of(page_refs, r):
        return jnp.concatenate([page_refs[ppb * r + i][...] for i in range(ppb)],
                               axis=1).astype(BF16)

    @pl.when(jnp.logical_and(seq == 0, step == 0))
    def _():
        qi = lax.broadcasted_iota(jnp.int32, (t, blk), 0)
        ki = lax.broadcasted_iota(jnp.int32, (t, blk), 1)
        qo = lax.broadcasted_iota(jnp.int32, (t, PAGE_SIZE), 0)
        ko = lax.broadcasted_iota(jnp.int32, (t, PAGE_SIZE), 1)
        for h in range(n_heads):
            table = [rb_ref[b * n_heads + h] for b in range(NUM_BUCKETS)]
            far = table[NUM_BUCKETS - 1]
            rows = slice(h * t, (h + 1) * t)
            blast_ref[rows, :] = _bias_of_distance(blk + qi - ki, table) - far
            bown_ref[rows, :] = jnp.where(ko <= qo, _bias_of_distance(qo - ko, table) - far,
                                          -jnp.inf)

    @pl.when(step == 0)
    def _():
        qt = jnp.tile(q_ref[...], (n_heads, 1))
        a_ref[...] = jnp.where(diag, qt, 0.0).astype(BF16)
        acc_ref[...] = jnp.zeros_like(acc_ref)
        gate_ref[...] = jnp.full_like(gate_ref, -jnp.inf)
        bmax_ref[...] = jnp.full_like(bmax_ref, -jnp.inf)

    @pl.when(step < n_groups)
    def _():
        a = a_ref[...]
        gate = gate_ref[...]
        bmax = bmax_ref[...]
        for r in range(bps):
            n = step * bps + r
            s = _dot(a, block_of(k_refs, r))
            s_ref[n] = s
            gate = jnp.where(lane == n, jnp.sum(s, axis=1, keepdims=True), gate)
            bmax = jnp.where(lane == n, jnp.max(s, axis=1, keepdims=True), bmax)
        gate_ref[...] = gate
        bmax_ref[...] = bmax

    @pl.when(step == n_groups)
    def _():
        lane_f = lane.astype(F32)
        g = gate_ref[...]
        picks = []
        for _ in range(MOBA_TOP_K):
            top = jnp.max(g, axis=1, keepdims=True)
            pick = jnp.min(jnp.where(g == top, lane_f, float(LANES)), axis=1, keepdims=True)
            picks.append(pick)
            g = jnp.where(lane_f == pick, -jnp.inf, g)

        def picked(n):
            hit = jnp.where(picks[0] == n, 1.0, 0.0)
            for pick in picks[1:]:
                hit = jnp.where(pick == n, 1.0, hit)
            return hit > 0.0

        last = n_blk - 1
        s_own = _dot_nt(a_ref[...], new_rows(kn_ref)) + bown_ref[...]
        bmax = jnp.where(lane == last,
                         jnp.max(s_ref[last] + blast_ref[...], axis=1, keepdims=True), bmax_ref[...])
        m = jnp.maximum(jnp.max(s_own, axis=1, keepdims=True),
                        jnp.max(jnp.where(picked(lane_f), bmax, -jnp.inf), axis=1, keepdims=True))
        p_own = jnp.exp(s_own - m)
        lsum = jnp.zeros((cols, blk), F32)
        for n in range(n_blk):
            sn = s_ref[n] + blast_ref[...] if n == last else s_ref[n]
            p = jnp.where(picked(float(n)), jnp.exp(sn - m), 0.0)
            lsum = lsum + p
            p_ref[n] = p.astype(BF16)
        l = jnp.sum(lsum, axis=1, keepdims=True) + jnp.sum(p_own, axis=1, keepdims=True)
        pown_ref[...] = p_own.astype(BF16)
        l_ref[...] = jnp.broadcast_to(l, (cols, LANES))

    def weigh_values(v_refs):
        acc = acc_ref[...]
        for r in range(bps):
            acc = acc + _dot_nt(p_ref[(step - n_groups) * bps + r], block_of(v_refs, r))
        acc_ref[...] = acc

    for parity, v_refs in enumerate((v_even_refs, v_odd_refs)):
        pl.when(jnp.logical_and(step >= n_groups, (step - n_groups) % 2 == parity))(
            functools.partial(weigh_values, v_refs))

    @pl.when(step == 2 * n_groups - 1)
    def _():
        acc = acc_ref[...] + _dot(pown_ref[...], new_rows(vn_ref))
        acc = acc * jnp.tile(1.0 / l_ref[...], (1, d // LANES))
        acc = jnp.where(diag, acc, 0.0)
        o_ref[...] = jnp.sum(acc.reshape(n_heads, t, d), axis=0).astype(o_ref.dtype)


def _moba_sample(q, k_new, v_new, cache_k, cache_v, page_table, rel_bias):
    nseq, t, d = q.shape
    n_heads = d // HEAD_DIM
    n_pages = page_table.shape[1]
    past = n_pages * PAGE_SIZE
    blk = MOBA_BLOCK
    n_blk = past // blk
    pp = PAGES_PER_STEP
    cols = n_heads * t
    assert past % blk == 0 and blk % PAGE_SIZE == 0 and pp % (blk // PAGE_SIZE) == 0
    assert n_pages % pp == 0 and cols % SUBLANES == 0 and t % SUBLANES == 0 and t <= PAGE_SIZE
    assert MOBA_TOP_K <= n_blk <= LANES
    g = n_pages // pp
    assert g % 2 == 0

    def kmap(r):
        def index(b, s, pt, rb):
            ahead = jnp.logical_and(s == 2 * g - 1, b + 1 < nseq)
            group = jnp.where(ahead, 0, jnp.minimum(s, g - 1))
            return (pt[jnp.where(ahead, b + 1, b) * n_pages + group * pp + r], 0, 0)
        return index

    def vmap_even(r):
        def index(b, s, pt, rb):
            group = jnp.minimum(2 * ((jnp.maximum(s - g, 0) + 1) // 2), g - 2)
            return (pt[b * n_pages + group * pp + r], 0, 0)
        return index

    def vmap_odd(r):
        def index(b, s, pt, rb):
            early = s < g
            group = jnp.where(early, g - 1, jnp.minimum(2 * (jnp.maximum(s - g, 0) // 2) + 1, g - 1))
            return (pt[jnp.where(early, jnp.maximum(b - 1, 0), b) * n_pages + group * pp + r], 0, 0)
        return index

    seq = pl.BlockSpec((None, t, d), lambda b, s, pt, rb: (b, 0, 0))
    in_specs = ([pl.BlockSpec((None, d, PAGE_SIZE), kmap(r)) for r in range(pp)]
                + [pl.BlockSpec((None, d, PAGE_SIZE), vmap_even(r)) for r in range(pp)]
                + [pl.BlockSpec((None, d, PAGE_SIZE), vmap_odd(r)) for r in range(pp)]
                + [seq, seq, seq])
    return pl.pallas_call(
        functools.partial(_moba_sample_kernel, n_groups=g, n_blk=n_blk, n_heads=n_heads, t=t),
        grid_spec=pltpu.PrefetchScalarGridSpec(
            num_scalar_prefetch=2,
            grid=(nseq, 2 * g),
            in_specs=in_specs,
            out_specs=seq,
            scratch_shapes=[
                pltpu.VMEM((cols, d), BF16),
                pltpu.VMEM((n_blk, cols, blk), F32),
                pltpu.VMEM((n_blk, cols, blk), BF16),
                pltpu.VMEM((cols, PAGE_SIZE), BF16),
                pltpu.VMEM((cols, LANES), F32),
                pltpu.VMEM((cols, d), F32),
                pltpu.VMEM((cols, blk), F32),
                pltpu.VMEM((cols, PAGE_SIZE), F32),
                pltpu.VMEM((cols, LANES), F32),
                pltpu.VMEM((cols, LANES), F32),
            ],
        ),
        out_shape=jax.ShapeDtypeStruct((nseq, t, d), BF16),
        compiler_params=_cparams(("arbitrary", "arbitrary")),
        name="moba_sample",
    )(page_table.reshape(-1), rel_bias.reshape(-1), *([cache_k] * pp), *([cache_v] * 2 * pp),
      q, k_new, v_new)


def _tile_rows(s):
    for t in (512, 256, 128, 64, 32, 16, 8):
        if s % t == 0:
            return t
    raise ValueError(f"sequence length {s} is not a multiple of {SUBLANES}")


def _trunk(xs, mods, hists_in, past, page_table, w, *, t):
    d = xs[0].shape[-1]
    n_heads = d // HEAD_DIM
    depth = w["norm_g"].shape[0]
    tiles = [(1, t), (SUBLANES, xs[1].shape[1])]
    qkv_tiles = [(1, t), xs[1].shape[:2]]
    hists, new_k, new_v = [[], []], [[], []], [[], []]
    for layer in range(depth):
        jm = layer // 2
        ffn = functools.partial(_ffn, mods=mods, norm_g=w["norm_g"], wg=w["wg"], wu=w["wu"],
                                wd=w["wd"], layer=layer, t=t)
        xs = list(ffn(xs, sub=0))
        attns = None
        if layer % 2 == 0:
            for k in range(2):
                xs[k], st = _conv_mixer(xs[k], mods[k], w["norm_g"], hists_in[k][jm], w["pw1"],
                                        w["b_pw1"], w["dw"], w["b_dw"], w["ln_g"], w["ln_b"],
                                        w["pw2"], w["b_pw2"], layer=layer, jm=jm,
                                        nb=tiles[k][0], t=tiles[k][1])
                hists[k].append(st)
        else:
            scale = 1.0 / math.sqrt(HEAD_DIM)
            attns = []
            for k in range(2):
                nseq, s, _ = xs[k].shape
                q, kk, vv = _qkv(xs[k], mods[k], w["norm_g"], w["qkv"], layer=layer, jm=jm,
                                 nb=qkv_tiles[k][0], t=qkv_tiles[k][1],
                                 q_dtype=BF16 if k == 0 else F32,
                                 q_scale=scale * LOG2E if k == 0 else scale)
                if k == 0:
                    attns.append(_moba_prompt(q, kk, vv, w["rel_bias"]))
                else:
                    ck, cv, n_phys = past
                    attns.append(_moba_sample(q, kk, vv, ck, cv, page_table + jm * n_phys,
                                              w["rel_bias"]))
                new_k[k].append(kk.reshape(nseq, s, n_heads, HEAD_DIM))
                new_v[k].append(vv.reshape(nseq, s, n_heads, HEAD_DIM))
        last = layer == depth - 1
        xs = ffn(xs, sub=2, attns=attns, wo=w["wo"] if attns is not None else None, wo_idx=jm,
                 final_g=w["final_g"] if last else None)
    return xs, [jnp.stack(h) for h in hists], [jnp.stack(a) for a in new_k], \
        [jnp.stack(a) for a in new_v]


def kernel(x_prompt, x_sample, state_conv, cache_k, cache_v, page_table, c_prompt, c_sample, norm_g, ada_w, ada_b, ffn_w_gate, ffn_w_up, ffn_w_down, conv_w_pw1, conv_b_pw1, conv_w_dw, conv_b_dw, conv_ln_g, conv_ln_b, conv_w_pw2, conv_b_pw2, attn_w_qkv, attn_w_o, rel_bias, final_norm_g):
    b, s, d = x_prompt.shape
    depth = norm_g.shape[0]
    w = dict(
        norm_g=norm_g.reshape(depth, 3, 1, d),
        wg=ffn_w_gate.astype(BF16), wu=ffn_w_up.astype(BF16), wd=ffn_w_down.astype(BF16),
        pw1=conv_w_pw1.astype(BF16), b_pw1=conv_b_pw1, dw=conv_w_dw, b_dw=conv_b_dw,
        ln_g=conv_ln_g, ln_b=conv_ln_b, pw2=conv_w_pw2.astype(BF16), b_pw2=conv_b_pw2,
        qkv=attn_w_qkv.astype(BF16), wo=attn_w_o.astype(BF16), rel_bias=rel_bias,
        final_g=final_norm_g,
    )
    mod_p, mod_s = _ada_mod(c_prompt, c_sample, ada_w, ada_b)

    n_layers, n_phys = cache_k.shape[:2]
    pages = lambda c: jnp.transpose(c, (0, 1, 3, 4, 2)).reshape(n_layers * n_phys, d, PAGE_SIZE)

    hist0 = jnp.zeros((conv_w_dw.shape[0], b, CONV_WIDTH - 1, conv_w_dw.shape[2]), F32)
    ys, states, ks, vs = _trunk([x_prompt, x_sample], [mod_p, mod_s], [hist0, state_conv],
                                (pages(cache_k), pages(cache_v), n_phys), page_table, w,
                                t=_tile_rows(s))
    return (ys[0], ys[1], states[0], states[1], ks[0], vs[0], ks[1], vs[1])
```

```python
import functools
import math

import numpy as np
import jax
import jax.numpy as jnp
from jax import lax
from jax.experimental import pallas as pl
from jax.experimental.pallas import tpu as pltpu

F32 = jnp.float32
BF16 = jnp.bfloat16

HEAD_DIM = 64
CONV_WIDTH = 31
MOBA_BLOCK = 256
MOBA_TOP_K = 3
PAGE_SIZE = 128
NUM_BUCKETS = 32
MAX_DISTANCE = 128
NORM_EPS = 1e-6

LANES = 128
SUBLANES = 8
HIST_PAD = 32
VMEM_LIMIT = 56 * 1024 * 1024
PAGES_PER_STEP = 8
RING_GROUPS = 4
LOG2E = math.log2(math.e)
MASKED = -1e30


def _cparams(sem):
    return pltpu.CompilerParams(dimension_semantics=sem, vmem_limit_bytes=VMEM_LIMIT)


def _resident(block_shape, index_map):
    return pl.BlockSpec(block_shape, index_map, pipeline_mode=pl.Buffered(1))


def _mod_norm(x, g, shift, scale):
    ms = jnp.mean(x * x, axis=-1, keepdims=True)
    n = x * lax.rsqrt(ms + NORM_EPS)
    return n * g * (1.0 + scale) + shift


def _dot(a, b):
    return jnp.dot(a, b, preferred_element_type=F32)


def _dot_nt(a, b):
    return lax.dot_general(a, b, (((1,), (1,)), ((), ())), preferred_element_type=F32)


def _dot_tn(a, b):
    return lax.dot_general(a, b, (((0,), (0,)), ((), ())), preferred_element_type=F32)


def _mod_kernel(cp_ref, cs_ref, w_ref, b_ref, op_ref, os_ref):
    w = w_ref[...].astype(BF16)
    op_ref[...] = _dot(jax.nn.silu(cp_ref[...]).astype(BF16), w) + b_ref[...]
    os_ref[...] = _dot(jax.nn.silu(cs_ref[...]).astype(BF16), w) + b_ref[...]


def _ada_mod(c_prompt, c_sample, ada_w, ada_b):
    d = c_prompt.shape[1]
    depth = ada_w.shape[0]
    n_mod = ada_w.shape[2] // d
    outs = pl.pallas_call(
        _mod_kernel,
        grid=(depth, n_mod),
        in_specs=[
            pl.BlockSpec(c_prompt.shape, lambda l, j: (0, 0)),
            pl.BlockSpec(c_sample.shape, lambda l, j: (0, 0)),
            pl.BlockSpec((None, d, d), lambda l, j: (l, 0, j)),
            pl.BlockSpec((None, None, 1, d), lambda l, j: (l, j, 0, 0)),
        ],
        out_specs=[pl.BlockSpec((None, None) + c.shape, lambda l, j: (l, j, 0, 0))
                   for c in (c_prompt, c_sample)],
        out_shape=[jax.ShapeDtypeStruct((depth, n_mod) + c.shape, F32)
                   for c in (c_prompt, c_sample)],
        compiler_params=_cparams(("arbitrary", "arbitrary")),
        name="ada_mod",
    )(c_prompt, c_sample, ada_w, ada_b.reshape(depth, n_mod, 1, d))
    return [o.reshape(depth, n_mod, o.shape[2], 1, d) for o in outs]


def _ffn_kernel(*refs, shapes, ff_chunk, pre_attn, final_norm):
    it = iter(refs)
    groups = []
    for _ in shapes:
        x_ref, mod_ref = next(it), next(it)
        a_ref, gate1_ref = (next(it), next(it)) if pre_attn else (None, None)
        groups.append((x_ref, mod_ref, a_ref, gate1_ref))
    g_ref = next(it)
    wo_ref = next(it) if pre_attn else None
    wg_ref, wu_ref, wd_ref = next(it), next(it), next(it)
    fg_ref = next(it) if final_norm else None
    o_refs = [next(it) for _ in shapes]
    act_ref = next(it)
    d_ff = wg_ref.shape[-1]

    def run(k):
        nb, t = shapes[k]
        x_ref, mod_ref, a_ref, gate1_ref = groups[k]
        x = x_ref[...]
        d = x.shape[-1]
        if pre_attn:
            y = _dot(a_ref[...].reshape(nb * t, d), wo_ref[...])
            x = x + gate1_ref[0] * y.reshape(nb, t, d)
        h = _mod_norm(x, g_ref[...], mod_ref[0], mod_ref[1])
        hb = h.reshape(nb * t, d).astype(BF16)
        for j in range(d_ff // ff_chunk):
            sl = slice(j * ff_chunk, (j + 1) * ff_chunk)
            gj = _dot(hb, wg_ref[:, sl])
            uj = _dot(hb, wu_ref[:, sl])
            act_ref[0:nb * t, sl] = (jax.nn.silu(gj) * uj).astype(BF16)
        y = _dot(act_ref[0:nb * t, :], wd_ref[...])
        out = x + 0.5 * mod_ref[2] * y.reshape(nb, t, d)
        if final_norm:
            ms = jnp.mean(out * out, axis=-1, keepdims=True)
            out = out * lax.rsqrt(ms + NORM_EPS) * fg_ref[...]
        o_refs[k][...] = out

    run(0)
    if len(shapes) > 1:
        first_step = jnp.logical_and(pl.program_id(0) == 0, pl.program_id(1) == 0)
        pl.when(first_step)(functools.partial(run, 1))


def _ffn(xs, mods, norm_g, wg, wu, wd, *, layer, sub, t, attns=None, wo=None, wo_idx=0,
         final_g=None):
    nseq, s, d = xs[0].shape
    d_ff = wg.shape[-1]
    half = sub // 2
    pre_attn = attns is not None
    final_norm = final_g is not None
    shapes = [(1, t)] + [x.shape[:2] for x in xs[1:]]
    tile = lambda b, i: (b, i, 0)
    whole = lambda b, i: (0, 0, 0)
    in_specs, args, out_specs = [], [], []
    for k, (x, mod) in enumerate(zip(xs, mods)):
        nb, rows = shapes[k]
        seq0 = (lambda b: b) if k == 0 else (lambda b: 0)
        xspec = pl.BlockSpec((nb, rows, d), tile if k == 0 else whole)
        in_specs += [xspec, pl.BlockSpec((None, 3, nb, 1, d),
                                         lambda b, i, seq0=seq0: (layer, sub, seq0(b), 0, 0))]
        args += [x, mod]
        if pre_attn:
            gate1 = 3 * (sub - 1) + 2
            in_specs += [xspec, pl.BlockSpec((None, 1, nb, 1, d),
                                             lambda b, i, seq0=seq0: (layer, gate1, seq0(b), 0, 0))]
            args += [attns[k], mod]
        out_specs.append(xspec)
    in_specs.append(pl.BlockSpec((None, None, 1, d), lambda b, i: (layer, sub, 0, 0)))
    args.append(norm_g)
    if pre_attn:
        in_specs.append(_resident((None, d, d), lambda b, i: (wo_idx, 0, 0)))
        args.append(wo)
    in_specs += [
        _resident((None, None, d, d_ff), lambda b, i: (layer, half, 0, 0)),
        _resident((None, None, d, d_ff), lambda b, i: (layer, half, 0, 0)),
        _resident((None, None, d_ff, d), lambda b, i: (layer, half, 0, 0)),
    ]
    args += [wg, wu, wd]
    if final_norm:
        in_specs.append(pl.BlockSpec((1, d), lambda b, i: (0, 0)))
        args.append(final_g.reshape(1, d))
    ff_chunk = 256 if d_ff % 256 == 0 else LANES
    return pl.pallas_call(
        functools.partial(_ffn_kernel, shapes=shapes, ff_chunk=ff_chunk, pre_attn=pre_attn,
                          final_norm=final_norm),
        grid=(nseq, s // t),
        in_specs=in_specs,
        out_specs=out_specs,
        out_shape=[jax.ShapeDtypeStruct(x.shape, F32) for x in xs],
        scratch_shapes=[pltpu.VMEM((max(nb * rows for nb, rows in shapes), d_ff), BF16)],
        compiler_params=_cparams(("arbitrary", "arbitrary")),
        name="ffn",
    )(*args)


def _conv_kernel(x_ref, mod_ref, g_ref, hist_ref, w1_ref, b1_ref, wdw_ref, bdw_ref, lng_ref,
                 lnb_ref, w2_ref, b2_ref, o_ref, st_ref, full_ref, y_ref, shift_ref, *, nb, t,
                 n_tiles, n_parts, row_blk, lane_blk):
    c = w2_ref.shape[0]
    d = x_ref.shape[-1]
    first = HIST_PAD - (CONV_WIDTH - 1)
    shift_rows = shift_ref.shape[2]

    @pl.when(pl.program_id(1) == 0)
    def _():
        full_ref[:, 0:HIST_PAD, :] = hist_ref[...]

    part = t // n_parts

    def glu(p):
        xp = x_ref[:, p * part:(p + 1) * part, :]
        h = _mod_norm(xp, g_ref[...], mod_ref[0], mod_ref[1])
        a = _dot(h.reshape(nb * part, d).astype(BF16), w1_ref[...]) + b1_ref[...]
        u = a[:, :c] * jax.nn.sigmoid(a[:, c:])
        full_ref[:, HIST_PAD + p * part:HIST_PAD + (p + 1) * part, :] = u.reshape(nb, part, c)

    def depthwise(p):
        row0 = p * part
        for c0 in range(0, c, lane_blk):
            lanes = slice(c0, c0 + lane_blk)
            for s in range(1, SUBLANES):
                shift_ref[s - 1] = full_ref[:, row0 + s:row0 + s + shift_rows, lanes]
            for r0 in range(0, part, row_blk):
                acc = jnp.zeros((nb * row_blk // SUBLANES, SUBLANES, lane_blk), F32)
                for w in range(CONV_WIDTH):
                    base, s = divmod(first + w, SUBLANES)
                    lo = base * SUBLANES + r0
                    src = (full_ref[:, row0 + lo:row0 + lo + row_blk, lanes] if s == 0
                           else shift_ref[s - 1, :, lo:lo + row_blk, :])
                    acc = acc + src.reshape(acc.shape) * wdw_ref[w, :, lanes][None]
                y_ref[:, row0 + r0:row0 + r0 + row_blk, lanes] = (
                    acc.reshape(nb, row_blk, lane_blk) + bdw_ref[:, lanes])

    def project(p):
        rows = slice(p * part, (p + 1) * part)
        y = y_ref[:, rows, :].reshape(nb * part, c)
        mu = jnp.mean(y, axis=-1, keepdims=True)
        yc = y - mu
        var = jnp.mean(yc * yc, axis=-1, keepdims=True)
        yn = yc * lax.rsqrt(var + NORM_EPS) * lng_ref[...] + lnb_ref[...]
        out = _dot(jax.nn.silu(yn).astype(BF16), w2_ref[...]) + b2_ref[...]
        o_ref[:, rows, :] = x_ref[:, rows, :] + mod_ref[2] * out.reshape(nb, part, d)

    glu(0)
    for p in range(n_parts):
        if p + 1 < n_parts:
            glu(p + 1)
        depthwise(p)
        project(p)

    st_ref[...] = full_ref[:, t + first:t + HIST_PAD, :]
    if n_tiles > 1:
        full_ref[:, 0:HIST_PAD, :] = full_ref[:, t:t + HIST_PAD, :]


def _conv_mixer(x, mod, norm_g, hist, w1, b1, wdw, bdw, lng, lnb, w2, b2, *, layer, jm, nb, t):
    nseq, s, d = x.shape
    c = w2.shape[1]
    n_tiles = s // t
    acc_vregs = 16
    row_blk = min(t, 64)
    n_parts = 2 if t % (2 * row_blk) == 0 else 1
    part = t // n_parts
    lane_blk = max(LANES, min(c, acc_vregs * SUBLANES * LANES // (nb * row_blk) // LANES * LANES))
    xspec = pl.BlockSpec((nb, t, d), lambda b, i: (b, i, 0))
    const = lambda b, i: (jm, 0, 0)
    vec = lambda a: a.reshape(a.shape[0], 1, a.shape[1])
    hist32 = jnp.pad(hist, ((0, 0), (HIST_PAD - (CONV_WIDTH - 1), 0), (0, 0)))
    return pl.pallas_call(
        functools.partial(_conv_kernel, nb=nb, t=t, n_tiles=n_tiles, n_parts=n_parts,
                          row_blk=row_blk, lane_blk=lane_blk),
        grid=(nseq // nb, n_tiles),
        in_specs=[
            xspec,
            pl.BlockSpec((None, 3, nb, 1, d), lambda b, i: (layer, 1, b, 0, 0)),
            pl.BlockSpec((None, None, 1, d), lambda b, i: (layer, 1, 0, 0)),
            pl.BlockSpec((nb, HIST_PAD, c), lambda b, i: (b, 0, 0)),
            _resident((None, d, 2 * c), const),
            pl.BlockSpec((None, 1, 2 * c), const),
            pl.BlockSpec((None, CONV_WIDTH, SUBLANES, c), lambda b, i: (jm, 0, 0, 0)),
            pl.BlockSpec((None, 1, c), const),
            pl.BlockSpec((None, 1, c), const),
            pl.BlockSpec((None, 1, c), const),
            _resident((None, c, d), const),
            pl.BlockSpec((None, 1, d), const),
        ],
        out_specs=[xspec, pl.BlockSpec((nb, CONV_WIDTH - 1, c), lambda b, i: (b, 0, 0))],
        out_shape=[jax.ShapeDtypeStruct(x.shape, F32),
                   jax.ShapeDtypeStruct((nseq, CONV_WIDTH - 1, c), F32)],
        scratch_shapes=[pltpu.VMEM((nb, HIST_PAD + t, c), F32), pltpu.VMEM((nb, t, c), F32),
                        pltpu.VMEM((SUBLANES - 1, nb, HIST_PAD - SUBLANES + part, lane_blk), F32)],
        compiler_params=_cparams(("arbitrary", "arbitrary")),
        name="conv_mixer",
    )(x, mod, norm_g, hist32, w1, vec(b1),
      jnp.broadcast_to(wdw[:, :, None, :], wdw.shape[:2] + (SUBLANES, c)),
      vec(bdw), vec(lng), vec(lnb), w2, vec(b2))


def _qkv_kernel(x_ref, mod_ref, g_ref, w_ref, q_ref, k_ref, v_ref, *, nb, t, q_scale):
    x = x_ref[...]
    d = x.shape[-1]
    h = _mod_norm(x, g_ref[...], mod_ref[0], mod_ref[1])
    hb = h.reshape(nb * t, d).astype(BF16)
    q = _dot(hb, w_ref[:, 0:d]) * q_scale
    q_ref[...] = q.reshape(nb, t, d).astype(q_ref.dtype)
    k_ref[...] = _dot(hb, w_ref[:, d:2 * d]).reshape(nb, t, d)
    v_ref[...] = _dot(hb, w_ref[:, 2 * d:3 * d]).reshape(nb, t, d)


def _qkv(x, mod, norm_g, w, *, layer, jm, nb, t, q_dtype, q_scale):
    nseq, s, d = x.shape
    xspec = pl.BlockSpec((nb, t, d), lambda b, i: (b, i, 0))
    return pl.pallas_call(
        functools.partial(_qkv_kernel, nb=nb, t=t, q_scale=q_scale),
        grid=(nseq // nb, s // t),
        in_specs=[
            xspec,
            pl.BlockSpec((None, 3, nb, 1, d), lambda b, i: (layer, 1, b, 0, 0)),
            pl.BlockSpec((None, None, 1, d), lambda b, i: (layer, 1, 0, 0)),
            _resident((None, d, 3 * d), lambda b, i: (jm, 0, 0)),
        ],
        out_specs=[xspec, xspec, xspec],
        out_shape=[jax.ShapeDtypeStruct(x.shape, q_dtype), jax.ShapeDtypeStruct(x.shape, F32),
                   jax.ShapeDtypeStruct(x.shape, F32)],
        compiler_params=_cparams(("arbitrary", "arbitrary")),
        name="qkv",
    )(x, mod, norm_g, w)


def _bucket_of_distance(n_dist):
    dist = np.arange(n_dist)
    max_exact = NUM_BUCKETS // 2
    nf = np.maximum(dist, max_exact).astype(np.float32)
    large = max_exact + (np.log(nf / max_exact) / math.log(MAX_DISTANCE / max_exact)
                         * (NUM_BUCKETS - max_exact)).astype(np.int32)
    large = np.minimum(large, NUM_BUCKETS - 1)
    return np.where(dist < max_exact, dist, large).astype(np.int32)


_BUCKETS = _bucket_of_distance(4 * MAX_DISTANCE)
assert _BUCKETS[-1] == NUM_BUCKETS - 1 and np.all(np.diff(_BUCKETS) >= 0)
_BUCKET_START = [int(np.argmax(_BUCKETS >= b)) for b in range(NUM_BUCKETS)]
assert _BUCKET_START[NUM_BUCKETS - 1] <= MOBA_BLOCK


def _bias_of_distance(dist, table):
    bias = jnp.full(dist.shape, table[0], F32)
    for b in range(1, NUM_BUCKETS):
        bias = jnp.where(dist >= _BUCKET_START[b], table[b], bias)
    return bias


def _rank_select(gate, n_valid, axis, first=0):
    idx = lax.broadcasted_iota(jnp.int32, gate.shape, axis)
    cnt = jnp.zeros(gate.shape, F32)
    for j in range(first, first + n_valid):
        gj = lax.slice_in_dim(gate, j, j + 1, axis=axis)
        tie = jnp.where(idx > j, 1.0, 0.0)
        cnt = cnt + jnp.where(gj > gate, 1.0, jnp.where(gj == gate, tie, 0.0))
    return cnt < float(MOBA_TOP_K)


def _moba_prompt_kernel(rb_ref, q_ref, k_ref, v_ref, o_ref, bias_ref, *, n_blk, n_heads):
    pair = pl.program_id(0)
    blk = MOBA_BLOCK

    @pl.when(pl.program_id(1) == 0)
    def _():
        dist = (lax.broadcasted_iota(jnp.int32, (blk, blk), 0)
                - lax.broadcasted_iota(jnp.int32, (blk, blk), 1))
        for hh in range(2):
            table = [rb_ref[b * n_heads + 2 * pair + hh] for b in range(NUM_BUCKETS)]
            far = table[NUM_BUCKETS - 1]
            own = (_bias_of_distance(dist, table) - far) * LOG2E
            bias_ref[hh, 0] = jnp.where(dist >= 0, own, -jnp.inf)
            bias_ref[hh, 1] = (_bias_of_distance(dist + blk, table) - far) * LOG2E

    kf = k_ref[...]
    vf = v_ref[...]
    s_len = kf.shape[0]
    lane = lax.broadcasted_iota(jnp.int32, (1, LANES), 1)
    key_lane = lax.broadcasted_iota(jnp.int32, (s_len, LANES), 1)
    key_blk = lax.broadcasted_iota(jnp.int32, (s_len, LANES), 0) // blk
    km_row = lax.broadcasted_iota(jnp.int32, (LANES, LANES), 0)

    heads = []
    for hh in range(2):
        off = (1 - hh) * HEAD_DIM
        hmask = (lane // HEAD_DIM) == hh
        k_aug = jnp.where(hmask, kf, jnp.where(key_lane == key_blk + off, 1.0, 0.0)).astype(BF16)
        v_aug = jnp.where(hmask, vf, 1.0).astype(BF16)
        km = jnp.zeros((LANES, LANES), F32)
        for j in range(n_blk):
            km = jnp.where(km_row == off + j,
                           jnp.mean(kf[j * blk:(j + 1) * blk, :], axis=0, keepdims=True), km)
        heads.append((hh, off, hmask, k_aug, v_aug, km.astype(BF16)))

    def logits(i, head):
        hh, off, hmask, k_aug, _, km = head
        qi = q_ref[i * blk:(i + 1) * blk, :]
        if i > MOBA_TOP_K:
            gate = _dot_nt(jnp.where(hmask, qi, jnp.zeros_like(qi)), km)
            keep = _rank_select(gate, i, axis=1, first=off)
            past = jnp.logical_and(lane >= off, lane < off + i)
            drop = jnp.where(past, jnp.where(keep, 0.0, MASKED), 0.0)
            q_aug = jnp.where(hmask, qi, drop.astype(BF16))
        else:
            q_aug = jnp.where(hmask, qi, jnp.zeros_like(qi))
        for j in range(i + 1):
            s = _dot_nt(q_aug, k_aug[j * blk:(j + 1) * blk, :])
            if j == i:
                s = s + bias_ref[hh, 0]
            elif j == i - 1:
                s = s + bias_ref[hh, 1]
            yield s

    def weighted_values(s_list, head):
        v_aug = head[4]
        m = jnp.max(functools.reduce(jnp.maximum, s_list), axis=-1, keepdims=True)
        acc = jnp.zeros((blk, LANES), F32)
        for j, s in enumerate(s_list):
            acc = acc + _dot(jnp.exp2(s - m).astype(BF16), v_aug[j * blk:(j + 1) * blk, :])
        return acc

    units = [(i, head) for i in range(n_blk) for head in heads]
    low = lane < HEAD_DIM
    ahead = 2
    pending = [list(logits(*unit)) for unit in units[:ahead]]
    accs = []
    for u, (i, head) in enumerate(units):
        if u + ahead < len(units):
            pending.append(list(logits(*units[u + ahead])))
        accs.append(weighted_values(pending.pop(0), head))
        if len(accs) == 2:
            num = jnp.where(low, accs[0], accs[1])
            den = pltpu.roll(jnp.where(low, accs[1], accs[0]), HEAD_DIM, axis=1)
            o_ref[i * blk:(i + 1) * blk, :] = (num * (1.0 / den)).astype(o_ref.dtype)
            accs = []


def _moba_prompt(q, k, v, rel_bias):
    b, s, d = q.shape
    n_heads = d // HEAD_DIM
    blk = MOBA_BLOCK
    n_blk = s // blk
    assert s % blk == 0 and n_heads % 2 == 0 and 2 * HEAD_DIM == LANES
    hspec = pl.BlockSpec((None, s, LANES), lambda p, bi, rb_ref: (bi, 0, p))
    return pl.pallas_call(
        functools.partial(_moba_prompt_kernel, n_blk=n_blk, n_heads=n_heads),
        grid_spec=pltpu.PrefetchScalarGridSpec(
            num_scalar_prefetch=1,
            grid=(n_heads // 2, b),
            in_specs=[hspec, hspec, hspec],
            out_specs=hspec,
            scratch_shapes=[pltpu.VMEM((2, 2, blk, blk), F32)],
        ),
        out_shape=jax.ShapeDtypeStruct((b, s, d), BF16),
        compiler_params=_cparams(("arbitrary", "arbitrary")),
        name="moba_prompt",
    )(rel_bias.reshape(-1), q, k, v)


def _moba_sample_kernel(pt_ref, rb_ref, *refs, n_groups, n_blk, n_heads, t):
    pp = PAGES_PER_STEP
    (ck_hbm, cv_hbm, q_ref, kn_ref, vn_ref, o_ref, ring_ref, sem, a_ref, s_ref, p_ref, pown_ref,
     l_ref, acc_ref, blast_ref, bown_ref, gate_ref, bmax_ref) = refs
    seq = pl.program_id(0)
    step = pl.program_id(1)
    blk = MOBA_BLOCK
    ppb = blk // PAGE_SIZE
    bps = pp // ppb
    d = q_ref.shape[-1]
    cols = n_heads * t
    row_head = lax.broadcasted_iota(jnp.int32, (cols, d), 0) // t
    lane_head = lax.broadcasted_iota(jnp.int32, (cols, d), 1) // HEAD_DIM
    diag = row_head == lane_head
    lane = lax.broadcasted_iota(jnp.int32, (cols, LANES), 1)

    steps = 2 * n_groups
    n_pages = n_groups * pp
    group = seq * steps + step
    n_stream = pl.num_programs(0) * steps

    def group_copies(m, src_hbm, first_page):
        slot = m % RING_GROUPS
        return [pltpu.make_async_copy(
            src_hbm.at[pt_ref[(m // steps) * n_pages + first_page + r]],
            ring_ref.at[slot, r], sem.at[slot]) for r in range(pp)]

    def start_group(m):
        st = m % steps

        @pl.when(st < n_groups)
        def _():
            for copy in group_copies(m, ck_hbm, st * pp):
                copy.start()

        @pl.when(st >= n_groups)
        def _():
            for copy in group_copies(m, cv_hbm, (st - n_groups) * pp):
                copy.start()

    @pl.when(group == 0)
    def _():
        for m in range(RING_GROUPS - 1):
            start_group(jnp.int32(m))

    @pl.when(group + RING_GROUPS - 1 < n_stream)
    def _():
        start_group(group + RING_GROUPS - 1)

    for copy in group_copies(group, ck_hbm, 0):
        copy.wait()
    pages_ref = ring_ref.at[group % RING_GROUPS]

    def new_rows(ref):
        return jnp.concatenate([ref[...], jnp.zeros((PAGE_SIZE - t, ref.shape[-1]), F32)],
                               axis=0).astype(BF16)

    def block_of(r):
        return jnp.concatenate([pages_ref[ppb * r + i] for i in range(ppb)], axis=1).astype(BF16)

    @pl.when(jnp.logical_and(seq == 0, step == 0))
    def _():
        qi = lax.broadcasted_iota(jnp.int32, (t, blk), 0)
        ki = lax.broadcasted_iota(jnp.int32, (t, blk), 1)
        qo = lax.broadcasted_iota(jnp.int32, (t, PAGE_SIZE), 0)
        ko = lax.broadcasted_iota(jnp.int32, (t, PAGE_SIZE), 1)
        for h in range(n_heads):
            table = [rb_ref[b * n_heads + h] for b in range(NUM_BUCKETS)]
            far = table[NUM_BUCKETS - 1]
            rows = slice(h * t, (h + 1) * t)
            blast_ref[rows, :] = _bias_of_distance(blk + qi - ki, table) - far
            bown_ref[rows, :] = jnp.where(ko <= qo, _bias_of_distance(qo - ko, table) - far,
                                          -jnp.inf)

    @pl.when(step == 0)
    def _():
        qt = jnp.tile(q_ref[...], (n_heads, 1))
        a_ref[...] = jnp.where(diag, qt, 0.0).astype(BF16)
        acc_ref[...] = jnp.zeros_like(acc_ref)
        gate_ref[...] = jnp.full_like(gate_ref, -jnp.inf)
        bmax_ref[...] = jnp.full_like(bmax_ref, -jnp.inf)

    @pl.when(step < n_groups)
    def _():
        a = a_ref[...]
        gate = gate_ref[...]
        bmax = bmax_ref[...]
        for r in range(bps):
            n = step * bps + r
            s = _dot(a, block_of(r))
            s_ref[n] = s
            gate = jnp.where(lane == n, jnp.sum(s, axis=1, keepdims=True), gate)
            bmax = jnp.where(lane == n, jnp.max(s, axis=1, keepdims=True), bmax)
        gate_ref[...] = gate
        bmax_ref[...] = bmax

    @pl.when(step == n_groups)
    def _():
        lane_f = lane.astype(F32)
        g = gate_ref[...]
        picks = []
        for _ in range(MOBA_TOP_K):
            top = jnp.max(g, axis=1, keepdims=True)
            pick = jnp.min(jnp.where(g == top, lane_f, float(LANES)), axis=1, keepdims=True)
            picks.append(pick)
            g = jnp.where(lane_f == pick, -jnp.inf, g)

        def picked(n):
            hit = jnp.where(picks[0] == n, 1.0, 0.0)
            for pick in picks[1:]:
                hit = jnp.where(pick == n, 1.0, hit)
            return hit > 0.0

        last = n_blk - 1
        s_own = _dot_nt(a_ref[...], new_rows(kn_ref)) + bown_ref[...]
        bmax = jnp.where(lane == last,
                         jnp.max(s_ref[last] + blast_ref[...], axis=1, keepdims=True), bmax_ref[...])
        m = jnp.maximum(jnp.max(s_own, axis=1, keepdims=True),
                        jnp.max(jnp.where(picked(lane_f), bmax, -jnp.inf), axis=1, keepdims=True))
        p_own = jnp.exp(s_own - m)
        lsum = jnp.zeros((cols, blk), F32)
        for n in range(n_blk):
            sn = s_ref[n] + blast_ref[...] if n == last else s_ref[n]
            p = jnp.where(picked(float(n)), jnp.exp(sn - m), 0.0)
            lsum = lsum + p
            p_ref[n] = p.astype(BF16)
        l = jnp.sum(lsum, axis=1, keepdims=True) + jnp.sum(p_own, axis=1, keepdims=True)
        pown_ref[...] = p_own.astype(BF16)
        l_ref[...] = jnp.broadcast_to(l, (cols, LANES))

    @pl.when(step >= n_groups)
    def _():
        acc = acc_ref[...]
        for r in range(bps):
            acc = acc + _dot_nt(p_ref[(step - n_groups) * bps + r], block_of(r))
        acc_ref[...] = acc

    @pl.when(step == 2 * n_groups - 1)
    def _():
        acc = acc_ref[...] + _dot(pown_ref[...], new_rows(vn_ref))
        acc = acc * jnp.tile(1.0 / l_ref[...], (1, d // LANES))
        acc = jnp.where(diag, acc, 0.0)
        o_ref[...] = jnp.sum(acc.reshape(n_heads, t, d), axis=0).astype(o_ref.dtype)


def _moba_sample(q, k_new, v_new, cache_k, cache_v, page_table, rel_bias):
    nseq, t, d = q.shape
    n_heads = d // HEAD_DIM
    n_pages = page_table.shape[1]
    past = n_pages * PAGE_SIZE
    blk = MOBA_BLOCK
    n_blk = past // blk
    pp = PAGES_PER_STEP
    cols = n_heads * t
    assert past % blk == 0 and blk % PAGE_SIZE == 0 and pp % (blk // PAGE_SIZE) == 0
    assert n_pages % pp == 0 and cols % SUBLANES == 0 and t % SUBLANES == 0 and t <= PAGE_SIZE
    assert MOBA_TOP_K <= n_blk <= LANES
    g = n_pages // pp
    assert nseq * 2 * g >= RING_GROUPS

    seq = pl.BlockSpec((None, t, d), lambda b, s, pt, rb: (b, 0, 0))
    hbm = pl.BlockSpec(memory_space=pl.ANY)
    in_specs = [hbm, hbm, seq, seq, seq]
    return pl.pallas_call(
        functools.partial(_moba_sample_kernel, n_groups=g, n_blk=n_blk, n_heads=n_heads, t=t),
        grid_spec=pltpu.PrefetchScalarGridSpec(
            num_scalar_prefetch=2,
            grid=(nseq, 2 * g),
            in_specs=in_specs,
            out_specs=seq,
            scratch_shapes=[
                pltpu.VMEM((RING_GROUPS, pp, d, PAGE_SIZE), F32),
                pltpu.SemaphoreType.DMA((RING_GROUPS,)),
                pltpu.VMEM((cols, d), BF16),
                pltpu.VMEM((n_blk, cols, blk), F32),
                pltpu.VMEM((n_blk, cols, blk), BF16),
                pltpu.VMEM((cols, PAGE_SIZE), BF16),
                pltpu.VMEM((cols, LANES), F32),
                pltpu.VMEM((cols, d), F32),
                pltpu.VMEM((cols, blk), F32),
                pltpu.VMEM((cols, PAGE_SIZE), F32),
                pltpu.VMEM((cols, LANES), F32),
                pltpu.VMEM((cols, LANES), F32),
            ],
        ),
        out_shape=jax.ShapeDtypeStruct((nseq, t, d), BF16),
        compiler_params=_cparams(("arbitrary", "arbitrary")),
        name="moba_sample",
    )(page_table.reshape(-1), rel_bias.reshape(-1), cache_k, cache_v, q, k_new, v_new)


def _tile_rows(s):
    for t in (512, 256, 128, 64, 32, 16, 8):
        if s % t == 0:
            return t
    raise ValueError(f"sequence length {s} is not a multiple of {SUBLANES}")


def _trunk(xs, mods, hists_in, past, page_table, w, *, t):
    d = xs[0].shape[-1]
    n_heads = d // HEAD_DIM
    depth = w["norm_g"].shape[0]
    tiles = [(1, t), (SUBLANES, xs[1].shape[1])]
    qkv_tiles = [(1, t), xs[1].shape[:2]]
    hists, new_k, new_v = [[], []], [[], []], [[], []]
    for layer in range(depth):
        jm = layer // 2
        ffn = functools.partial(_ffn, mods=mods, norm_g=w["norm_g"], wg=w["wg"], wu=w["wu"],
                                wd=w["wd"], layer=layer, t=t)
        xs = list(ffn(xs, sub=0))
        attns = None
        if layer % 2 == 0:
            for k in range(2):
                xs[k], st = _conv_mixer(xs[k], mods[k], w["norm_g"], hists_in[k][jm], w["pw1"],
                                        w["b_pw1"], w["dw"], w["b_dw"], w["ln_g"], w["ln_b"],
                                        w["pw2"], w["b_pw2"], layer=layer, jm=jm,
                                        nb=tiles[k][0], t=tiles[k][1])
                hists[k].append(st)
        else:
            scale = 1.0 / math.sqrt(HEAD_DIM)
            attns = []
            for k in range(2):
                nseq, s, _ = xs[k].shape
                q, kk, vv = _qkv(xs[k], mods[k], w["norm_g"], w["qkv"], layer=layer, jm=jm,
                                 nb=qkv_tiles[k][0], t=qkv_tiles[k][1],
                                 q_dtype=BF16 if k == 0 else F32,
                                 q_scale=scale * LOG2E if k == 0 else scale)
                if k == 0:
                    attns.append(_moba_prompt(q, kk, vv, w["rel_bias"]))
                else:
                    ck, cv, n_phys = past
                    attns.append(_moba_sample(q, kk, vv, ck, cv, page_table + jm * n_phys,
                                              w["rel_bias"]))
                new_k[k].append(kk.reshape(nseq, s, n_heads, HEAD_DIM))
                new_v[k].append(vv.reshape(nseq, s, n_heads, HEAD_DIM))
        last = layer == depth - 1
        xs = ffn(xs, sub=2, attns=attns, wo=w["wo"] if attns is not None else None, wo_idx=jm,
                 final_g=w["final_g"] if last else None)
    return xs, [jnp.stack(h) for h in hists], [jnp.stack(a) for a in new_k], \
        [jnp.stack(a) for a in new_v]


def kernel(x_prompt, x_sample, state_conv, cache_k, cache_v, page_table, c_prompt, c_sample, norm_g, ada_w, ada_b, ffn_w_gate, ffn_w_up, ffn_w_down, conv_w_pw1, conv_b_pw1, conv_w_dw, conv_b_dw, conv_ln_g, conv_ln_b, conv_w_pw2, conv_b_pw2, attn_w_qkv, attn_w_o, rel_bias, final_norm_g):
    b, s, d = x_prompt.shape
    depth = norm_g.shape[0]
    w = dict(
        norm_g=norm_g.reshape(depth, 3, 1, d),
        wg=ffn_w_gate.astype(BF16), wu=ffn_w_up.astype(BF16), wd=ffn_w_down.astype(BF16),
        pw1=conv_w_pw1.astype(BF16), b_pw1=conv_b_pw1, dw=conv_w_dw, b_dw=conv_b_dw,
        ln_g=conv_ln_g, ln_b=conv_ln_b, pw2=conv_w_pw2.astype(BF16), b_pw2=conv_b_pw2,
        qkv=attn_w_qkv.astype(BF16), wo=attn_w_o.astype(BF16), rel_bias=rel_bias,
        final_g=final_norm_g,
    )
    mod_p, mod_s = _ada_mod(c_prompt, c_sample, ada_w, ada_b)

    n_layers, n_phys = cache_k.shape[:2]
    pages = lambda c: jnp.transpose(c, (0, 1, 3, 4, 2)).reshape(n_layers * n_phys, d, PAGE_SIZE)

    hist0 = jnp.zeros((conv_w_dw.shape[0], b, CONV_WIDTH - 1, conv_w_dw.shape[2]), F32)
    ys, states, ks, vs = _trunk([x_prompt, x_sample], [mod_p, mod_s], [hist0, state_conv],
                                (pages(cache_k), pages(cache_v), n_phys), page_table, w,
                                t=_tile_rows(s))
    return (ys[0], ys[1], states[0], states[1], ks[0], vs[0], ks[1], vs[1])
```

```python
import functools
import math

import numpy as np
import jax
import jax.numpy as jnp
from jax import lax
from jax.experimental import pallas as pl
from jax.experimental.pallas import tpu as pltpu

F32 = jnp.float32
BF16 = jnp.bfloat16

HEAD_DIM = 64
CONV_WIDTH = 31
MOBA_BLOCK = 256
MOBA_TOP_K = 3
PAGE_SIZE = 128
NUM_BUCKETS = 32
MAX_DISTANCE = 128
NORM_EPS = 1e-6

LANES = 128
SUBLANES = 8
HIST_PAD = 32
VMEM_LIMIT = 56 * 1024 * 1024
PAGES_PER_STEP = 8
RING_GROUPS = 4
LOG2E = math.log2(math.e)
MASKED = -1e30


def _cparams(sem):
    return pltpu.CompilerParams(dimension_semantics=sem, vmem_limit_bytes=VMEM_LIMIT)


def _resident(block_shape, index_map):
    return pl.BlockSpec(block_shape, index_map, pipeline_mode=pl.Buffered(1))


def _mod_norm(x, g, shift, scale):
    ms = jnp.mean(x * x, axis=-1, keepdims=True)
    n = x * lax.rsqrt(ms + NORM_EPS)
    return n * g * (1.0 + scale) + shift


def _dot(a, b):
    return jnp.dot(a, b, preferred_element_type=F32)


def _dot_nt(a, b):
    return lax.dot_general(a, b, (((1,), (1,)), ((), ())), preferred_element_type=F32)


def _dot_tn(a, b):
    return lax.dot_general(a, b, (((0,), (0,)), ((), ())), preferred_element_type=F32)


def _mod_kernel(cp_ref, cs_ref, w_ref, b_ref, op_ref, os_ref):
    d = cp_ref.shape[1]
    scp = jax.nn.silu(cp_ref[...]).astype(BF16)
    scs = jax.nn.silu(cs_ref[...]).astype(BF16)
    for m in range(op_ref.shape[0]):
        w = w_ref[:, m * d:(m + 1) * d].astype(BF16)
        op_ref[m] = _dot(scp, w) + b_ref[m]
        os_ref[m] = _dot(scs, w) + b_ref[m]


def _ada_mod(c_prompt, c_sample, ada_w, ada_b):
    d = c_prompt.shape[1]
    depth = ada_w.shape[0]
    n_mod = ada_w.shape[2] // d
    per_step = 3
    assert n_mod % per_step == 0
    outs = pl.pallas_call(
        _mod_kernel,
        grid=(depth, n_mod // per_step),
        in_specs=[
            pl.BlockSpec(c_prompt.shape, lambda l, j: (0, 0)),
            pl.BlockSpec(c_sample.shape, lambda l, j: (0, 0)),
            pl.BlockSpec((None, d, per_step * d), lambda l, j: (l, 0, j)),
            pl.BlockSpec((None, per_step, 1, d), lambda l, j: (l, j, 0, 0)),
        ],
        out_specs=[pl.BlockSpec((None, per_step) + c.shape, lambda l, j: (l, j, 0, 0))
                   for c in (c_prompt, c_sample)],
        out_shape=[jax.ShapeDtypeStruct((depth, n_mod) + c.shape, F32)
                   for c in (c_prompt, c_sample)],
        compiler_params=_cparams(("arbitrary", "arbitrary")),
        name="ada_mod",
    )(c_prompt, c_sample, ada_w, ada_b.reshape(depth, n_mod, 1, d))
    return [o.reshape(depth, n_mod, o.shape[2], 1, d) for o in outs]


def _ffn_kernel(*refs, shapes, ff_chunk, pre_attn, final_norm, layer, half):
    it = iter(refs)
    groups = []
    for _ in shapes:
        x_ref, mod_ref = next(it), next(it)
        a_ref, gate1_ref = (next(it), next(it)) if pre_attn else (None, None)
        groups.append((x_ref, mod_ref, a_ref, gate1_ref))
    g_ref = next(it)
    wo_ref = next(it) if pre_attn else None
    wg_hbm, wu_hbm, wd_hbm = next(it), next(it), next(it)
    fg_ref = next(it) if final_norm else None
    o_refs = [next(it) for _ in shapes]
    act_ref, wg_ref, wu_ref, wd_ref, gu_stage, wd_stage, gu_sem, wd_sem = (next(it) for _ in range(8))
    d_ff = wg_ref.shape[-1]
    n_chunks = d_ff // ff_chunk
    gu_slots, wd_slots = gu_stage.shape[0], wd_stage.shape[0]

    def gu_copy(c):
        cols = slice((c // 2) * ff_chunk, (c // 2 + 1) * ff_chunk)
        src = (wg_hbm if c % 2 == 0 else wu_hbm).at[layer, half, :, cols]
        return pltpu.make_async_copy(src, gu_stage.at[c % gu_slots], gu_sem.at[c % gu_slots])

    def wd_copy(j):
        src = wd_hbm.at[layer, half, j * ff_chunk:(j + 1) * ff_chunk, :]
        return pltpu.make_async_copy(src, wd_stage.at[j % wd_slots], wd_sem.at[j % wd_slots])

    def stage_chunk(j):
        cols = slice(j * ff_chunk, (j + 1) * ff_chunk)
        for c, dst in ((2 * j, wg_ref), (2 * j + 1, wu_ref)):
            gu_copy(c).wait()
            dst[:, cols] = gu_stage[c % gu_slots].astype(BF16)
            if c + gu_slots < 2 * n_chunks:
                gu_copy(c + gu_slots).start()
        wd_copy(j).wait()
        wd_ref[cols, :] = wd_stage[j % wd_slots].astype(BF16)
        if j + wd_slots < n_chunks:
            wd_copy(j + wd_slots).start()

    def run(k, stage):
        nb, t = shapes[k]
        x_ref, mod_ref, a_ref, gate1_ref = groups[k]
        if stage:
            for c in range(gu_slots):
                gu_copy(c).start()
            for j in range(wd_slots):
                wd_copy(j).start()
        x = x_ref[...]
        d = x.shape[-1]
        if pre_attn:
            y = _dot(a_ref[...].reshape(nb * t, d), wo_ref[...])
            x = x + gate1_ref[0] * y.reshape(nb, t, d)
        h = _mod_norm(x, g_ref[...], mod_ref[0], mod_ref[1])
        hb = h.reshape(nb * t, d).astype(BF16)
        for j in range(n_chunks):
            if stage:
                stage_chunk(j)
            sl = slice(j * ff_chunk, (j + 1) * ff_chunk)
            gj = _dot(hb, wg_ref[:, sl])
            uj = _dot(hb, wu_ref[:, sl])
            act_ref[0:nb * t, sl] = (jax.nn.silu(gj) * uj).astype(BF16)
        y = _dot(act_ref[0:nb * t, :], wd_ref[...])
        out = x + 0.5 * mod_ref[2] * y.reshape(nb, t, d)
        if final_norm:
            ms = jnp.mean(out * out, axis=-1, keepdims=True)
            out = out * lax.rsqrt(ms + NORM_EPS) * fg_ref[...]
        o_refs[k][...] = out

    first_step = jnp.logical_and(pl.program_id(0) == 0, pl.program_id(1) == 0)

    @pl.when(first_step)
    def _():
        run(0, stage=True)
        for k in range(1, len(shapes)):
            run(k, stage=False)

    @pl.when(jnp.logical_not(first_step))
    def _():
        run(0, stage=False)


def _ffn(xs, mods, norm_g, wg, wu, wd, *, layer, sub, t, attns=None, wo=None, wo_idx=0,
         final_g=None):
    nseq, s, d = xs[0].shape
    d_ff = wg.shape[-1]
    half = sub // 2
    pre_attn = attns is not None
    final_norm = final_g is not None
    shapes = [(1, t)] + [x.shape[:2] for x in xs[1:]]
    tile = lambda b, i: (b, i, 0)
    whole = lambda b, i: (0, 0, 0)
    in_specs, args, out_specs = [], [], []
    for k, (x, mod) in enumerate(zip(xs, mods)):
        nb, rows = shapes[k]
        seq0 = (lambda b: b) if k == 0 else (lambda b: 0)
        xspec = pl.BlockSpec((nb, rows, d), tile if k == 0 else whole)
        in_specs += [xspec, pl.BlockSpec((None, 3, nb, 1, d),
                                         lambda b, i, seq0=seq0: (layer, sub, seq0(b), 0, 0))]
        args += [x, mod]
        if pre_attn:
            gate1 = 3 * (sub - 1) + 2
            in_specs += [xspec, pl.BlockSpec((None, 1, nb, 1, d),
                                             lambda b, i, seq0=seq0: (layer, gate1, seq0(b), 0, 0))]
            args += [attns[k], mod]
        out_specs.append(xspec)
    in_specs.append(pl.BlockSpec((None, None, 1, d), lambda b, i: (layer, sub, 0, 0)))
    args.append(norm_g)
    if pre_attn:
        in_specs.append(_resident((None, d, d), lambda b, i: (wo_idx, 0, 0)))
        args.append(wo)
    in_specs += [pl.BlockSpec(memory_space=pl.ANY)] * 3
    args += [wg, wu, wd]
    if final_norm:
        in_specs.append(pl.BlockSpec((1, d), lambda b, i: (0, 0)))
        args.append(final_g.reshape(1, d))
    ff_chunk = 256 if d_ff % 256 == 0 else LANES
    n_chunks = d_ff // ff_chunk
    gu_slots = min(4, 2 * n_chunks)
    wd_slots = min(2, n_chunks)
    return pl.pallas_call(
        functools.partial(_ffn_kernel, shapes=shapes, ff_chunk=ff_chunk, pre_attn=pre_attn,
                          final_norm=final_norm, layer=layer, half=half),
        grid=(nseq, s // t),
        in_specs=in_specs,
        out_specs=out_specs,
        out_shape=[jax.ShapeDtypeStruct(x.shape, F32) for x in xs],
        scratch_shapes=[
            pltpu.VMEM((max(nb * rows for nb, rows in shapes), d_ff), BF16),
            pltpu.VMEM((d, d_ff), BF16), pltpu.VMEM((d, d_ff), BF16),
            pltpu.VMEM((d_ff, d), BF16),
            pltpu.VMEM((gu_slots, d, ff_chunk), F32),
            pltpu.VMEM((wd_slots, ff_chunk, d), F32),
            pltpu.SemaphoreType.DMA((gu_slots,)), pltpu.SemaphoreType.DMA((wd_slots,)),
        ],
        compiler_params=_cparams(("arbitrary", "arbitrary")),
        name="ffn",
    )(*args)


def _conv_kernel(x_ref, mod_ref, g_ref, hist_ref, w1_ref, b1_ref, wdw_ref, bdw_ref, lng_ref,
                 lnb_ref, w2_ref, b2_ref, o_ref, st_ref, full_ref, y_ref, shift_ref, *, nb, t,
                 n_tiles, n_parts, row_blk, lane_blk):
    c = w2_ref.shape[0]
    d = x_ref.shape[-1]
    first = HIST_PAD - (CONV_WIDTH - 1)
    shift_rows = shift_ref.shape[2]

    @pl.when(pl.program_id(1) == 0)
    def _():
        full_ref[:, 0:HIST_PAD, :] = hist_ref[...]

    part = t // n_parts

    def glu(p):
        xp = x_ref[:, p * part:(p + 1) * part, :]
        h = _mod_norm(xp, g_ref[...], mod_ref[0], mod_ref[1])
        a = _dot(h.reshape(nb * part, d).astype(BF16), w1_ref[...]) + b1_ref[...]
        u = a[:, :c] * jax.nn.sigmoid(a[:, c:])
        full_ref[:, HIST_PAD + p * part:HIST_PAD + (p + 1) * part, :] = u.reshape(nb, part, c)

    def depthwise(p):
        row0 = p * part
        for c0 in range(0, c, lane_blk):
            lanes = slice(c0, c0 + lane_blk)
            for s in range(1, SUBLANES):
                shift_ref[s - 1] = full_ref[:, row0 + s:row0 + s + shift_rows, lanes]
            for r0 in range(0, part, row_blk):
                acc = jnp.zeros((nb * row_blk // SUBLANES, SUBLANES, lane_blk), F32)
                for w in range(CONV_WIDTH):
                    base, s = divmod(first + w, SUBLANES)
                    lo = base * SUBLANES + r0
                    src = (full_ref[:, row0 + lo:row0 + lo + row_blk, lanes] if s == 0
                           else shift_ref[s - 1, :, lo:lo + row_blk, :])
                    acc = acc + src.reshape(acc.shape) * wdw_ref[w, :, lanes][None]
                y_ref[:, row0 + r0:row0 + r0 + row_blk, lanes] = (
                    acc.reshape(nb, row_blk, lane_blk) + bdw_ref[:, lanes])

    def project(p):
        rows = slice(p * part, (p + 1) * part)
        y = y_ref[:, rows, :].reshape(nb * part, c)
        mu = jnp.mean(y, axis=-1, keepdims=True)
        yc = y - mu
        var = jnp.mean(yc * yc, axis=-1, keepdims=True)
        yn = yc * lax.rsqrt(var + NORM_EPS) * lng_ref[...] + lnb_ref[...]
        out = _dot(jax.nn.silu(yn).astype(BF16), w2_ref[...]) + b2_ref[...]
        o_ref[:, rows, :] = x_ref[:, rows, :] + mod_ref[2] * out.reshape(nb, part, d)

    glu(0)
    for p in range(n_parts):
        if p + 1 < n_parts:
            glu(p + 1)
        depthwise(p)
        project(p)

    st_ref[...] = full_ref[:, t + first:t + HIST_PAD, :]
    if n_tiles > 1:
        full_ref[:, 0:HIST_PAD, :] = full_ref[:, t:t + HIST_PAD, :]


def _conv_mixer(x, mod, norm_g, hist, w1, b1, wdw, bdw, lng, lnb, w2, b2, *, layer, jm, nb, t):
    nseq, s, d = x.shape
    c = w2.shape[1]
    n_tiles = s // t
    acc_vregs = 16
    row_blk = min(t, 64)
    n_parts = 2 if t % (2 * row_blk) == 0 else 1
    part = t // n_parts
    lane_blk = max(LANES, min(c, acc_vregs * SUBLANES * LANES // (nb * row_blk) // LANES * LANES))
    xspec = pl.BlockSpec((nb, t, d), lambda b, i: (b, i, 0))
    const = lambda b, i: (jm, 0, 0)
    vec = lambda a: a.reshape(a.shape[0], 1, a.shape[1])
    hist32 = jnp.pad(hist, ((0, 0), (HIST_PAD - (CONV_WIDTH - 1), 0), (0, 0)))
    return pl.pallas_call(
        functools.partial(_conv_kernel, nb=nb, t=t, n_tiles=n_tiles, n_parts=n_parts,
                          row_blk=row_blk, lane_blk=lane_blk),
        grid=(nseq // nb, n_tiles),
        in_specs=[
            xspec,
            pl.BlockSpec((None, 3, nb, 1, d), lambda b, i: (layer, 1, b, 0, 0)),
            pl.BlockSpec((None, None, 1, d), lambda b, i: (layer, 1, 0, 0)),
            pl.BlockSpec((nb, HIST_PAD, c), lambda b, i: (b, 0, 0)),
            _resident((None, d, 2 * c), const),
            pl.BlockSpec((None, 1, 2 * c), const),
            pl.BlockSpec((None, CONV_WIDTH, SUBLANES, c), lambda b, i: (jm, 0, 0, 0)),
            pl.BlockSpec((None, 1, c), const),
            pl.BlockSpec((None, 1, c), const),
            pl.BlockSpec((None, 1, c), const),
            _resident((None, c, d), const),
            pl.BlockSpec((None, 1, d), const),
        ],
        out_specs=[xspec, pl.BlockSpec((nb, CONV_WIDTH - 1, c), lambda b, i: (b, 0, 0))],
        out_shape=[jax.ShapeDtypeStruct(x.shape, F32),
                   jax.ShapeDtypeStruct((nseq, CONV_WIDTH - 1, c), F32)],
        scratch_shapes=[pltpu.VMEM((nb, HIST_PAD + t, c), F32), pltpu.VMEM((nb, t, c), F32),
                        pltpu.VMEM((SUBLANES - 1, nb, HIST_PAD - SUBLANES + part, lane_blk), F32)],
        compiler_params=_cparams(("arbitrary", "arbitrary")),
        name="conv_mixer",
    )(x, mod, norm_g, hist32, w1, vec(b1),
      jnp.broadcast_to(wdw[:, :, None, :], wdw.shape[:2] + (SUBLANES, c)),
      vec(bdw), vec(lng), vec(lnb), w2, vec(b2))


def _qkv_kernel(x_ref, mod_ref, g_ref, w_ref, q_ref, k_ref, v_ref, *, nb, t, q_scale):
    x = x_ref[...]
    d = x.shape[-1]
    h = _mod_norm(x, g_ref[...], mod_ref[0], mod_ref[1])
    hb = h.reshape(nb * t, d).astype(BF16)
    q = _dot(hb, w_ref[:, 0:d]) * q_scale
    q_ref[...] = q.reshape(nb, t, d).astype(q_ref.dtype)
    k_ref[...] = _dot(hb, w_ref[:, d:2 * d]).reshape(nb, t, d)
    v_ref[...] = _dot(hb, w_ref[:, 2 * d:3 * d]).reshape(nb, t, d)


def _qkv(x, mod, norm_g, w, *, layer, jm, nb, t, q_dtype, q_scale):
    nseq, s, d = x.shape
    xspec = pl.BlockSpec((nb, t, d), lambda b, i: (b, i, 0))
    return pl.pallas_call(
        functools.partial(_qkv_kernel, nb=nb, t=t, q_scale=q_scale),
        grid=(nseq // nb, s // t),
        in_specs=[
            xspec,
            pl.BlockSpec((None, 3, nb, 1, d), lambda b, i: (layer, 1, b, 0, 0)),
            pl.BlockSpec((None, None, 1, d), lambda b, i: (layer, 1, 0, 0)),
            _resident((None, d, 3 * d), lambda b, i: (jm, 0, 0)),
        ],
        out_specs=[xspec, xspec, xspec],
        out_shape=[jax.ShapeDtypeStruct(x.shape, q_dtype), jax.ShapeDtypeStruct(x.shape, F32),
                   jax.ShapeDtypeStruct(x.shape, F32)],
        compiler_params=_cparams(("arbitrary", "arbitrary")),
        name="qkv",
    )(x, mod, norm_g, w)


def _bucket_of_distance(n_dist):
    dist = np.arange(n_dist)
    max_exact = NUM_BUCKETS // 2
    nf = np.maximum(dist, max_exact).astype(np.float32)
    large = max_exact + (np.log(nf / max_exact) / math.log(MAX_DISTANCE / max_exact)
                         * (NUM_BUCKETS - max_exact)).astype(np.int32)
    large = np.minimum(large, NUM_BUCKETS - 1)
    return np.where(dist < max_exact, dist, large).astype(np.int32)


_BUCKETS = _bucket_of_distance(4 * MAX_DISTANCE)
assert _BUCKETS[-1] == NUM_BUCKETS - 1 and np.all(np.diff(_BUCKETS) >= 0)
_BUCKET_START = [int(np.argmax(_BUCKETS >= b)) for b in range(NUM_BUCKETS)]
assert _BUCKET_START[NUM_BUCKETS - 1] <= MOBA_BLOCK


def _bias_of_distance(dist, table):
    bias = jnp.full(dist.shape, table[0], F32)
    for b in range(1, NUM_BUCKETS):
        bias = jnp.where(dist >= _BUCKET_START[b], table[b], bias)
    return bias


def _rank_select(gate, n_valid, first):
    idx = lax.broadcasted_iota(jnp.int32, gate.shape, 1)
    cnt = jnp.zeros(gate.shape, F32)
    for j in range(first, first + n_valid):
        gj = gate[:, j:j + 1]
        tie = jnp.where(idx > j, 1.0, 0.0)
        cnt = cnt + jnp.where(gj > gate, 1.0, jnp.where(gj == gate, tie, 0.0))
    return cnt < float(MOBA_TOP_K)


def _top_k_lanes(gate, valid):
    lane_f = lax.broadcasted_iota(jnp.int32, gate.shape, 1).astype(F32)
    g = jnp.where(valid, gate, -jnp.inf)
    picks = []
    for _ in range(MOBA_TOP_K):
        top = jnp.max(g, axis=1, keepdims=True)
        pick = jnp.min(jnp.where(g == top, lane_f, float(LANES)), axis=1, keepdims=True)
        picks.append(pick)
        g = jnp.where(lane_f == pick, -jnp.inf, g)
    return picks


def _is_picked(picks, n):
    hit = jnp.where(picks[0] == n, 1.0, 0.0)
    for pick in picks[1:]:
        hit = jnp.where(pick == n, 1.0, hit)
    return hit > 0.0


def _moba_prompt_kernel(rb_ref, q_ref, k_ref, v_ref, o_ref, bias_ref, *, n_blk, n_heads):
    pair = pl.program_id(0)
    blk = MOBA_BLOCK

    @pl.when(pl.program_id(1) == 0)
    def _():
        dist = (lax.broadcasted_iota(jnp.int32, (blk, blk), 0)
                - lax.broadcasted_iota(jnp.int32, (blk, blk), 1))
        for hh in range(2):
            table = [rb_ref[b * n_heads + 2 * pair + hh] for b in range(NUM_BUCKETS)]
            far = table[NUM_BUCKETS - 1]
            own = (_bias_of_distance(dist, table) - far) * LOG2E
            bias_ref[hh, 0] = jnp.where(dist >= 0, own, -jnp.inf)
            bias_ref[hh, 1] = (_bias_of_distance(dist + blk, table) - far) * LOG2E

    kf = k_ref[...]
    vf = v_ref[...]
    s_len = kf.shape[0]
    lane = lax.broadcasted_iota(jnp.int32, (1, LANES), 1)
    key_lane = lax.broadcasted_iota(jnp.int32, (s_len, LANES), 1)
    key_blk = lax.broadcasted_iota(jnp.int32, (s_len, LANES), 0) // blk
    km_row = lax.broadcasted_iota(jnp.int32, (LANES, LANES), 0)

    heads = []
    for hh in range(2):
        off = (1 - hh) * HEAD_DIM
        hmask = (lane // HEAD_DIM) == hh
        k_aug = jnp.where(hmask, kf, jnp.where(key_lane == key_blk + off, 1.0, 0.0)).astype(BF16)
        v_aug = jnp.where(hmask, vf, 1.0).astype(BF16)
        km = jnp.zeros((LANES, LANES), F32)
        for j in range(n_blk):
            km = jnp.where(km_row == off + j,
                           jnp.mean(kf[j * blk:(j + 1) * blk, :], axis=0, keepdims=True), km)
        heads.append((hh, off, hmask, k_aug, v_aug, km.astype(BF16)))

    def logits(i, head):
        hh, off, hmask, k_aug, _, km = head
        qi = q_ref[i * blk:(i + 1) * blk, :]
        if i > MOBA_TOP_K:
            gate = _dot_nt(jnp.where(hmask, qi, jnp.zeros_like(qi)), km)
            keep = _rank_select(gate, i, first=off)
            past = jnp.logical_and(lane >= off, lane < off + i)
            drop = jnp.where(past, jnp.where(keep, 0.0, MASKED), 0.0)
            q_aug = jnp.where(hmask, qi, drop.astype(BF16))
        else:
            q_aug = jnp.where(hmask, qi, jnp.zeros_like(qi))
        for j in range(i + 1):
            s = _dot_nt(q_aug, k_aug[j * blk:(j + 1) * blk, :])
            if j == i:
                s = s + bias_ref[hh, 0]
            elif j == i - 1:
                s = s + bias_ref[hh, 1]
            yield s

    def weighted_values(s_list, head):
        v_aug = head[4]
        m = jnp.max(functools.reduce(jnp.maximum, s_list), axis=-1, keepdims=True)
        acc = jnp.zeros((blk, LANES), F32)
        for j, s in enumerate(s_list):
            acc = acc + _dot(jnp.exp2(s - m).astype(BF16), v_aug[j * blk:(j + 1) * blk, :])
        return acc

    units = [(i, head) for i in range(n_blk) for head in heads]
    low = lane < HEAD_DIM
    ahead = 2
    pending = [list(logits(*unit)) for unit in units[:ahead]]
    accs = []
    for u, (i, head) in enumerate(units):
        if u + ahead < len(units):
            pending.append(list(logits(*units[u + ahead])))
        accs.append(weighted_values(pending.pop(0), head))
        if len(accs) == 2:
            num = jnp.where(low, accs[0], accs[1])
            den = pltpu.roll(jnp.where(low, accs[1], accs[0]), HEAD_DIM, axis=1)
            o_ref[i * blk:(i + 1) * blk, :] = (num * (1.0 / den)).astype(o_ref.dtype)
            accs = []


def _moba_prompt(q, k, v, rel_bias):
    b, s, d = q.shape
    n_heads = d // HEAD_DIM
    blk = MOBA_BLOCK
    n_blk = s // blk
    assert s % blk == 0 and n_heads % 2 == 0 and 2 * HEAD_DIM == LANES
    hspec = pl.BlockSpec((None, s, LANES), lambda p, bi, rb_ref: (bi, 0, p))
    return pl.pallas_call(
        functools.partial(_moba_prompt_kernel, n_blk=n_blk, n_heads=n_heads),
        grid_spec=pltpu.PrefetchScalarGridSpec(
            num_scalar_prefetch=1,
            grid=(n_heads // 2, b),
            in_specs=[hspec, hspec, hspec],
            out_specs=hspec,
            scratch_shapes=[pltpu.VMEM((2, 2, blk, blk), F32)],
        ),
        out_shape=jax.ShapeDtypeStruct((b, s, d), BF16),
        compiler_params=_cparams(("arbitrary", "arbitrary")),
        name="moba_prompt",
    )(rel_bias.reshape(-1), q, k, v)


def _moba_sample_kernel(pt_ref, rb_ref, *refs, n_groups, n_blk, n_heads, t):
    pp = PAGES_PER_STEP
    (ck_hbm, cv_hbm, q_ref, kn_ref, vn_ref, o_ref, ring_ref, sem, a_ref, s_ref, p_ref, pown_ref,
     l_ref, acc_ref, blast_ref, bown_ref, gate_ref, bmax_ref) = refs
    seq = pl.program_id(0)
    step = pl.program_id(1)
    blk = MOBA_BLOCK
    ppb = blk // PAGE_SIZE
    bps = pp // ppb
    d = q_ref.shape[-1]
    cols = n_heads * t
    row_head = lax.broadcasted_iota(jnp.int32, (cols, d), 0) // t
    lane_head = lax.broadcasted_iota(jnp.int32, (cols, d), 1) // HEAD_DIM
    diag = row_head == lane_head
    lane = lax.broadcasted_iota(jnp.int32, (cols, LANES), 1)

    steps = 2 * n_groups
    n_pages = n_groups * pp
    group = seq * steps + step
    n_stream = pl.num_programs(0) * steps

    def group_copies(m, src_hbm, first_page):
        slot = m % RING_GROUPS
        return [pltpu.make_async_copy(
            src_hbm.at[pt_ref[(m // steps) * n_pages + first_page + r]],
            ring_ref.at[slot, r], sem.at[slot]) for r in range(pp)]

    def start_group(m):
        st = m % steps

        @pl.when(st < n_groups)
        def _():
            for copy in group_copies(m, ck_hbm, st * pp):
                copy.start()

        @pl.when(st >= n_groups)
        def _():
            for copy in group_copies(m, cv_hbm, (st - n_groups) * pp):
                copy.start()

    @pl.when(group == 0)
    def _():
        for m in range(RING_GROUPS - 1):
            start_group(jnp.int32(m))

    @pl.when(group + RING_GROUPS - 1 < n_stream)
    def _():
        start_group(group + RING_GROUPS - 1)

    for copy in group_copies(group, ck_hbm, 0):
        copy.wait()
    pages_ref = ring_ref.at[group % RING_GROUPS]

    def new_rows(ref):
        return jnp.concatenate([ref[...], jnp.zeros((PAGE_SIZE - t, ref.shape[-1]), F32)],
                               axis=0).astype(BF16)

    def block_of(r):
        return jnp.concatenate([pages_ref[ppb * r + i] for i in range(ppb)], axis=1).astype(BF16)

    @pl.when(jnp.logical_and(seq == 0, step == 0))
    def _():
        qi = lax.broadcasted_iota(jnp.int32, (t, blk), 0)
        ki = lax.broadcasted_iota(jnp.int32, (t, blk), 1)
        qo = lax.broadcasted_iota(jnp.int32, (t, PAGE_SIZE), 0)
        ko = lax.broadcasted_iota(jnp.int32, (t, PAGE_SIZE), 1)
        for h in range(n_heads):
            table = [rb_ref[b * n_heads + h] for b in range(NUM_BUCKETS)]
            far = table[NUM_BUCKETS - 1]
            rows = slice(h * t, (h + 1) * t)
            blast_ref[rows, :] = _bias_of_distance(blk + qi - ki, table) - far
            bown_ref[rows, :] = jnp.where(ko <= qo, _bias_of_distance(qo - ko, table) - far,
                                          -jnp.inf)

    @pl.when(step == 0)
    def _():
        qt = jnp.tile(q_ref[...], (n_heads, 1))
        a_ref[...] = jnp.where(diag, qt, 0.0).astype(BF16)
        acc_ref[...] = jnp.zeros_like(acc_ref)
        gate_ref[...] = jnp.full_like(gate_ref, -jnp.inf)
        bmax_ref[...] = jnp.full_like(bmax_ref, -jnp.inf)

    @pl.when(step < n_groups)
    def _():
        a = a_ref[...]
        gate = gate_ref[...]
        bmax = bmax_ref[...]
        for r in range(bps):
            n = step * bps + r
            s = _dot(a, block_of(r))
            s_ref[n] = s
            gate = jnp.where(lane == n, jnp.sum(s, axis=1, keepdims=True), gate)
            bmax = jnp.where(lane == n, jnp.max(s, axis=1, keepdims=True), bmax)
        gate_ref[...] = gate
        bmax_ref[...] = bmax

    @pl.when(step == n_groups)
    def _():
        picks = _top_k_lanes(gate_ref[...], lane < n_blk)
        picked = functools.partial(_is_picked, picks)
        lane_f = lane.astype(F32)
        last = n_blk - 1
        s_own = _dot_nt(a_ref[...], new_rows(kn_ref)) + bown_ref[...]
        bmax = jnp.where(lane == last,
                         jnp.max(s_ref[last] + blast_ref[...], axis=1, keepdims=True), bmax_ref[...])
        m = jnp.maximum(jnp.max(s_own, axis=1, keepdims=True),
                        jnp.max(jnp.where(picked(lane_f), bmax, -jnp.inf), axis=1, keepdims=True))
        p_own = jnp.exp(s_own - m)
        lsum = jnp.zeros((cols, blk), F32)
        for n in range(n_blk):
            sn = s_ref[n] + blast_ref[...] if n == last else s_ref[n]
            p = jnp.where(picked(float(n)), jnp.exp(sn - m), 0.0)
            lsum = lsum + p
            p_ref[n] = p.astype(BF16)
        l = jnp.sum(lsum, axis=1, keepdims=True) + jnp.sum(p_own, axis=1, keepdims=True)
        pown_ref[...] = p_own.astype(BF16)
        l_ref[...] = jnp.broadcast_to(l, (cols, LANES))

    @pl.when(step >= n_groups)
    def _():
        acc = acc_ref[...]
        for r in range(bps):
            acc = acc + _dot_nt(p_ref[(step - n_groups) * bps + r], block_of(r))
        acc_ref[...] = acc

    @pl.when(step == 2 * n_groups - 1)
    def _():
        acc = acc_ref[...] + _dot(pown_ref[...], new_rows(vn_ref))
        acc = acc * jnp.tile(1.0 / l_ref[...], (1, d // LANES))
        acc = jnp.where(diag, acc, 0.0)
        o_ref[...] = jnp.sum(acc.reshape(n_heads, t, d), axis=0).astype(o_ref.dtype)


def _moba_sample(q, k_new, v_new, cache_k, cache_v, page_table, rel_bias):
    nseq, t, d = q.shape
    n_heads = d // HEAD_DIM
    n_pages = page_table.shape[1]
    past = n_pages * PAGE_SIZE
    blk = MOBA_BLOCK
    n_blk = past // blk
    pp = PAGES_PER_STEP
    cols = n_heads * t
    assert past % blk == 0 and blk % PAGE_SIZE == 0 and pp % (blk // PAGE_SIZE) == 0
    assert n_pages % pp == 0 and cols % SUBLANES == 0 and t % SUBLANES == 0 and t <= PAGE_SIZE
    assert MOBA_TOP_K <= n_blk <= LANES
    g = n_pages // pp
    assert nseq * 2 * g >= RING_GROUPS

    seq = pl.BlockSpec((None, t, d), lambda b, s, pt, rb: (b, 0, 0))
    hbm = pl.BlockSpec(memory_space=pl.ANY)
    in_specs = [hbm, hbm, seq, seq, seq]
    return pl.pallas_call(
        functools.partial(_moba_sample_kernel, n_groups=g, n_blk=n_blk, n_heads=n_heads, t=t),
        grid_spec=pltpu.PrefetchScalarGridSpec(
            num_scalar_prefetch=2,
            grid=(nseq, 2 * g),
            in_specs=in_specs,
            out_specs=seq,
            scratch_shapes=[
                pltpu.VMEM((RING_GROUPS, pp, d, PAGE_SIZE), F32),
                pltpu.SemaphoreType.DMA((RING_GROUPS,)),
                pltpu.VMEM((cols, d), BF16),
                pltpu.VMEM((n_blk, cols, blk), F32),
                pltpu.VMEM((n_blk, cols, blk), BF16),
                pltpu.VMEM((cols, PAGE_SIZE), BF16),
                pltpu.VMEM((cols, LANES), F32),
                pltpu.VMEM((cols, d), F32),
                pltpu.VMEM((cols, blk), F32),
                pltpu.VMEM((cols, PAGE_SIZE), F32),
                pltpu.VMEM((cols, LANES), F32),
                pltpu.VMEM((cols, LANES), F32),
            ],
        ),
        out_shape=jax.ShapeDtypeStruct((nseq, t, d), BF16),
        compiler_params=_cparams(("arbitrary", "arbitrary")),
        name="moba_sample",
    )(page_table.reshape(-1), rel_bias.reshape(-1), cache_k, cache_v, q, k_new, v_new)


def _tile_rows(s):
    for t in (512, 256, 128, 64, 32, 16, 8):
        if s % t == 0:
            return t
    raise ValueError(f"sequence length {s} is not a multiple of {SUBLANES}")


def _trunk(xs, mods, hists_in, past, page_table, w, *, t):
    d = xs[0].shape[-1]
    n_heads = d // HEAD_DIM
    depth = w["norm_g"].shape[0]
    tiles = [(1, t), (SUBLANES, xs[1].shape[1])]
    qkv_tiles = [(1, t), xs[1].shape[:2]]
    hists, new_k, new_v = [[], []], [[], []], [[], []]
    for layer in range(depth):
        jm = layer // 2
        ffn = functools.partial(_ffn, mods=mods, norm_g=w["norm_g"], wg=w["wg"], wu=w["wu"],
                                wd=w["wd"], layer=layer, t=t)
        xs = list(ffn(xs, sub=0))
        attns = None
        if layer % 2 == 0:
            for k in range(2):
                xs[k], st = _conv_mixer(xs[k], mods[k], w["norm_g"], hists_in[k][jm], w["pw1"],
                                        w["b_pw1"], w["dw"], w["b_dw"], w["ln_g"], w["ln_b"],
                                        w["pw2"], w["b_pw2"], layer=layer, jm=jm,
                                        nb=tiles[k][0], t=tiles[k][1])
                hists[k].append(st)
        else:
            scale = 1.0 / math.sqrt(HEAD_DIM)
            attns = []
            for k in range(2):
                nseq, s, _ = xs[k].shape
                q, kk, vv = _qkv(xs[k], mods[k], w["norm_g"], w["qkv"], layer=layer, jm=jm,
                                 nb=qkv_tiles[k][0], t=qkv_tiles[k][1],
                                 q_dtype=BF16 if k == 0 else F32,
                                 q_scale=scale * LOG2E if k == 0 else scale)
                if k == 0:
                    attns.append(_moba_prompt(q, kk, vv, w["rel_bias"]))
                else:
                    ck, cv, n_phys = past
                    attns.append(_moba_sample(q, kk, vv, ck, cv, page_table + jm * n_phys,
                                              w["rel_bias"]))
                new_k[k].append(kk.reshape(nseq, s, n_heads, HEAD_DIM))
                new_v[k].append(vv.reshape(nseq, s, n_heads, HEAD_DIM))
        last = layer == depth - 1
        xs = ffn(xs, sub=2, attns=attns, wo=w["wo"] if attns is not None else None, wo_idx=jm,
                 final_g=w["final_g"] if last else None)
    return xs, [jnp.stack(h) for h in hists], [jnp.stack(a) for a in new_k], \
        [jnp.stack(a) for a in new_v]


def kernel(x_prompt, x_sample, state_conv, cache_k, cache_v, page_table, c_prompt, c_sample, norm_g, ada_w, ada_b, ffn_w_gate, ffn_w_up, ffn_w_down, conv_w_pw1, conv_b_pw1, conv_w_dw, conv_b_dw, conv_ln_g, conv_ln_b, conv_w_pw2, conv_b_pw2, attn_w_qkv, attn_w_o, rel_bias, final_norm_g):
    b, s, d = x_prompt.shape
    depth = norm_g.shape[0]
    w = dict(
        norm_g=norm_g.reshape(depth, 3, 1, d),
        wg=ffn_w_gate, wu=ffn_w_up, wd=ffn_w_down,
        pw1=conv_w_pw1.astype(BF16), b_pw1=conv_b_pw1, dw=conv_w_dw, b_dw=conv_b_dw,
        ln_g=conv_ln_g, ln_b=conv_ln_b, pw2=conv_w_pw2.astype(BF16), b_pw2=conv_b_pw2,
        qkv=attn_w_qkv.astype(BF16), wo=attn_w_o.astype(BF16), rel_bias=rel_bias,
        final_g=final_norm_g,
    )
    mod_p, mod_s = _ada_mod(c_prompt, c_sample, ada_w, ada_b)

    n_layers, n_phys = cache_k.shape[:2]
    pages = lambda c: jnp.transpose(c, (0, 1, 3, 4, 2)).reshape(n_layers * n_phys, d, PAGE_SIZE)

    hist0 = jnp.zeros((conv_w_dw.shape[0], b, CONV_WIDTH - 1, conv_w_dw.shape[2]), F32)
    ys, states, ks, vs = _trunk([x_prompt, x_sample], [mod_p, mod_s], [hist0, state_conv],
                                (pages(cache_k), pages(cache_v), n_phys), page_table, w,
                                t=_tile_rows(s))
    return (ys[0], ys[1], states[0], states[1], ks[0], vs[0], ks[1], vs[1])
```

```python
import functools
import math

import numpy as np
import jax
import jax.numpy as jnp
from jax import lax
from jax.experimental import pallas as pl
from jax.experimental.pallas import tpu as pltpu

F32 = jnp.float32
BF16 = jnp.bfloat16

HEAD_DIM = 64
CONV_WIDTH = 31
MOBA_BLOCK = 256
MOBA_TOP_K = 3
PAGE_SIZE = 128
NUM_BUCKETS = 32
MAX_DISTANCE = 128
NORM_EPS = 1e-6

LANES = 128
SUBLANES = 8
HIST_PAD = 32
VMEM_LIMIT = 56 * 1024 * 1024
PAGES_PER_STEP = 8
RING_GROUPS = 4
LOG2E = math.log2(math.e)
MASKED = -1e30


def _cparams(sem):
    return pltpu.CompilerParams(dimension_semantics=sem, vmem_limit_bytes=VMEM_LIMIT)


def _resident(block_shape, index_map):
    return pl.BlockSpec(block_shape, index_map, pipeline_mode=pl.Buffered(1))


def _mod_norm(x, g, shift, scale):
    ms = jnp.mean(x * x, axis=-1, keepdims=True)
    n = x * lax.rsqrt(ms + NORM_EPS)
    return n * g * (1.0 + scale) + shift


def _dot(a, b):
    return jnp.dot(a, b, preferred_element_type=F32)


def _dot_nt(a, b):
    return lax.dot_general(a, b, (((1,), (1,)), ((), ())), preferred_element_type=F32)


def _dot_tn(a, b):
    return lax.dot_general(a, b, (((0,), (0,)), ((), ())), preferred_element_type=F32)


def _mod_kernel(cp_ref, cs_ref, w_ref, b_ref, op_ref, os_ref):
    d = cp_ref.shape[1]
    scp = jax.nn.silu(cp_ref[...]).astype(BF16)
    scs = jax.nn.silu(cs_ref[...]).astype(BF16)
    for m in range(op_ref.shape[0]):
        w = w_ref[:, m * d:(m + 1) * d].astype(BF16)
        op_ref[m] = _dot(scp, w) + b_ref[m]
        os_ref[m] = _dot(scs, w) + b_ref[m]


def _ada_mod(c_prompt, c_sample, ada_w, ada_b):
    d = c_prompt.shape[1]
    depth = ada_w.shape[0]
    n_mod = ada_w.shape[2] // d
    per_step = 3
    assert n_mod % per_step == 0
    outs = pl.pallas_call(
        _mod_kernel,
        grid=(depth, n_mod // per_step),
        in_specs=[
            pl.BlockSpec(c_prompt.shape, lambda l, j: (0, 0)),
            pl.BlockSpec(c_sample.shape, lambda l, j: (0, 0)),
            pl.BlockSpec((None, d, per_step * d), lambda l, j: (l, 0, j)),
            pl.BlockSpec((None, per_step, 1, d), lambda l, j: (l, j, 0, 0)),
        ],
        out_specs=[pl.BlockSpec((None, per_step) + c.shape, lambda l, j: (l, j, 0, 0))
                   for c in (c_prompt, c_sample)],
        out_shape=[jax.ShapeDtypeStruct((depth, n_mod) + c.shape, F32)
                   for c in (c_prompt, c_sample)],
        compiler_params=_cparams(("arbitrary", "arbitrary")),
        name="ada_mod",
    )(c_prompt, c_sample, ada_w, ada_b.reshape(depth, n_mod, 1, d))
    return [o.reshape(depth, n_mod, o.shape[2], 1, d) for o in outs]


def _ffn_kernel(*refs, shapes, ff_chunk, pre_attn, final_norm, layer, half):
    it = iter(refs)
    groups = []
    for _ in shapes:
        x_ref, mod_ref = next(it), next(it)
        a_ref, gate1_ref = (next(it), next(it)) if pre_attn else (None, None)
        groups.append((x_ref, mod_ref, a_ref, gate1_ref))
    g_ref = next(it)
    wo_ref = next(it) if pre_attn else None
    wg_hbm, wu_hbm, wd_hbm = next(it), next(it), next(it)
    fg_ref = next(it) if final_norm else None
    o_refs = [next(it) for _ in shapes]
    act_ref, wg_ref, wu_ref, wd_ref, gu_stage, wd_stage, gu_sem, wd_sem = (next(it) for _ in range(8))
    d_ff = wg_ref.shape[-1]
    n_chunks = d_ff // ff_chunk
    gu_slots, wd_slots = gu_stage.shape[0], wd_stage.shape[0]

    def gu_copy(c):
        cols = slice((c // 2) * ff_chunk, (c // 2 + 1) * ff_chunk)
        src = (wg_hbm if c % 2 == 0 else wu_hbm).at[layer, half, :, cols]
        return pltpu.make_async_copy(src, gu_stage.at[c % gu_slots], gu_sem.at[c % gu_slots])

    def wd_copy(j):
        src = wd_hbm.at[layer, half, j * ff_chunk:(j + 1) * ff_chunk, :]
        return pltpu.make_async_copy(src, wd_stage.at[j % wd_slots], wd_sem.at[j % wd_slots])

    def stage_chunk(j):
        cols = slice(j * ff_chunk, (j + 1) * ff_chunk)
        for c, dst in ((2 * j, wg_ref), (2 * j + 1, wu_ref)):
            gu_copy(c).wait()
            dst[:, cols] = gu_stage[c % gu_slots].astype(BF16)
            if c + gu_slots < 2 * n_chunks:
                gu_copy(c + gu_slots).start()
        wd_copy(j).wait()
        wd_ref[cols, :] = wd_stage[j % wd_slots].astype(BF16)
        if j + wd_slots < n_chunks:
            wd_copy(j + wd_slots).start()

    def run(k, stage):
        nb, t = shapes[k]
        x_ref, mod_ref, a_ref, gate1_ref = groups[k]
        if stage:
            for c in range(gu_slots):
                gu_copy(c).start()
            for j in range(wd_slots):
                wd_copy(j).start()
        x = x_ref[...]
        d = x.shape[-1]
        if pre_attn:
            y = _dot(a_ref[...].reshape(nb * t, d), wo_ref[...])
            x = x + gate1_ref[0] * y.reshape(nb, t, d)
        h = _mod_norm(x, g_ref[...], mod_ref[0], mod_ref[1])
        hb = h.reshape(nb * t, d).astype(BF16)
        for j in range(n_chunks):
            if stage:
                stage_chunk(j)
            sl = slice(j * ff_chunk, (j + 1) * ff_chunk)
            gj = _dot(hb, wg_ref[:, sl])
            uj = _dot(hb, wu_ref[:, sl])
            act_ref[0:nb * t, sl] = (jax.nn.silu(gj) * uj).astype(BF16)
        y = _dot(act_ref[0:nb * t, :], wd_ref[...])
        out = x + 0.5 * mod_ref[2] * y.reshape(nb, t, d)
        if final_norm:
            ms = jnp.mean(out * out, axis=-1, keepdims=True)
            out = out * lax.rsqrt(ms + NORM_EPS) * fg_ref[...]
        o_refs[k][...] = out

    first_step = jnp.logical_and(pl.program_id(0) == 0, pl.program_id(1) == 0)

    @pl.when(first_step)
    def _():
        run(0, stage=True)
        for k in range(1, len(shapes)):
            run(k, stage=False)

    @pl.when(jnp.logical_not(first_step))
    def _():
        run(0, stage=False)


def _ffn(xs, mods, norm_g, wg, wu, wd, *, layer, sub, t, attns=None, wo=None, wo_idx=0,
         final_g=None):
    nseq, s, d = xs[0].shape
    d_ff = wg.shape[-1]
    half = sub // 2
    pre_attn = attns is not None
    final_norm = final_g is not None
    shapes = [(1, t)] + [x.shape[:2] for x in xs[1:]]
    tile = lambda b, i: (b, i, 0)
    whole = lambda b, i: (0, 0, 0)
    in_specs, args, out_specs = [], [], []
    for k, (x, mod) in enumerate(zip(xs, mods)):
        nb, rows = shapes[k]
        seq0 = (lambda b: b) if k == 0 else (lambda b: 0)
        xspec = pl.BlockSpec((nb, rows, d), tile if k == 0 else whole)
        in_specs += [xspec, pl.BlockSpec((None, 3, nb, 1, d),
                                         lambda b, i, seq0=seq0: (layer, sub, seq0(b), 0, 0))]
        args += [x, mod]
        if pre_attn:
            gate1 = 3 * (sub - 1) + 2
            in_specs += [xspec, pl.BlockSpec((None, 1, nb, 1, d),
                                             lambda b, i, seq0=seq0: (layer, gate1, seq0(b), 0, 0))]
            args += [attns[k], mod]
        out_specs.append(xspec)
    in_specs.append(pl.BlockSpec((None, None, 1, d), lambda b, i: (layer, sub, 0, 0)))
    args.append(norm_g)
    if pre_attn:
        in_specs.append(_resident((None, d, d), lambda b, i: (wo_idx, 0, 0)))
        args.append(wo)
    in_specs += [pl.BlockSpec(memory_space=pl.ANY)] * 3
    args += [wg, wu, wd]
    if final_norm:
        in_specs.append(pl.BlockSpec((1, d), lambda b, i: (0, 0)))
        args.append(final_g.reshape(1, d))
    ff_chunk = 256 if d_ff % 256 == 0 else LANES
    n_chunks = d_ff // ff_chunk
    gu_slots = min(4, 2 * n_chunks)
    wd_slots = min(2, n_chunks)
    return pl.pallas_call(
        functools.partial(_ffn_kernel, shapes=shapes, ff_chunk=ff_chunk, pre_attn=pre_attn,
                          final_norm=final_norm, layer=layer, half=half),
        grid=(nseq, s // t),
        in_specs=in_specs,
        out_specs=out_specs,
        out_shape=[jax.ShapeDtypeStruct(x.shape, F32) for x in xs],
        scratch_shapes=[
            pltpu.VMEM((max(nb * rows for nb, rows in shapes), d_ff), BF16),
            pltpu.VMEM((d, d_ff), BF16), pltpu.VMEM((d, d_ff), BF16),
            pltpu.VMEM((d_ff, d), BF16),
            pltpu.VMEM((gu_slots, d, ff_chunk), F32),
            pltpu.VMEM((wd_slots, ff_chunk, d), F32),
            pltpu.SemaphoreType.DMA((gu_slots,)), pltpu.SemaphoreType.DMA((wd_slots,)),
        ],
        compiler_params=_cparams(("arbitrary", "arbitrary")),
        name="ffn",
    )(*args)


def _conv_kernel(x_ref, mod_ref, g_ref, hist_ref, w1_ref, b1_ref, wdw_ref, bdw_ref, lng_ref,
                 lnb_ref, w2_ref, b2_ref, o_ref, st_ref, full_ref, y_ref, shift_ref, *, nb, t,
                 n_tiles, n_parts, row_blk, lane_blk):
    c = w2_ref.shape[0]
    d = x_ref.shape[-1]
    first = HIST_PAD - (CONV_WIDTH - 1)
    shift_rows = shift_ref.shape[2]

    @pl.when(pl.program_id(1) == 0)
    def _():
        full_ref[:, 0:HIST_PAD, :] = hist_ref[...]

    part = t // n_parts

    def glu(p):
        xp = x_ref[:, p * part:(p + 1) * part, :]
        h = _mod_norm(xp, g_ref[...], mod_ref[0], mod_ref[1])
        a = _dot(h.reshape(nb * part, d).astype(BF16), w1_ref[...]) + b1_ref[...]
        u = a[:, :c] * jax.nn.sigmoid(a[:, c:])
        full_ref[:, HIST_PAD + p * part:HIST_PAD + (p + 1) * part, :] = u.reshape(nb, part, c)

    def depthwise(p):
        row0 = p * part
        for c0 in range(0, c, lane_blk):
            lanes = slice(c0, c0 + lane_blk)
            for s in range(1, SUBLANES):
                shift_ref[s - 1] = full_ref[:, row0 + s:row0 + s + shift_rows, lanes]
            for r0 in range(0, part, row_blk):
                acc = jnp.zeros((nb * row_blk // SUBLANES, SUBLANES, lane_blk), F32)
                for w in range(CONV_WIDTH):
                    base, s = divmod(first + w, SUBLANES)
                    lo = base * SUBLANES + r0
                    src = (full_ref[:, row0 + lo:row0 + lo + row_blk, lanes] if s == 0
                           else shift_ref[s - 1, :, lo:lo + row_blk, :])
                    acc = acc + src.reshape(acc.shape) * wdw_ref[w, :, lanes][None]
                y_ref[:, row0 + r0:row0 + r0 + row_blk, lanes] = (
                    acc.reshape(nb, row_blk, lane_blk) + bdw_ref[:, lanes])

    def project(p):
        rows = slice(p * part, (p + 1) * part)
        y = y_ref[:, rows, :].reshape(nb * part, c)
        mu = jnp.mean(y, axis=-1, keepdims=True)
        yc = y - mu
        var = jnp.mean(yc * yc, axis=-1, keepdims=True)
        yn = yc * lax.rsqrt(var + NORM_EPS) * lng_ref[...] + lnb_ref[...]
        out = _dot(jax.nn.silu(yn).astype(BF16), w2_ref[...]) + b2_ref[...]
        o_ref[:, rows, :] = x_ref[:, rows, :] + mod_ref[2] * out.reshape(nb, part, d)

    glu(0)
    for p in range(n_parts):
        if p + 1 < n_parts:
            glu(p + 1)
        depthwise(p)
        project(p)

    st_ref[...] = full_ref[:, t + first:t + HIST_PAD, :]
    if n_tiles > 1:
        full_ref[:, 0:HIST_PAD, :] = full_ref[:, t:t + HIST_PAD, :]


def _conv_mixer(x, mod, norm_g, hist, w1, b1, wdw, bdw, lng, lnb, w2, b2, *, layer, jm, nb, t):
    nseq, s, d = x.shape
    c = w2.shape[1]
    n_tiles = s // t
    acc_vregs = 16
    row_blk = min(t, 64)
    n_parts = 2 if t % (2 * row_blk) == 0 else 1
    part = t // n_parts
    lane_blk = max(LANES, min(c, acc_vregs * SUBLANES * LANES // (nb * row_blk) // LANES * LANES))
    xspec = pl.BlockSpec((nb, t, d), lambda b, i: (b, i, 0))
    const = lambda b, i: (jm, 0, 0)
    vec = lambda a: a.reshape(a.shape[0], 1, a.shape[1])
    hist32 = jnp.pad(hist, ((0, 0), (HIST_PAD - (CONV_WIDTH - 1), 0), (0, 0)))
    return pl.pallas_call(
        functools.partial(_conv_kernel, nb=nb, t=t, n_tiles=n_tiles, n_parts=n_parts,
                          row_blk=row_blk, lane_blk=lane_blk),
        grid=(nseq // nb, n_tiles),
        in_specs=[
            xspec,
            pl.BlockSpec((None, 3, nb, 1, d), lambda b, i: (layer, 1, b, 0, 0)),
            pl.BlockSpec((None, None, 1, d), lambda b, i: (layer, 1, 0, 0)),
            pl.BlockSpec((nb, HIST_PAD, c), lambda b, i: (b, 0, 0)),
            _resident((None, d, 2 * c), const),
            pl.BlockSpec((None, 1, 2 * c), const),
            pl.BlockSpec((None, CONV_WIDTH, SUBLANES, c), lambda b, i: (jm, 0, 0, 0)),
            pl.BlockSpec((None, 1, c), const),
            pl.BlockSpec((None, 1, c), const),
            pl.BlockSpec((None, 1, c), const),
            _resident((None, c, d), const),
            pl.BlockSpec((None, 1, d), const),
        ],
        out_specs=[xspec, pl.BlockSpec((nb, CONV_WIDTH - 1, c), lambda b, i: (b, 0, 0))],
        out_shape=[jax.ShapeDtypeStruct(x.shape, F32),
                   jax.ShapeDtypeStruct((nseq, CONV_WIDTH - 1, c), F32)],
        scratch_shapes=[pltpu.VMEM((nb, HIST_PAD + t, c), F32), pltpu.VMEM((nb, t, c), F32),
                        pltpu.VMEM((SUBLANES - 1, nb, HIST_PAD - SUBLANES + part, lane_blk), F32)],
        compiler_params=_cparams(("arbitrary", "arbitrary")),
        name="conv_mixer",
    )(x, mod, norm_g, hist32, w1, vec(b1),
      jnp.broadcast_to(wdw[:, :, None, :], wdw.shape[:2] + (SUBLANES, c)),
      vec(bdw), vec(lng), vec(lnb), w2, vec(b2))


def _qkv_kernel(x_ref, mod_ref, g_ref, w_ref, q_ref, k_ref, v_ref, *, nb, t, q_scale):
    x = x_ref[...]
    d = x.shape[-1]
    h = _mod_norm(x, g_ref[...], mod_ref[0], mod_ref[1])
    hb = h.reshape(nb * t, d).astype(BF16)
    q = _dot(hb, w_ref[:, 0:d]) * q_scale
    q_ref[...] = q.reshape(nb, t, d).astype(q_ref.dtype)
    k_ref[...] = _dot(hb, w_ref[:, d:2 * d]).reshape(nb, t, d)
    v_ref[...] = _dot(hb, w_ref[:, 2 * d:3 * d]).reshape(nb, t, d)


def _qkv(x, mod, norm_g, w, *, layer, jm, nb, t, q_dtype, q_scale):
    nseq, s, d = x.shape
    xspec = pl.BlockSpec((nb, t, d), lambda b, i: (b, i, 0))
    return pl.pallas_call(
        functools.partial(_qkv_kernel, nb=nb, t=t, q_scale=q_scale),
        grid=(nseq // nb, s // t),
        in_specs=[
            xspec,
            pl.BlockSpec((None, 3, nb, 1, d), lambda b, i: (layer, 1, b, 0, 0)),
            pl.BlockSpec((None, None, 1, d), lambda b, i: (layer, 1, 0, 0)),
            _resident((None, d, 3 * d), lambda b, i: (jm, 0, 0)),
        ],
        out_specs=[xspec, xspec, xspec],
        out_shape=[jax.ShapeDtypeStruct(x.shape, q_dtype), jax.ShapeDtypeStruct(x.shape, F32),
                   jax.ShapeDtypeStruct(x.shape, F32)],
        compiler_params=_cparams(("arbitrary", "arbitrary")),
        name="qkv",
    )(x, mod, norm_g, w)


def _bucket_of_distance(n_dist):
    dist = np.arange(n_dist)
    max_exact = NUM_BUCKETS // 2
    nf = np.maximum(dist, max_exact).astype(np.float32)
    large = max_exact + (np.log(nf / max_exact) / math.log(MAX_DISTANCE / max_exact)
                         * (NUM_BUCKETS - max_exact)).astype(np.int32)
    large = np.minimum(large, NUM_BUCKETS - 1)
    return np.where(dist < max_exact, dist, large).astype(np.int32)


_BUCKETS = _bucket_of_distance(4 * MAX_DISTANCE)
assert _BUCKETS[-1] == NUM_BUCKETS - 1 and np.all(np.diff(_BUCKETS) >= 0)
_BUCKET_START = [int(np.argmax(_BUCKETS >= b)) for b in range(NUM_BUCKETS)]
assert _BUCKET_START[NUM_BUCKETS - 1] <= MOBA_BLOCK


def _bias_of_distance(dist, table):
    bias = jnp.full(dist.shape, table[0], F32)
    for b in range(1, NUM_BUCKETS):
        bias = jnp.where(dist >= _BUCKET_START[b], table[b], bias)
    return bias


def _rank_select(gate, n_valid, first):
    idx = lax.broadcasted_iota(jnp.int32, gate.shape, 1)
    cnt = jnp.zeros(gate.shape, F32)
    for j in range(first, first + n_valid):
        gj = gate[:, j:j + 1]
        tie = jnp.where(idx > j, 1.0, 0.0)
        cnt = cnt + jnp.where(gj > gate, 1.0, jnp.where(gj == gate, tie, 0.0))
    return cnt < float(MOBA_TOP_K)


def _top_k_lanes(gate, valid):
    lane_f = lax.broadcasted_iota(jnp.int32, gate.shape, 1).astype(F32)
    g = jnp.where(valid, gate, -jnp.inf)
    picks = []
    for _ in range(MOBA_TOP_K):
        top = jnp.max(g, axis=1, keepdims=True)
        pick = jnp.min(jnp.where(g == top, lane_f, float(LANES)), axis=1, keepdims=True)
        picks.append(pick)
        g = jnp.where(lane_f == pick, -jnp.inf, g)
    return picks


def _is_picked(picks, n):
    hit = jnp.where(picks[0] == n, 1.0, 0.0)
    for pick in picks[1:]:
        hit = jnp.where(pick == n, 1.0, hit)
    return hit > 0.0


def _moba_prompt_kernel(rb_ref, q_ref, k_ref, v_ref, o_ref, bias_ref, *, n_blk, n_heads):
    pair = pl.program_id(0)
    blk = MOBA_BLOCK

    @pl.when(pl.program_id(1) == 0)
    def _():
        dist = (lax.broadcasted_iota(jnp.int32, (blk, blk), 1)
                - lax.broadcasted_iota(jnp.int32, (blk, blk), 0))
        for hh in range(2):
            table = [rb_ref[b * n_heads + 2 * pair + hh] for b in range(NUM_BUCKETS)]
            far = table[NUM_BUCKETS - 1]
            own = (_bias_of_distance(dist, table) - far) * LOG2E
            bias_ref[hh, 0] = jnp.where(dist >= 0, own, -jnp.inf)
            bias_ref[hh, 1] = (_bias_of_distance(dist + blk, table) - far) * LOG2E

    kf = k_ref[...]
    vf = v_ref[...]
    s_len = kf.shape[0]
    lane = lax.broadcasted_iota(jnp.int32, (1, LANES), 1)
    key_lane = lax.broadcasted_iota(jnp.int32, (s_len, LANES), 1)
    key_blk = lax.broadcasted_iota(jnp.int32, (s_len, LANES), 0) // blk
    km_row = lax.broadcasted_iota(jnp.int32, (LANES, LANES), 0)

    heads = []
    for hh in range(2):
        off = (1 - hh) * HEAD_DIM
        hmask = (lane // HEAD_DIM) == hh
        k_aug = jnp.where(hmask, kf, jnp.where(key_lane == key_blk + off, 1.0, 0.0)).astype(BF16)
        v_aug = jnp.transpose(jnp.where(hmask, vf, 1.0)).astype(BF16)
        km = jnp.zeros((LANES, LANES), F32)
        for j in range(n_blk):
            km = jnp.where(km_row == off + j,
                           jnp.mean(kf[j * blk:(j + 1) * blk, :], axis=0, keepdims=True), km)
        heads.append((hh, off, hmask, k_aug, v_aug, km.astype(BF16)))

    def logits(i, head):
        hh, off, hmask, k_aug, _, km = head
        qi = q_ref[i * blk:(i + 1) * blk, :]
        if i > MOBA_TOP_K:
            gate = _dot_nt(jnp.where(hmask, qi, jnp.zeros_like(qi)), km)
            keep = _rank_select(gate, i, first=off)
            past = jnp.logical_and(lane >= off, lane < off + i)
            drop = jnp.where(past, jnp.where(keep, 0.0, MASKED), 0.0)
            q_aug = jnp.where(hmask, qi, drop.astype(BF16))
        else:
            q_aug = jnp.where(hmask, qi, jnp.zeros_like(qi))
        for j in range(i + 1):
            s = _dot_nt(k_aug[j * blk:(j + 1) * blk, :], q_aug)
            if j == i:
                s = s + bias_ref[hh, 0]
            elif j == i - 1:
                s = s + bias_ref[hh, 1]
            yield s

    def weighted_values(s_list, head):
        v_aug = head[4]
        m = jnp.max(functools.reduce(jnp.maximum, s_list), axis=0, keepdims=True)
        acc = jnp.zeros((LANES, blk), F32)
        for j, s in enumerate(s_list):
            acc = acc + _dot(v_aug[:, j * blk:(j + 1) * blk], jnp.exp2(s - m).astype(BF16))
        return acc

    units = [(i, head) for i in range(n_blk) for head in heads]
    low = lax.broadcasted_iota(jnp.int32, (LANES, 1), 0) < HEAD_DIM
    ahead = 2
    pending = [list(logits(*unit)) for unit in units[:ahead]]
    accs = []
    for u, (i, head) in enumerate(units):
        if u + ahead < len(units):
            pending.append(list(logits(*units[u + ahead])))
        accs.append(weighted_values(pending.pop(0), head))
        if len(accs) == 2:
            out_t = jnp.where(low, accs[0] * (1.0 / accs[0][HEAD_DIM:HEAD_DIM + 1, :]),
                              accs[1] * (1.0 / accs[1][0:1, :]))
            o_ref[i * blk:(i + 1) * blk, :] = jnp.transpose(out_t).astype(o_ref.dtype)
            accs = []


def _moba_prompt(q, k, v, rel_bias):
    b, s, d = q.shape
    n_heads = d // HEAD_DIM
    blk = MOBA_BLOCK
    n_blk = s // blk
    assert s % blk == 0 and n_heads % 2 == 0 and 2 * HEAD_DIM == LANES
    hspec = pl.BlockSpec((None, s, LANES), lambda p, bi, rb_ref: (bi, 0, p))
    return pl.pallas_call(
        functools.partial(_moba_prompt_kernel, n_blk=n_blk, n_heads=n_heads),
        grid_spec=pltpu.PrefetchScalarGridSpec(
            num_scalar_prefetch=1,
            grid=(n_heads // 2, b),
            in_specs=[hspec, hspec, hspec],
            out_specs=hspec,
            scratch_shapes=[pltpu.VMEM((2, 2, blk, blk), F32)],
        ),
        out_shape=jax.ShapeDtypeStruct((b, s, d), BF16),
        compiler_params=_cparams(("arbitrary", "arbitrary")),
        name="moba_prompt",
    )(rel_bias.reshape(-1), q, k, v)


def _moba_sample_kernel(pt_ref, rb_ref, *refs, n_groups, n_blk, n_heads, t):
    pp = PAGES_PER_STEP
    (ck_hbm, cv_hbm, q_ref, kn_ref, vn_ref, o_ref, ring_ref, sem, a_ref, s_ref, p_ref, pown_ref,
     l_ref, acc_ref, blast_ref, bown_ref, gate_ref, bmax_ref) = refs
    seq = pl.program_id(0)
    step = pl.program_id(1)
    blk = MOBA_BLOCK
    ppb = blk // PAGE_SIZE
    bps = pp // ppb
    d = q_ref.shape[-1]
    cols = n_heads * t
    row_head = lax.broadcasted_iota(jnp.int32, (cols, d), 0) // t
    lane_head = lax.broadcasted_iota(jnp.int32, (cols, d), 1) // HEAD_DIM
    diag = row_head == lane_head
    lane = lax.broadcasted_iota(jnp.int32, (cols, LANES), 1)

    steps = 2 * n_groups
    n_pages = n_groups * pp
    group = seq * steps + step
    n_stream = pl.num_programs(0) * steps

    def group_copies(m, src_hbm, first_page):
        slot = m % RING_GROUPS
        return [pltpu.make_async_copy(
            src_hbm.at[pt_ref[(m // steps) * n_pages + first_page + r]],
            ring_ref.at[slot, r], sem.at[slot]) for r in range(pp)]

    def start_group(m):
        st = m % steps

        @pl.when(st < n_groups)
        def _():
            for copy in group_copies(m, ck_hbm, st * pp):
                copy.start()

        @pl.when(st >= n_groups)
        def _():
            for copy in group_copies(m, cv_hbm, (st - n_groups) * pp):
                copy.start()

    @pl.when(group == 0)
    def _():
        for m in range(RING_GROUPS - 1):
            start_group(jnp.int32(m))

    @pl.when(group + RING_GROUPS - 1 < n_stream)
    def _():
        start_group(group + RING_GROUPS - 1)

    for copy in group_copies(group, ck_hbm, 0):
        copy.wait()
    pages_ref = ring_ref.at[group % RING_GROUPS]

    def new_rows(ref):
        return jnp.concatenate([ref[...], jnp.zeros((PAGE_SIZE - t, ref.shape[-1]), F32)],
                               axis=0).astype(BF16)

    def block_of(r):
        return jnp.concatenate([pages_ref[ppb * r + i] for i in range(ppb)], axis=1).astype(BF16)

    @pl.when(jnp.logical_and(seq == 0, step == 0))
    def _():
        qi = lax.broadcasted_iota(jnp.int32, (t, blk), 0)
        ki = lax.broadcasted_iota(jnp.int32, (t, blk), 1)
        qo = lax.broadcasted_iota(jnp.int32, (t, PAGE_SIZE), 0)
        ko = lax.broadcasted_iota(jnp.int32, (t, PAGE_SIZE), 1)
        for h in range(n_heads):
            table = [rb_ref[b * n_heads + h] for b in range(NUM_BUCKETS)]
            far = table[NUM_BUCKETS - 1]
            rows = slice(h * t, (h + 1) * t)
            blast_ref[rows, :] = _bias_of_distance(blk + qi - ki, table) - far
            bown_ref[rows, :] = jnp.where(ko <= qo, _bias_of_distance(qo - ko, table) - far,
                                          -jnp.inf)

    @pl.when(step == 0)
    def _():
        qt = jnp.tile(q_ref[...], (n_heads, 1))
        a_ref[...] = jnp.where(diag, qt, 0.0).astype(BF16)
        acc_ref[...] = jnp.zeros_like(acc_ref)
        gate_ref[...] = jnp.full_like(gate_ref, -jnp.inf)
        bmax_ref[...] = jnp.full_like(bmax_ref, -jnp.inf)

    @pl.when(step < n_groups)
    def _():
        a = a_ref[...]
        gate = gate_ref[...]
        bmax = bmax_ref[...]
        for r in range(bps):
            n = step * bps + r
            s = _dot(a, block_of(r))
            s_ref[n] = s
            gate = jnp.where(lane == n, jnp.sum(s, axis=1, keepdims=True), gate)
            bmax = jnp.where(lane == n, jnp.max(s, axis=1, keepdims=True), bmax)
        gate_ref[...] = gate
        bmax_ref[...] = bmax

    @pl.when(step == n_groups)
    def _():
        picks = _top_k_lanes(gate_ref[...], lane < n_blk)
        picked = functools.partial(_is_picked, picks)
        lane_f = lane.astype(F32)
        last = n_blk - 1
        s_own = _dot_nt(a_ref[...], new_rows(kn_ref)) + bown_ref[...]
        bmax = jnp.where(lane == last,
                         jnp.max(s_ref[last] + blast_ref[...], axis=1, keepdims=True), bmax_ref[...])
        m = jnp.maximum(jnp.max(s_own, axis=1, keepdims=True),
                        jnp.max(jnp.where(picked(lane_f), bmax, -jnp.inf), axis=1, keepdims=True))
        p_own = jnp.exp(s_own - m)
        lsum = jnp.zeros((cols, blk), F32)
        for n in range(n_blk):
            sn = s_ref[n] + blast_ref[...] if n == last else s_ref[n]
            p = jnp.where(picked(float(n)), jnp.exp(sn - m), 0.0)
            lsum = lsum + p
            p_ref[n] = p.astype(BF16)
        l = jnp.sum(lsum, axis=1, keepdims=True) + jnp.sum(p_own, axis=1, keepdims=True)
        pown_ref[...] = p_own.astype(BF16)
        l_ref[...] = jnp.broadcast_to(l, (cols, LANES))

    @pl.when(step >= n_groups)
    def _():
        acc = acc_ref[...]
        for r in range(bps):
            acc = acc + _dot_nt(p_ref[(step - n_groups) * bps + r], block_of(r))
        acc_ref[...] = acc

    @pl.when(step == 2 * n_groups - 1)
    def _():
        acc = acc_ref[...] + _dot(pown_ref[...], new_rows(vn_ref))
        acc = acc * jnp.tile(1.0 / l_ref[...], (1, d // LANES))
        acc = jnp.where(diag, acc, 0.0)
        o_ref[...] = jnp.sum(acc.reshape(n_heads, t, d), axis=0).astype(o_ref.dtype)


def _moba_sample(q, k_new, v_new, cache_k, cache_v, page_table, rel_bias):
    nseq, t, d = q.shape
    n_heads = d // HEAD_DIM
    n_pages = page_table.shape[1]
    past = n_pages * PAGE_SIZE
    blk = MOBA_BLOCK
    n_blk = past // blk
    pp = PAGES_PER_STEP
    cols = n_heads * t
    assert past % blk == 0 and blk % PAGE_SIZE == 0 and pp % (blk // PAGE_SIZE) == 0
    assert n_pages % pp == 0 and cols % SUBLANES == 0 and t % SUBLANES == 0 and t <= PAGE_SIZE
    assert MOBA_TOP_K <= n_blk <= LANES
    g = n_pages // pp
    assert nseq * 2 * g >= RING_GROUPS

    seq = pl.BlockSpec((None, t, d), lambda b, s, pt, rb: (b, 0, 0))
    hbm = pl.BlockSpec(memory_space=pl.ANY)
    in_specs = [hbm, hbm, seq, seq, seq]
    return pl.pallas_call(
        functools.partial(_moba_sample_kernel, n_groups=g, n_blk=n_blk, n_heads=n_heads, t=t),
        grid_spec=pltpu.PrefetchScalarGridSpec(
            num_scalar_prefetch=2,
            grid=(nseq, 2 * g),
            in_specs=in_specs,
            out_specs=seq,
            scratch_shapes=[
                pltpu.VMEM((RING_GROUPS, pp, d, PAGE_SIZE), F32),
                pltpu.SemaphoreType.DMA((RING_GROUPS,)),
                pltpu.VMEM((cols, d), BF16),
                pltpu.VMEM((n_blk, cols, blk), F32),
                pltpu.VMEM((n_blk, cols, blk), BF16),
                pltpu.VMEM((cols, PAGE_SIZE), BF16),
                pltpu.VMEM((cols, LANES), F32),
                pltpu.VMEM((cols, d), F32),
                pltpu.VMEM((cols, blk), F32),
                pltpu.VMEM((cols, PAGE_SIZE), F32),
                pltpu.VMEM((cols, LANES), F32),
                pltpu.VMEM((cols, LANES), F32),
            ],
        ),
        out_shape=jax.ShapeDtypeStruct((nseq, t, d), BF16),
        compiler_params=_cparams(("arbitrary", "arbitrary")),
        name="moba_sample",
    )(page_table.reshape(-1), rel_bias.reshape(-1), cache_k, cache_v, q, k_new, v_new)


def _tile_rows(s):
    for t in (512, 256, 128, 64, 32, 16, 8):
        if s % t == 0:
            return t
    raise ValueError(f"sequence length {s} is not a multiple of {SUBLANES}")


def _trunk(xs, mods, hists_in, past, page_table, w, *, t):
    d = xs[0].shape[-1]
    n_heads = d // HEAD_DIM
    depth = w["norm_g"].shape[0]
    tiles = [(1, t), (SUBLANES, xs[1].shape[1])]
    qkv_tiles = [(1, t), xs[1].shape[:2]]
    hists, new_k, new_v = [[], []], [[], []], [[], []]
    for layer in range(depth):
        jm = layer // 2
        ffn = functools.partial(_ffn, mods=mods, norm_g=w["norm_g"], wg=w["wg"], wu=w["wu"],
                                wd=w["wd"], layer=layer, t=t)
        xs = list(ffn(xs, sub=0))
        attns = None
        if layer % 2 == 0:
            for k in range(2):
                xs[k], st = _conv_mixer(xs[k], mods[k], w["norm_g"], hists_in[k][jm], w["pw1"],
                                        w["b_pw1"], w["dw"], w["b_dw"], w["ln_g"], w["ln_b"],
                                        w["pw2"], w["b_pw2"], layer=layer, jm=jm,
                                        nb=tiles[k][0], t=tiles[k][1])
                hists[k].append(st)
        else:
            scale = 1.0 / math.sqrt(HEAD_DIM)
            attns = []
            for k in range(2):
                nseq, s, _ = xs[k].shape
                q, kk, vv = _qkv(xs[k], mods[k], w["norm_g"], w["qkv"], layer=layer, jm=jm,
                                 nb=qkv_tiles[k][0], t=qkv_tiles[k][1],
                                 q_dtype=BF16 if k == 0 else F32,
                                 q_scale=scale * LOG2E if k == 0 else scale)
                if k == 0:
                    attns.append(_moba_prompt(q, kk, vv, w["rel_bias"]))
                else:
                    ck, cv, n_phys = past
                    attns.append(_moba_sample(q, kk, vv, ck, cv, page_table + jm * n_phys,
                                              w["rel_bias"]))
                new_k[k].append(kk.reshape(nseq, s, n_heads, HEAD_DIM))
                new_v[k].append(vv.reshape(nseq, s, n_heads, HEAD_DIM))
        last = layer == depth - 1
        xs = ffn(xs, sub=2, attns=attns, wo=w["wo"] if attns is not None else None, wo_idx=jm,
                 final_g=w["final_g"] if last else None)
    return xs, [jnp.stack(h) for h in hists], [jnp.stack(a) for a in new_k], \
        [jnp.stack(a) for a in new_v]


def kernel(x_prompt, x_sample, state_conv, cache_k, cache_v, page_table, c_prompt, c_sample, norm_g, ada_w, ada_b, ffn_w_gate, ffn_w_up, ffn_w_down, conv_w_pw1, conv_b_pw1, conv_w_dw, conv_b_dw, conv_ln_g, conv_ln_b, conv_w_pw2, conv_b_pw2, attn_w_qkv, attn_w_o, rel_bias, final_norm_g):
    b, s, d = x_prompt.shape
    depth = norm_g.shape[0]
    w = dict(
        norm_g=norm_g.reshape(depth, 3, 1, d),
        wg=ffn_w_gate, wu=ffn_w_up, wd=ffn_w_down,
        pw1=conv_w_pw1.astype(BF16), b_pw1=conv_b_pw1, dw=conv_w_dw, b_dw=conv_b_dw,
        ln_g=conv_ln_g, ln_b=conv_ln_b, pw2=conv_w_pw2.astype(BF16), b_pw2=conv_b_pw2,
        qkv=attn_w_qkv.astype(BF16), wo=attn_w_o.astype(BF16), rel_bias=rel_bias,
        final_g=final_norm_g,
    )
    mod_p, mod_s = _ada_mod(c_prompt, c_sample, ada_w, ada_b)

    n_layers, n_phys = cache_k.shape[:2]
    pages = lambda c: jnp.transpose(c, (0, 1, 3, 4, 2)).reshape(n_layers * n_phys, d, PAGE_SIZE)

    hist0 = jnp.zeros((conv_w_dw.shape[0], b, CONV_WIDTH - 1, conv_w_dw.shape[2]), F32)
    ys, states, ks, vs = _trunk([x_prompt, x_sample], [mod_p, mod_s], [hist0, state_conv],
                                (pages(cache_k), pages(cache_v), n_phys), page_table, w,
                                t=_tile_rows(s))
    return (ys[0], ys[1], states[0], states[1], ks[0], vs[0], ks[1], vs[1])
```

```python
import functools
import math

import numpy as np
import jax
import jax.numpy as jnp
from jax import lax
from jax.experimental import pallas as pl
from jax.experimental.pallas import tpu as pltpu

F32 = jnp.float32
BF16 = jnp.bfloat16

HEAD_DIM = 64
CONV_WIDTH = 31
MOBA_BLOCK = 256
MOBA_TOP_K = 3
PAGE_SIZE = 128
NUM_BUCKETS = 32
MAX_DISTANCE = 128
NORM_EPS = 1e-6

LANES = 128
SUBLANES = 8
HIST_PAD = 32
VMEM_LIMIT = 56 * 1024 * 1024
PAGES_PER_STEP = 16
RING_GROUPS = 3
PROMPT_KEY_TILE = 256
LOG2E = math.log2(math.e)
MASKED = -1e30


def _cparams(sem):
    return pltpu.CompilerParams(dimension_semantics=sem, vmem_limit_bytes=VMEM_LIMIT)


def _resident(block_shape, index_map):
    return pl.BlockSpec(block_shape, index_map, pipeline_mode=pl.Buffered(1))


def _mod_norm(x, g, shift, scale):
    ms = jnp.mean(x * x, axis=-1, keepdims=True)
    n = x * lax.rsqrt(ms + NORM_EPS)
    return n * g * (1.0 + scale) + shift


def _dot(a, b):
    return jnp.dot(a, b, preferred_element_type=F32)


def _dot_nt(a, b):
    return lax.dot_general(a, b, (((1,), (1,)), ((), ())), preferred_element_type=F32)


def _dot_tn(a, b):
    return lax.dot_general(a, b, (((0,), (0,)), ((), ())), preferred_element_type=F32)


def _mod_kernel(cp_ref, cs_ref, w_ref, b_ref, op_ref, os_ref):
    d = cp_ref.shape[1]
    scp = jax.nn.silu(cp_ref[...]).astype(BF16)
    scs = jax.nn.silu(cs_ref[...]).astype(BF16)
    for m in range(op_ref.shape[0]):
        w = w_ref[:, m * d:(m + 1) * d].astype(BF16)
        op_ref[m] = _dot(scp, w) + b_ref[m]
        os_ref[m] = _dot(scs, w) + b_ref[m]


def _ada_mod(c_prompt, c_sample, ada_w, ada_b):
    d = c_prompt.shape[1]
    depth = ada_w.shape[0]
    n_mod = ada_w.shape[2] // d
    per_step = 3
    assert n_mod % per_step == 0
    outs = pl.pallas_call(
        _mod_kernel,
        grid=(depth, n_mod // per_step),
        in_specs=[
            pl.BlockSpec(c_prompt.shape, lambda l, j: (0, 0)),
            pl.BlockSpec(c_sample.shape, lambda l, j: (0, 0)),
            pl.BlockSpec((None, d, per_step * d), lambda l, j: (l, 0, j)),
            pl.BlockSpec((None, per_step, 1, d), lambda l, j: (l, j, 0, 0)),
        ],
        out_specs=[pl.BlockSpec((None, per_step) + c.shape, lambda l, j: (l, j, 0, 0))
                   for c in (c_prompt, c_sample)],
        out_shape=[jax.ShapeDtypeStruct((depth, n_mod) + c.shape, F32)
                   for c in (c_prompt, c_sample)],
        compiler_params=_cparams(("arbitrary", "arbitrary")),
        name="ada_mod",
    )(c_prompt, c_sample, ada_w, ada_b.reshape(depth, n_mod, 1, d))
    return [o.reshape(depth, n_mod, o.shape[2], 1, d) for o in outs]


def _ffn_kernel(*refs, shapes, ff_chunk, pre_attn, final_norm, layer, half):
    it = iter(refs)
    groups = []
    for _ in shapes:
        x_ref, mod_ref = next(it), next(it)
        a_ref, gate1_ref = (next(it), next(it)) if pre_attn else (None, None)
        groups.append((x_ref, mod_ref, a_ref, gate1_ref))
    g_ref = next(it)
    wo_ref = next(it) if pre_attn else None
    wg_hbm, wu_hbm, wd_hbm = next(it), next(it), next(it)
    fg_ref = next(it) if final_norm else None
    o_refs = [next(it) for _ in shapes]
    act_ref, wg_ref, wu_ref, wd_ref, gu_stage, wd_stage, gu_sem, wd_sem = (next(it) for _ in range(8))
    d_ff = wg_ref.shape[-1]
    n_chunks = d_ff // ff_chunk
    gu_slots, wd_slots = gu_stage.shape[0], wd_stage.shape[0]

    def gu_copy(c):
        cols = slice((c // 2) * ff_chunk, (c // 2 + 1) * ff_chunk)
        src = (wg_hbm if c % 2 == 0 else wu_hbm).at[layer, half, :, cols]
        return pltpu.make_async_copy(src, gu_stage.at[c % gu_slots], gu_sem.at[c % gu_slots])

    def wd_copy(j):
        src = wd_hbm.at[layer, half, j * ff_chunk:(j + 1) * ff_chunk, :]
        return pltpu.make_async_copy(src, wd_stage.at[j % wd_slots], wd_sem.at[j % wd_slots])

    def stage_chunk(j):
        cols = slice(j * ff_chunk, (j + 1) * ff_chunk)
        for c, dst in ((2 * j, wg_ref), (2 * j + 1, wu_ref)):
            gu_copy(c).wait()
            dst[:, cols] = gu_stage[c % gu_slots].astype(BF16)
            if c + gu_slots < 2 * n_chunks:
                gu_copy(c + gu_slots).start()
        wd_copy(j).wait()
        wd_ref[cols, :] = wd_stage[j % wd_slots].astype(BF16)
        if j + wd_slots < n_chunks:
            wd_copy(j + wd_slots).start()

    def run(k, stage):
        nb, t = shapes[k]
        x_ref, mod_ref, a_ref, gate1_ref = groups[k]
        if stage:
            for c in range(gu_slots):
                gu_copy(c).start()
            for j in range(wd_slots):
                wd_copy(j).start()
        x = x_ref[...]
        d = x.shape[-1]
        if pre_attn:
            y = _dot(a_ref[...].reshape(nb * t, d), wo_ref[...])
            x = x + gate1_ref[0] * y.reshape(nb, t, d)
        h = _mod_norm(x, g_ref[...], mod_ref[0], mod_ref[1])
        hb = h.reshape(nb * t, d).astype(BF16)
        for j in range(n_chunks):
            if stage:
                stage_chunk(j)
            sl = slice(j * ff_chunk, (j + 1) * ff_chunk)
            gj = _dot(hb, wg_ref[:, sl])
            uj = _dot(hb, wu_ref[:, sl])
            act_ref[0:nb * t, sl] = (jax.nn.silu(gj) * uj).astype(BF16)
        y = _dot(act_ref[0:nb * t, :], wd_ref[...])
        out = x + 0.5 * mod_ref[2] * y.reshape(nb, t, d)
        if final_norm:
            ms = jnp.mean(out * out, axis=-1, keepdims=True)
            out = out * lax.rsqrt(ms + NORM_EPS) * fg_ref[...]
        o_refs[k][...] = out

    first_step = jnp.logical_and(pl.program_id(0) == 0, pl.program_id(1) == 0)

    @pl.when(first_step)
    def _():
        run(0, stage=True)
        for k in range(1, len(shapes)):
            run(k, stage=False)

    @pl.when(jnp.logical_not(first_step))
    def _():
        run(0, stage=False)


def _ffn(xs, mods, norm_g, wg, wu, wd, *, layer, sub, t, attns=None, wo=None, wo_idx=0,
         final_g=None):
    nseq, s, d = xs[0].shape
    d_ff = wg.shape[-1]
    half = sub // 2
    pre_attn = attns is not None
    final_norm = final_g is not None
    shapes = [(1, t)] + [x.shape[:2] for x in xs[1:]]
    tile = lambda b, i: (b, i, 0)
    whole = lambda b, i: (0, 0, 0)
    in_specs, args, out_specs = [], [], []
    for k, (x, mod) in enumerate(zip(xs, mods)):
        nb, rows = shapes[k]
        seq0 = (lambda b: b) if k == 0 else (lambda b: 0)
        xspec = pl.BlockSpec((nb, rows, d), tile if k == 0 else whole)
        in_specs += [xspec, pl.BlockSpec((None, 3, nb, 1, d),
                                         lambda b, i, seq0=seq0: (layer, sub, seq0(b), 0, 0))]
        args += [x, mod]
        if pre_attn:
            gate1 = 3 * (sub - 1) + 2
            in_specs += [xspec, pl.BlockSpec((None, 1, nb, 1, d),
                                             lambda b, i, seq0=seq0: (layer, gate1, seq0(b), 0, 0))]
            args += [attns[k], mod]
        out_specs.append(xspec)
    in_specs.append(pl.BlockSpec((None, None, 1, d), lambda b, i: (layer, sub, 0, 0)))
    args.append(norm_g)
    if pre_attn:
        in_specs.append(_resident((None, d, d), lambda b, i: (wo_idx, 0, 0)))
        args.append(wo)
    in_specs += [pl.BlockSpec(memory_space=pl.ANY)] * 3
    args += [wg, wu, wd]
    if final_norm:
        in_specs.append(pl.BlockSpec((1, d), lambda b, i: (0, 0)))
        args.append(final_g.reshape(1, d))
    ff_chunk = 256 if d_ff % 256 == 0 else LANES
    n_chunks = d_ff // ff_chunk
    gu_slots = min(4, 2 * n_chunks)
    wd_slots = min(2, n_chunks)
    return pl.pallas_call(
        functools.partial(_ffn_kernel, shapes=shapes, ff_chunk=ff_chunk, pre_attn=pre_attn,
                          final_norm=final_norm, layer=layer, half=half),
        grid=(nseq, s // t),
        in_specs=in_specs,
        out_specs=out_specs,
        out_shape=[jax.ShapeDtypeStruct(x.shape, F32) for x in xs],
        scratch_shapes=[
            pltpu.VMEM((max(nb * rows for nb, rows in shapes), d_ff), BF16),
            pltpu.VMEM((d, d_ff), BF16), pltpu.VMEM((d, d_ff), BF16),
            pltpu.VMEM((d_ff, d), BF16),
            pltpu.VMEM((gu_slots, d, ff_chunk), F32),
            pltpu.VMEM((wd_slots, ff_chunk, d), F32),
            pltpu.SemaphoreType.DMA((gu_slots,)), pltpu.SemaphoreType.DMA((wd_slots,)),
        ],
        compiler_params=_cparams(("arbitrary", "arbitrary")),
        name="ffn",
    )(*args)


def _conv_kernel(x_ref, mod_ref, g_ref, hist_ref, w1_ref, b1_ref, wdw_ref, bdw_ref, lng_ref,
                 lnb_ref, w2_ref, b2_ref, o_ref, st_ref, full_ref, y_ref, shift_ref, *, nb, t,
                 n_tiles, n_parts, row_blk, lane_blk):
    c = w2_ref.shape[0]
    d = x_ref.shape[-1]
    first = HIST_PAD - (CONV_WIDTH - 1)
    shift_rows = shift_ref.shape[2]

    @pl.when(pl.program_id(1) == 0)
    def _():
        full_ref[:, 0:HIST_PAD, :] = hist_ref[...]

    part = t // n_parts

    def glu(p):
        xp = x_ref[:, p * part:(p + 1) * part, :]
        h = _mod_norm(xp, g_ref[...], mod_ref[0], mod_ref[1])
        a = _dot(h.reshape(nb * part, d).astype(BF16), w1_ref[...]) + b1_ref[...]
        u = a[:, :c] * jax.nn.sigmoid(a[:, c:])
        full_ref[:, HIST_PAD + p * part:HIST_PAD + (p + 1) * part, :] = u.reshape(nb, part, c)

    def depthwise(p):
        row0 = p * part
        for c0 in range(0, c, lane_blk):
            lanes = slice(c0, c0 + lane_blk)
            for s in range(1, SUBLANES):
                shift_ref[s - 1] = full_ref[:, row0 + s:row0 + s + shift_rows, lanes]
            for r0 in range(0, part, row_blk):
                acc = jnp.zeros((nb * row_blk // SUBLANES, SUBLANES, lane_blk), F32)
                for w in range(CONV_WIDTH):
                    base, s = divmod(first + w, SUBLANES)
                    lo = base * SUBLANES + r0
                    src = (full_ref[:, row0 + lo:row0 + lo + row_blk, lanes] if s == 0
                           else shift_ref[s - 1, :, lo:lo + row_blk, :])
                    acc = acc + src.reshape(acc.shape) * wdw_ref[w, :, lanes][None]
                y_ref[:, row0 + r0:row0 + r0 + row_blk, lanes] = (
                    acc.reshape(nb, row_blk, lane_blk) + bdw_ref[:, lanes])

    def project(p):
        rows = slice(p * part, (p + 1) * part)
        y = y_ref[:, rows, :].reshape(nb * part, c)
        mu = jnp.mean(y, axis=-1, keepdims=True)
        yc = y - mu
        var = jnp.mean(yc * yc, axis=-1, keepdims=True)
        yn = yc * lax.rsqrt(var + NORM_EPS) * lng_ref[...] + lnb_ref[...]
        out = _dot(jax.nn.silu(yn).astype(BF16), w2_ref[...]) + b2_ref[...]
        o_ref[:, rows, :] = x_ref[:, rows, :] + mod_ref[2] * out.reshape(nb, part, d)

    glu(0)
    for p in range(n_parts):
        if p + 1 < n_parts:
            glu(p + 1)
        depthwise(p)
        project(p)

    st_ref[...] = full_ref[:, t + first:t + HIST_PAD, :]
    if n_tiles > 1:
        full_ref[:, 0:HIST_PAD, :] = full_ref[:, t:t + HIST_PAD, :]


def _conv_mixer(x, mod, norm_g, hist, w1, b1, wdw, bdw, lng, lnb, w2, b2, *, layer, jm, nb, t):
    nseq, s, d = x.shape
    c = w2.shape[1]
    n_tiles = s // t
    acc_vregs = 16
    row_blk = min(t, 64)
    n_parts = 2 if t % (2 * row_blk) == 0 else 1
    part = t // n_parts
    lane_blk = max(LANES, min(c, acc_vregs * SUBLANES * LANES // (nb * row_blk) // LANES * LANES))
    xspec = pl.BlockSpec((nb, t, d), lambda b, i: (b, i, 0))
    const = lambda b, i: (jm, 0, 0)
    vec = lambda a: a.reshape(a.shape[0], 1, a.shape[1])
    hist32 = jnp.pad(hist, ((0, 0), (HIST_PAD - (CONV_WIDTH - 1), 0), (0, 0)))
    return pl.pallas_call(
        functools.partial(_conv_kernel, nb=nb, t=t, n_tiles=n_tiles, n_parts=n_parts,
                          row_blk=row_blk, lane_blk=lane_blk),
        grid=(nseq // nb, n_tiles),
        in_specs=[
            xspec,
            pl.BlockSpec((None, 3, nb, 1, d), lambda b, i: (layer, 1, b, 0, 0)),
            pl.BlockSpec((None, None, 1, d), lambda b, i: (layer, 1, 0, 0)),
            pl.BlockSpec((nb, HIST_PAD, c), lambda b, i: (b, 0, 0)),
            _resident((None, d, 2 * c), const),
            pl.BlockSpec((None, 1, 2 * c), const),
            pl.BlockSpec((None, CONV_WIDTH, SUBLANES, c), lambda b, i: (jm, 0, 0, 0)),
            pl.BlockSpec((None, 1, c), const),
            pl.BlockSpec((None, 1, c), const),
            pl.BlockSpec((None, 1, c), const),
            _resident((None, c, d), const),
            pl.BlockSpec((None, 1, d), const),
        ],
        out_specs=[xspec, pl.BlockSpec((nb, CONV_WIDTH - 1, c), lambda b, i: (b, 0, 0))],
        out_shape=[jax.ShapeDtypeStruct(x.shape, F32),
                   jax.ShapeDtypeStruct((nseq, CONV_WIDTH - 1, c), F32)],
        scratch_shapes=[pltpu.VMEM((nb, HIST_PAD + t, c), F32), pltpu.VMEM((nb, t, c), F32),
                        pltpu.VMEM((SUBLANES - 1, nb, HIST_PAD - SUBLANES + part, lane_blk), F32)],
        compiler_params=_cparams(("arbitrary", "arbitrary")),
        name="conv_mixer",
    )(x, mod, norm_g, hist32, w1, vec(b1),
      jnp.broadcast_to(wdw[:, :, None, :], wdw.shape[:2] + (SUBLANES, c)),
      vec(bdw), vec(lng), vec(lnb), w2, vec(b2))


def _qkv_kernel(x_ref, mod_ref, g_ref, w_ref, q_ref, k_ref, v_ref, *, nb, t, q_scale):
    x = x_ref[...]
    d = x.shape[-1]
    h = _mod_norm(x, g_ref[...], mod_ref[0], mod_ref[1])
    hb = h.reshape(nb * t, d).astype(BF16)
    q = _dot(hb, w_ref[:, 0:d]) * q_scale
    q_ref[...] = q.reshape(nb, t, d).astype(q_ref.dtype)
    k_ref[...] = _dot(hb, w_ref[:, d:2 * d]).reshape(nb, t, d)
    v_ref[...] = _dot(hb, w_ref[:, 2 * d:3 * d]).reshape(nb, t, d)


def _qkv(x, mod, norm_g, w, *, layer, jm, nb, t, q_dtype, q_scale):
    nseq, s, d = x.shape
    xspec = pl.BlockSpec((nb, t, d), lambda b, i: (b, i, 0))
    return pl.pallas_call(
        functools.partial(_qkv_kernel, nb=nb, t=t, q_scale=q_scale),
        grid=(nseq // nb, s // t),
        in_specs=[
            xspec,
            pl.BlockSpec((None, 3, nb, 1, d), lambda b, i: (layer, 1, b, 0, 0)),
            pl.BlockSpec((None, None, 1, d), lambda b, i: (layer, 1, 0, 0)),
            _resident((None, d, 3 * d), lambda b, i: (jm, 0, 0)),
        ],
        out_specs=[xspec, xspec, xspec],
        out_shape=[jax.ShapeDtypeStruct(x.shape, q_dtype), jax.ShapeDtypeStruct(x.shape, F32),
                   jax.ShapeDtypeStruct(x.shape, F32)],
        compiler_params=_cparams(("arbitrary", "arbitrary")),
        name="qkv",
    )(x, mod, norm_g, w)


def _bucket_of_distance(n_dist):
    dist = np.arange(n_dist)
    max_exact = NUM_BUCKETS // 2
    nf = np.maximum(dist, max_exact).astype(np.float32)
    large = max_exact + (np.log(nf / max_exact) / math.log(MAX_DISTANCE / max_exact)
                         * (NUM_BUCKETS - max_exact)).astype(np.int32)
    large = np.minimum(large, NUM_BUCKETS - 1)
    return np.where(dist < max_exact, dist, large).astype(np.int32)


_BUCKETS = _bucket_of_distance(4 * MAX_DISTANCE)
assert _BUCKETS[-1] == NUM_BUCKETS - 1 and np.all(np.diff(_BUCKETS) >= 0)
_BUCKET_START = [int(np.argmax(_BUCKETS >= b)) for b in range(NUM_BUCKETS)]
assert _BUCKET_START[NUM_BUCKETS - 1] <= MOBA_BLOCK


def _bias_of_distance(dist, table):
    bias = jnp.full(dist.shape, table[0], F32)
    for b in range(1, NUM_BUCKETS):
        bias = jnp.where(dist >= _BUCKET_START[b], table[b], bias)
    return bias


def _rank_select(gate, n_valid, first):
    idx = lax.broadcasted_iota(jnp.int32, gate.shape, 1)
    cnt = jnp.zeros(gate.shape, F32)
    for j in range(first, first + n_valid):
        gj = gate[:, j:j + 1]
        tie = jnp.where(idx > j, 1.0, 0.0)
        cnt = cnt + jnp.where(gj > gate, 1.0, jnp.where(gj == gate, tie, 0.0))
    return cnt < float(MOBA_TOP_K)


def _top_k_lanes(gate, valid):
    lane_f = lax.broadcasted_iota(jnp.int32, gate.shape, 1).astype(F32)
    g = jnp.where(valid, gate, -jnp.inf)
    picks = []
    for _ in range(MOBA_TOP_K):
        top = jnp.max(g, axis=1, keepdims=True)
        pick = jnp.min(jnp.where(g == top, lane_f, float(LANES)), axis=1, keepdims=True)
        picks.append(pick)
        g = jnp.where(lane_f == pick, -jnp.inf, g)
    return picks


def _is_picked(picks, n):
    hit = jnp.where(picks[0] == n, 1.0, 0.0)
    for pick in picks[1:]:
        hit = jnp.where(pick == n, 1.0, hit)
    return hit > 0.0


def _moba_prompt_kernel(rb_ref, q_ref, k_ref, v_ref, o_ref, bias_ref, *, n_blk, n_heads):
    pair = pl.program_id(0)
    blk = MOBA_BLOCK
    kt = PROMPT_KEY_TILE

    @pl.when(pl.program_id(1) == 0)
    def _():
        dist = (lax.broadcasted_iota(jnp.int32, (blk, blk), 1)
                - lax.broadcasted_iota(jnp.int32, (blk, blk), 0))
        for hh in range(2):
            table = [rb_ref[b * n_heads + 2 * pair + hh] for b in range(NUM_BUCKETS)]
            far = table[NUM_BUCKETS - 1]
            own = (_bias_of_distance(dist, table) - far) * LOG2E
            bias_ref[hh, 0] = jnp.where(dist >= 0, own, -jnp.inf)
            bias_ref[hh, 1] = (_bias_of_distance(dist + blk, table) - far) * LOG2E

    kf = k_ref[...]
    vf = v_ref[...]
    s_len = kf.shape[0]
    lane = lax.broadcasted_iota(jnp.int32, (1, LANES), 1)
    key_lane = lax.broadcasted_iota(jnp.int32, (s_len, LANES), 1)
    key_blk = lax.broadcasted_iota(jnp.int32, (s_len, LANES), 0) // blk
    km_row = lax.broadcasted_iota(jnp.int32, (LANES, LANES), 0)

    heads = []
    for hh in range(2):
        off = (1 - hh) * HEAD_DIM
        hmask = (lane // HEAD_DIM) == hh
        k_aug = jnp.where(hmask, kf, jnp.where(key_lane == key_blk + off, 1.0, 0.0)).astype(BF16)
        v_aug = jnp.transpose(jnp.where(hmask, vf, 1.0)).astype(BF16)
        km = jnp.zeros((LANES, LANES), F32)
        for j in range(n_blk):
            km = jnp.where(km_row == off + j,
                           jnp.mean(kf[j * blk:(j + 1) * blk, :], axis=0, keepdims=True), km)
        heads.append((hh, off, hmask, k_aug, v_aug, km.astype(BF16)))

    def logits(i, head):
        hh, off, hmask, k_aug, _, km = head
        qi = q_ref[i * blk:(i + 1) * blk, :]
        if i > MOBA_TOP_K:
            gate = _dot_nt(jnp.where(hmask, qi, jnp.zeros_like(qi)), km)
            keep = _rank_select(gate, i, first=off)
            past = jnp.logical_and(lane >= off, lane < off + i)
            drop = jnp.where(past, jnp.where(keep, 0.0, MASKED), 0.0)
            q_aug = jnp.where(hmask, qi, drop.astype(BF16))
        else:
            q_aug = jnp.where(hmask, qi, jnp.zeros_like(qi))
        for j in range(i + 1):
            for r in range(0, blk, kt):
                s = _dot_nt(k_aug[j * blk + r:j * blk + r + kt, :], q_aug)
                if j == i:
                    s = s + bias_ref[hh, 0, r:r + kt, :]
                elif j == i - 1:
                    s = s + bias_ref[hh, 1, r:r + kt, :]
                yield s

    def weighted_values(s_list, head):
        v_aug = head[4]
        m = jnp.max(functools.reduce(jnp.maximum, s_list), axis=0, keepdims=True)
        acc = jnp.zeros((LANES, blk), F32)
        for n, s in enumerate(s_list):
            acc = acc + _dot(v_aug[:, n * kt:(n + 1) * kt], jnp.exp2(s - m).astype(BF16))
        return acc

    units = [(i, head) for i in range(n_blk) for head in heads]
    low = lax.broadcasted_iota(jnp.int32, (LANES, 1), 0) < HEAD_DIM
    ahead = 2
    pending = [list(logits(*unit)) for unit in units[:ahead]]
    accs = []
    for u, (i, head) in enumerate(units):
        if u + ahead < len(units):
            pending.append(list(logits(*units[u + ahead])))
        accs.append(weighted_values(pending.pop(0), head))
        if len(accs) == 2:
            out_t = jnp.where(low, accs[0] * (1.0 / accs[0][HEAD_DIM:HEAD_DIM + 1, :]),
                              accs[1] * (1.0 / accs[1][0:1, :]))
            o_ref[i * blk:(i + 1) * blk, :] = jnp.transpose(out_t).astype(o_ref.dtype)
            accs = []


def _moba_prompt(q, k, v, rel_bias):
    b, s, d = q.shape
    n_heads = d // HEAD_DIM
    blk = MOBA_BLOCK
    n_blk = s // blk
    assert s % blk == 0 and n_heads % 2 == 0 and 2 * HEAD_DIM == LANES
    hspec = pl.BlockSpec((None, s, LANES), lambda p, bi, rb_ref: (bi, 0, p))
    return pl.pallas_call(
        functools.partial(_moba_prompt_kernel, n_blk=n_blk, n_heads=n_heads),
        grid_spec=pltpu.PrefetchScalarGridSpec(
            num_scalar_prefetch=1,
            grid=(n_heads // 2, b),
            in_specs=[hspec, hspec, hspec],
            out_specs=hspec,
            scratch_shapes=[pltpu.VMEM((2, 2, blk, blk), F32)],
        ),
        out_shape=jax.ShapeDtypeStruct((b, s, d), BF16),
        compiler_params=_cparams(("arbitrary", "arbitrary")),
        name="moba_prompt",
    )(rel_bias.reshape(-1), q, k, v)


def _moba_sample_kernel(pt_ref, rb_ref, *refs, n_groups, n_blk, n_heads, t):
    pp = PAGES_PER_STEP
    (ck_hbm, cv_hbm, q_ref, kn_ref, vn_ref, o_ref, ring_ref, sem, a_ref, s_ref, p_ref, pown_ref,
     l_ref, acc_ref, blast_ref, bown_ref, gate_ref, bmax_ref) = refs
    seq = pl.program_id(0)
    step = pl.program_id(1)
    blk = MOBA_BLOCK
    ppb = blk // PAGE_SIZE
    bps = pp // ppb
    d = q_ref.shape[-1]
    cols = n_heads * t
    row_head = lax.broadcasted_iota(jnp.int32, (cols, d), 0) // t
    lane_head = lax.broadcasted_iota(jnp.int32, (cols, d), 1) // HEAD_DIM
    diag = row_head == lane_head
    lane = lax.broadcasted_iota(jnp.int32, (cols, LANES), 1)

    steps = 2 * n_groups
    n_pages = n_groups * pp
    group = seq * steps + step
    n_stream = pl.num_programs(0) * steps

    def group_copies(m, src_hbm, first_page):
        slot = m % RING_GROUPS
        return [pltpu.make_async_copy(
            src_hbm.at[pt_ref[(m // steps) * n_pages + first_page + r]],
            ring_ref.at[slot, r], sem.at[slot]) for r in range(pp)]

    def start_group(m):
        st = m % steps

        @pl.when(st < n_groups)
        def _():
            for copy in group_copies(m, ck_hbm, st * pp):
                copy.start()

        @pl.when(st >= n_groups)
        def _():
            for copy in group_copies(m, cv_hbm, (st - n_groups) * pp):
                copy.start()

    @pl.when(group == 0)
    def _():
        for m in range(RING_GROUPS - 1):
            start_group(jnp.int32(m))

    @pl.when(group + RING_GROUPS - 1 < n_stream)
    def _():
        start_group(group + RING_GROUPS - 1)

    for copy in group_copies(group, ck_hbm, 0):
        copy.wait()
    pages_ref = ring_ref.at[group % RING_GROUPS]

    def new_rows(ref):
        return jnp.concatenate([ref[...], jnp.zeros((PAGE_SIZE - t, ref.shape[-1]), F32)],
                               axis=0).astype(BF16)

    def block_of(r):
        return jnp.concatenate([pages_ref[ppb * r + i] for i in range(ppb)], axis=1).astype(BF16)

    @pl.when(jnp.logical_and(seq == 0, step == 0))
    def _():
        qi = lax.broadcasted_iota(jnp.int32, (t, blk), 0)
        ki = lax.broadcasted_iota(jnp.int32, (t, blk), 1)
        qo = lax.broadcasted_iota(jnp.int32, (t, PAGE_SIZE), 0)
        ko = lax.broadcasted_iota(jnp.int32, (t, PAGE_SIZE), 1)
        for h in range(n_heads):
            table = [rb_ref[b * n_heads + h] for b in range(NUM_BUCKETS)]
            far = table[NUM_BUCKETS - 1]
            rows = slice(h * t, (h + 1) * t)
            blast_ref[rows, :] = _bias_of_distance(blk + qi - ki, table) - far
            bown_ref[rows, :] = jnp.where(ko <= qo, _bias_of_distance(qo - ko, table) - far,
                                          -jnp.inf)

    @pl.when(step == 0)
    def _():
        qt = jnp.tile(q_ref[...], (n_heads, 1))
        a_ref[...] = jnp.where(diag, qt, 0.0).astype(BF16)
        acc_ref[...] = jnp.zeros_like(acc_ref)
        gate_ref[...] = jnp.full_like(gate_ref, -jnp.inf)
        bmax_ref[...] = jnp.full_like(bmax_ref, -jnp.inf)

    @pl.when(step < n_groups)
    def _():
        a = a_ref[...]
        gate = gate_ref[...]
        bmax = bmax_ref[...]
        for r in range(bps):
            n = step * bps + r
            s = _dot(a, block_of(r))
            s_ref[n] = s
            gate = jnp.where(lane == n, jnp.sum(s, axis=1, keepdims=True), gate)
            bmax = jnp.where(lane == n, jnp.max(s, axis=1, keepdims=True), bmax)
        gate_ref[...] = gate
        bmax_ref[...] = bmax

    @pl.when(step == n_groups)
    def _():
        picks = _top_k_lanes(gate_ref[...], lane < n_blk)
        picked = functools.partial(_is_picked, picks)
        lane_f = lane.astype(F32)
        last = n_blk - 1
        s_own = _dot_nt(a_ref[...], new_rows(kn_ref)) + bown_ref[...]
        bmax = jnp.where(lane == last,
                         jnp.max(s_ref[last] + blast_ref[...], axis=1, keepdims=True), bmax_ref[...])
        m = jnp.maximum(jnp.max(s_own, axis=1, keepdims=True),
                        jnp.max(jnp.where(picked(lane_f), bmax, -jnp.inf), axis=1, keepdims=True))
        p_own = jnp.exp(s_own - m)
        lsum = jnp.zeros((cols, blk), F32)
        for n in range(n_blk):
            sn = s_ref[n] + blast_ref[...] if n == last else s_ref[n]
            p = jnp.where(picked(float(n)), jnp.exp(sn - m), 0.0)
            lsum = lsum + p
            p_ref[n] = p.astype(BF16)
        l = jnp.sum(lsum, axis=1, keepdims=True) + jnp.sum(p_own, axis=1, keepdims=True)
        pown_ref[...] = p_own.astype(BF16)
        l_ref[...] = jnp.broadcast_to(l, (cols, LANES))

    @pl.when(step >= n_groups)
    def _():
        acc = acc_ref[...]
        for r in range(bps):
            acc = acc + _dot_nt(p_ref[(step - n_groups) * bps + r], block_of(r))
        acc_ref[...] = acc

    @pl.when(step == 2 * n_groups - 1)
    def _():
        acc = acc_ref[...] + _dot(pown_ref[...], new_rows(vn_ref))
        acc = acc * jnp.tile(1.0 / l_ref[...], (1, d // LANES))
        acc = jnp.where(diag, acc, 0.0)
        o_ref[...] = jnp.sum(acc.reshape(n_heads, t, d), axis=0).astype(o_ref.dtype)


def _moba_sample(q, k_new, v_new, cache_k, cache_v, page_table, rel_bias):
    nseq, t, d = q.shape
    n_heads = d // HEAD_DIM
    n_pages = page_table.shape[1]
    past = n_pages * PAGE_SIZE
    blk = MOBA_BLOCK
    n_blk = past // blk
    pp = PAGES_PER_STEP
    cols = n_heads * t
    assert past % blk == 0 and blk % PAGE_SIZE == 0 and pp % (blk // PAGE_SIZE) == 0
    assert n_pages % pp == 0 and cols % SUBLANES == 0 and t % SUBLANES == 0 and t <= PAGE_SIZE
    assert MOBA_TOP_K <= n_blk <= LANES
    g = n_pages // pp
    assert nseq * 2 * g >= RING_GROUPS

    seq = pl.BlockSpec((None, t, d), lambda b, s, pt, rb: (b, 0, 0))
    hbm = pl.BlockSpec(memory_space=pl.ANY)
    in_specs = [hbm, hbm, seq, seq, seq]
    return pl.pallas_call(
        functools.partial(_moba_sample_kernel, n_groups=g, n_blk=n_blk, n_heads=n_heads, t=t),
        grid_spec=pltpu.PrefetchScalarGridSpec(
            num_scalar_prefetch=2,
            grid=(nseq, 2 * g),
            in_specs=in_specs,
            out_specs=seq,
            scratch_shapes=[
                pltpu.VMEM((RING_GROUPS, pp, d, PAGE_SIZE), F32),
                pltpu.SemaphoreType.DMA((RING_GROUPS,)),
                pltpu.VMEM((cols, d), BF16),
                pltpu.VMEM((n_blk, cols, blk), F32),
                pltpu.VMEM((n_blk, cols, blk), BF16),
                pltpu.VMEM((cols, PAGE_SIZE), BF16),
                pltpu.VMEM((cols, LANES), F32),
                pltpu.VMEM((cols, d), F32),
                pltpu.VMEM((cols, blk), F32),
                pltpu.VMEM((cols, PAGE_SIZE), F32),
                pltpu.VMEM((cols, LANES), F32),
                pltpu.VMEM((cols, LANES), F32),
            ],
        ),
        out_shape=jax.ShapeDtypeStruct((nseq, t, d), BF16),
        compiler_params=_cparams(("arbitrary", "arbitrary")),
        name="moba_sample",
    )(page_table.reshape(-1), rel_bias.reshape(-1), cache_k, cache_v, q, k_new, v_new)


def _tile_rows(s):
    for t in (512, 256, 128, 64, 32, 16, 8):
        if s % t == 0:
            return t
    raise ValueError(f"sequence length {s} is not a multiple of {SUBLANES}")


def _trunk(xs, mods, hists_in, past, page_table, w, *, t):
    d = xs[0].shape[-1]
    n_heads = d // HEAD_DIM
    depth = w["norm_g"].shape[0]
    tiles = [(1, t), (SUBLANES, xs[1].shape[1])]
    qkv_tiles = [(1, t), xs[1].shape[:2]]
    hists, new_k, new_v = [[], []], [[], []], [[], []]
    for layer in range(depth):
        jm = layer // 2
        ffn = functools.partial(_ffn, mods=mods, norm_g=w["norm_g"], wg=w["wg"], wu=w["wu"],
                                wd=w["wd"], layer=layer, t=t)
        xs = list(ffn(xs, sub=0))
        attns = None
        if layer % 2 == 0:
            for k in range(2):
                xs[k], st = _conv_mixer(xs[k], mods[k], w["norm_g"], hists_in[k][jm], w["pw1"],
                                        w["b_pw1"], w["dw"], w["b_dw"], w["ln_g"], w["ln_b"],
                                        w["pw2"], w["b_pw2"], layer=layer, jm=jm,
                                        nb=tiles[k][0], t=tiles[k][1])
                hists[k].append(st)
        else:
            scale = 1.0 / math.sqrt(HEAD_DIM)
            attns = []
            for k in range(2):
                nseq, s, _ = xs[k].shape
                q, kk, vv = _qkv(xs[k], mods[k], w["norm_g"], w["qkv"], layer=layer, jm=jm,
                                 nb=qkv_tiles[k][0], t=qkv_tiles[k][1],
                                 q_dtype=BF16 if k == 0 else F32,
                                 q_scale=scale * LOG2E if k == 0 else scale)
                if k == 0:
                    attns.append(_moba_prompt(q, kk, vv, w["rel_bias"]))
                else:
                    ck, cv, n_phys = past
                    attns.append(_moba_sample(q, kk, vv, ck, cv, page_table + jm * n_phys,
                                              w["rel_bias"]))
                new_k[k].append(kk.reshape(nseq, s, n_heads, HEAD_DIM))
                new_v[k].append(vv.reshape(nseq, s, n_heads, HEAD_DIM))
        last = layer == depth - 1
        xs = ffn(xs, sub=2, attns=attns, wo=w["wo"] if attns is not None else None, wo_idx=jm,
                 final_g=w["final_g"] if last else None)
    return xs, [jnp.stack(h) for h in hists], [jnp.stack(a) for a in new_k], \
        [jnp.stack(a) for a in new_v]


def kernel(x_prompt, x_sample, state_conv, cache_k, cache_v, page_table, c_prompt, c_sample, norm_g, ada_w, ada_b, ffn_w_gate, ffn_w_up, ffn_w_down, conv_w_pw1, conv_b_pw1, conv_w_dw, conv_b_dw, conv_ln_g, conv_ln_b, conv_w_pw2, conv_b_pw2, attn_w_qkv, attn_w_o, rel_bias, final_norm_g):
    b, s, d = x_prompt.shape
    depth = norm_g.shape[0]
    w = dict(
        norm_g=norm_g.reshape(depth, 3, 1, d),
        wg=ffn_w_gate, wu=ffn_w_up, wd=ffn_w_down,
        pw1=conv_w_pw1.astype(BF16), b_pw1=conv_b_pw1, dw=conv_w_dw, b_dw=conv_b_dw,
        ln_g=conv_ln_g, ln_b=conv_ln_b, pw2=conv_w_pw2.astype(BF16), b_pw2=conv_b_pw2,
        qkv=attn_w_qkv.astype(BF16), wo=attn_w_o.astype(BF16), rel_bias=rel_bias,
        final_g=final_norm_g,
    )
    mod_p, mod_s = _ada_mod(c_prompt, c_sample, ada_w, ada_b)

    n_layers, n_phys = cache_k.shape[:2]
    pages = lambda c: jnp.transpose(c, (0, 1, 3, 4, 2)).reshape(n_layers * n_phys, d, PAGE_SIZE)

    hist0 = jnp.zeros((conv_w_dw.shape[0], b, CONV_WIDTH - 1, conv_w_dw.shape[2]), F32)
    ys, states, ks, vs = _trunk([x_prompt, x_sample], [mod_p, mod_s], [hist0, state_conv],
                                (pages(cache_k), pages(cache_v), n_phys), page_table, w,
                                t=_tile_rows(s))
    return (ys[0], ys[1], states[0], states[1], ks[0], vs[0], ks[1], vs[1])
```

```python
import functools
import math

import numpy as np
import jax
import jax.numpy as jnp
from jax import lax
from jax.experimental import pallas as pl
from jax.experimental.pallas import tpu as pltpu

F32 = jnp.float32
BF16 = jnp.bfloat16

HEAD_DIM = 64
CONV_WIDTH = 31
MOBA_BLOCK = 256
MOBA_TOP_K = 3
PAGE_SIZE = 128
NUM_BUCKETS = 32
MAX_DISTANCE = 128
NORM_EPS = 1e-6

LANES = 128
SUBLANES = 8
HIST_PAD = 32
VMEM_LIMIT = 56 * 1024 * 1024
PAGES_PER_STEP = 16
RING_GROUPS = 4
PROMPT_KEY_TILE = 256
LOG2E = math.log2(math.e)
MASKED = -1e30


def _cparams(sem):
    return pltpu.CompilerParams(dimension_semantics=sem, vmem_limit_bytes=VMEM_LIMIT)


def _resident(block_shape, index_map):
    return pl.BlockSpec(block_shape, index_map, pipeline_mode=pl.Buffered(1))


def _mod_norm(x, g, shift, scale):
    ms = jnp.mean(x * x, axis=-1, keepdims=True)
    n = x * lax.rsqrt(ms + NORM_EPS)
    return n * g * (1.0 + scale) + shift


def _dot(a, b):
    return jnp.dot(a, b, preferred_element_type=F32)


def _dot_nt(a, b):
    return lax.dot_general(a, b, (((1,), (1,)), ((), ())), preferred_element_type=F32)


def _dot_tn(a, b):
    return lax.dot_general(a, b, (((0,), (0,)), ((), ())), preferred_element_type=F32)


def _mod_kernel(cp_ref, cs_ref, w_ref, b_ref, op_ref, os_ref):
    d = cp_ref.shape[1]
    scp = jax.nn.silu(cp_ref[...]).astype(BF16)
    scs = jax.nn.silu(cs_ref[...]).astype(BF16)
    for m in range(op_ref.shape[0]):
        w = w_ref[:, m * d:(m + 1) * d].astype(BF16)
        op_ref[m] = _dot(scp, w) + b_ref[m]
        os_ref[m] = _dot(scs, w) + b_ref[m]


def _ada_mod(c_prompt, c_sample, ada_w, ada_b):
    d = c_prompt.shape[1]
    depth = ada_w.shape[0]
    n_mod = ada_w.shape[2] // d
    per_step = 3
    assert n_mod % per_step == 0
    outs = pl.pallas_call(
        _mod_kernel,
        grid=(depth, n_mod // per_step),
        in_specs=[
            pl.BlockSpec(c_prompt.shape, lambda l, j: (0, 0)),
            pl.BlockSpec(c_sample.shape, lambda l, j: (0, 0)),
            pl.BlockSpec((None, d, per_step * d), lambda l, j: (l, 0, j)),
            pl.BlockSpec((None, per_step, 1, d), lambda l, j: (l, j, 0, 0)),
        ],
        out_specs=[pl.BlockSpec((None, per_step) + c.shape, lambda l, j: (l, j, 0, 0))
                   for c in (c_prompt, c_sample)],
        out_shape=[jax.ShapeDtypeStruct((depth, n_mod) + c.shape, F32)
                   for c in (c_prompt, c_sample)],
        compiler_params=_cparams(("arbitrary", "arbitrary")),
        name="ada_mod",
    )(c_prompt, c_sample, ada_w, ada_b.reshape(depth, n_mod, 1, d))
    return [o.reshape(depth, n_mod, o.shape[2], 1, d) for o in outs]


def _ffn_kernel(*refs, shapes, ff_chunk, pre_attn, final_norm, layer, half):
    it = iter(refs)
    groups = []
    for _ in shapes:
        x_ref, mod_ref = next(it), next(it)
        a_ref, gate1_ref = (next(it), next(it)) if pre_attn else (None, None)
        groups.append((x_ref, mod_ref, a_ref, gate1_ref))
    g_ref = next(it)
    wo_ref = next(it) if pre_attn else None
    wg_hbm, wu_hbm, wd_hbm = next(it), next(it), next(it)
    fg_ref = next(it) if final_norm else None
    o_refs = [next(it) for _ in shapes]
    act_ref, wg_ref, wu_ref, wd_ref, gu_stage, wd_stage, gu_sem, wd_sem = (next(it) for _ in range(8))
    d_ff = wg_ref.shape[-1]
    n_chunks = d_ff // ff_chunk
    gu_slots, wd_slots = gu_stage.shape[0], wd_stage.shape[0]

    def gu_copy(c):
        cols = slice((c // 2) * ff_chunk, (c // 2 + 1) * ff_chunk)
        src = (wg_hbm if c % 2 == 0 else wu_hbm).at[layer, half, :, cols]
        return pltpu.make_async_copy(src, gu_stage.at[c % gu_slots], gu_sem.at[c % gu_slots])

    def wd_copy(j):
        src = wd_hbm.at[layer, half, j * ff_chunk:(j + 1) * ff_chunk, :]
        return pltpu.make_async_copy(src, wd_stage.at[j % wd_slots], wd_sem.at[j % wd_slots])

    def stage_chunk(j):
        cols = slice(j * ff_chunk, (j + 1) * ff_chunk)
        for c, dst in ((2 * j, wg_ref), (2 * j + 1, wu_ref)):
            gu_copy(c).wait()
            dst[:, cols] = gu_stage[c % gu_slots].astype(BF16)
            if c + gu_slots < 2 * n_chunks:
                gu_copy(c + gu_slots).start()
        wd_copy(j).wait()
        wd_ref[cols, :] = wd_stage[j % wd_slots].astype(BF16)
        if j + wd_slots < n_chunks:
            wd_copy(j + wd_slots).start()

    def run(k, stage):
        nb, t = shapes[k]
        x_ref, mod_ref, a_ref, gate1_ref = groups[k]
        if stage:
            for c in range(gu_slots):
                gu_copy(c).start()
            for j in range(wd_slots):
                wd_copy(j).start()
        x = x_ref[...]
        d = x.shape[-1]
        if pre_attn:
            y = _dot(a_ref[...].reshape(nb * t, d), wo_ref[...])
            x = x + gate1_ref[0] * y.reshape(nb, t, d)
        h = _mod_norm(x, g_ref[...], mod_ref[0], mod_ref[1])
        hb = h.reshape(nb * t, d).astype(BF16)
        for j in range(n_chunks):
            if stage:
                stage_chunk(j)
            sl = slice(j * ff_chunk, (j + 1) * ff_chunk)
            gj = _dot(hb, wg_ref[:, sl])
            uj = _dot(hb, wu_ref[:, sl])
            act_ref[0:nb * t, sl] = (jax.nn.silu(gj) * uj).astype(BF16)
        y = _dot(act_ref[0:nb * t, :], wd_ref[...])
        out = x + 0.5 * mod_ref[2] * y.reshape(nb, t, d)
        if final_norm:
            ms = jnp.mean(out * out, axis=-1, keepdims=True)
            out = out * lax.rsqrt(ms + NORM_EPS) * fg_ref[...]
        o_refs[k][...] = out

    first_step = jnp.logical_and(pl.program_id(0) == 0, pl.program_id(1) == 0)

    @pl.when(first_step)
    def _():
        run(0, stage=True)
        for k in range(1, len(shapes)):
            run(k, stage=False)

    @pl.when(jnp.logical_not(first_step))
    def _():
        run(0, stage=False)


def _ffn(xs, mods, norm_g, wg, wu, wd, *, layer, sub, t, attns=None, wo=None, wo_idx=0,
         final_g=None):
    nseq, s, d = xs[0].shape
    d_ff = wg.shape[-1]
    half = sub // 2
    pre_attn = attns is not None
    final_norm = final_g is not None
    shapes = [(1, t)] + [x.shape[:2] for x in xs[1:]]
    tile = lambda b, i: (b, i, 0)
    whole = lambda b, i: (0, 0, 0)
    in_specs, args, out_specs = [], [], []
    for k, (x, mod) in enumerate(zip(xs, mods)):
        nb, rows = shapes[k]
        seq0 = (lambda b: b) if k == 0 else (lambda b: 0)
        xspec = pl.BlockSpec((nb, rows, d), tile if k == 0 else whole)
        in_specs += [xspec, pl.BlockSpec((None, 3, nb, 1, d),
                                         lambda b, i, seq0=seq0: (layer, sub, seq0(b), 0, 0))]
        args += [x, mod]
        if pre_attn:
            gate1 = 3 * (sub - 1) + 2
            in_specs += [xspec, pl.BlockSpec((None, 1, nb, 1, d),
                                             lambda b, i, seq0=seq0: (layer, gate1, seq0(b), 0, 0))]
            args += [attns[k], mod]
        out_specs.append(xspec)
    in_specs.append(pl.BlockSpec((None, None, 1, d), lambda b, i: (layer, sub, 0, 0)))
    args.append(norm_g)
    if pre_attn:
        in_specs.append(_resident((None, d, d), lambda b, i: (wo_idx, 0, 0)))
        args.append(wo)
    in_specs += [pl.BlockSpec(memory_space=pl.ANY)] * 3
    args += [wg, wu, wd]
    if final_norm:
        in_specs.append(pl.BlockSpec((1, d), lambda b, i: (0, 0)))
        args.append(final_g.reshape(1, d))
    ff_chunk = 256 if d_ff % 256 == 0 else LANES
    n_chunks = d_ff // ff_chunk
    gu_slots = min(4, 2 * n_chunks)
    wd_slots = min(2, n_chunks)
    return pl.pallas_call(
        functools.partial(_ffn_kernel, shapes=shapes, ff_chunk=ff_chunk, pre_attn=pre_attn,
                          final_norm=final_norm, layer=layer, half=half),
        grid=(nseq, s // t),
        in_specs=in_specs,
        out_specs=out_specs,
        out_shape=[jax.ShapeDtypeStruct(x.shape, F32) for x in xs],
        scratch_shapes=[
            pltpu.VMEM((max(nb * rows for nb, rows in shapes), d_ff), BF16),
            pltpu.VMEM((d, d_ff), BF16), pltpu.VMEM((d, d_ff), BF16),
            pltpu.VMEM((d_ff, d), BF16),
            pltpu.VMEM((gu_slots, d, ff_chunk), F32),
            pltpu.VMEM((wd_slots, ff_chunk, d), F32),
            pltpu.SemaphoreType.DMA((gu_slots,)), pltpu.SemaphoreType.DMA((wd_slots,)),
        ],
        compiler_params=_cparams(("arbitrary", "arbitrary")),
        name="ffn",
    )(*args)


def _conv_kernel(x_ref, mod_ref, g_ref, hist_ref, w1_ref, b1_ref, wdw_ref, bdw_ref, lng_ref,
                 lnb_ref, w2_ref, b2_ref, o_ref, st_ref, full_ref, y_ref, shift_ref, *, nb, t,
                 n_tiles, n_parts, row_blk, lane_blk):
    c = w2_ref.shape[0]
    d = x_ref.shape[-1]
    first = HIST_PAD - (CONV_WIDTH - 1)
    shift_rows = shift_ref.shape[2]

    @pl.when(pl.program_id(1) == 0)
    def _():
        full_ref[:, 0:HIST_PAD, :] = hist_ref[...]

    part = t // n_parts

    def glu(p):
        xp = x_ref[:, p * part:(p + 1) * part, :]
        h = _mod_norm(xp, g_ref[...], mod_ref[0], mod_ref[1])
        a = _dot(h.reshape(nb * part, d).astype(BF16), w1_ref[...]) + b1_ref[...]
        u = a[:, :c] * jax.nn.sigmoid(a[:, c:])
        full_ref[:, HIST_PAD + p * part:HIST_PAD + (p + 1) * part, :] = u.reshape(nb, part, c)

    def depthwise(p):
        row0 = p * part
        for c0 in range(0, c, lane_blk):
            lanes = slice(c0, c0 + lane_blk)
            for s in range(1, SUBLANES):
                shift_ref[s - 1] = full_ref[:, row0 + s:row0 + s + shift_rows, lanes]
            for r0 in range(0, part, row_blk):
                acc = jnp.zeros((nb * row_blk // SUBLANES, SUBLANES, lane_blk), F32)
                for w in range(CONV_WIDTH):
                    base, s = divmod(first + w, SUBLANES)
                    lo = base * SUBLANES + r0
                    src = (full_ref[:, row0 + lo:row0 + lo + row_blk, lanes] if s == 0
                           else shift_ref[s - 1, :, lo:lo + row_blk, :])
                    acc = acc + src.reshape(acc.shape) * wdw_ref[w, :, lanes][None]
                y_ref[:, row0 + r0:row0 + r0 + row_blk, lanes] = (
                    acc.reshape(nb, row_blk, lane_blk) + bdw_ref[:, lanes])

    def project(p):
        rows = slice(p * part, (p + 1) * part)
        y = y_ref[:, rows, :].reshape(nb * part, c)
        mu = jnp.mean(y, axis=-1, keepdims=True)
        yc = y - mu
        var = jnp.mean(yc * yc, axis=-1, keepdims=True)
        yn = yc * lax.rsqrt(var + NORM_EPS) * lng_ref[...] + lnb_ref[...]
        out = _dot(jax.nn.silu(yn).astype(BF16), w2_ref[...]) + b2_ref[...]
        o_ref[:, rows, :] = x_ref[:, rows, :] + mod_ref[2] * out.reshape(nb, part, d)

    glu(0)
    for p in range(n_parts):
        if p + 1 < n_parts:
            glu(p + 1)
        depthwise(p)
        project(p)

    st_ref[...] = full_ref[:, t + first:t + HIST_PAD, :]
    if n_tiles > 1:
        full_ref[:, 0:HIST_PAD, :] = full_ref[:, t:t + HIST_PAD, :]


def _conv_mixer(x, mod, norm_g, hist, w1, b1, wdw, bdw, lng, lnb, w2, b2, *, layer, jm, nb, t):
    nseq, s, d = x.shape
    c = w2.shape[1]
    n_tiles = s // t
    acc_vregs = 16
    row_blk = min(t, 64)
    n_parts = 2 if t % (2 * row_blk) == 0 else 1
    part = t // n_parts
    lane_blk = max(LANES, min(c, acc_vregs * SUBLANES * LANES // (nb * row_blk) // LANES * LANES))
    xspec = pl.BlockSpec((nb, t, d), lambda b, i: (b, i, 0))
    const = lambda b, i: (jm, 0, 0)
    vec = lambda a: a.reshape(a.shape[0], 1, a.shape[1])
    hist32 = jnp.pad(hist, ((0, 0), (HIST_PAD - (CONV_WIDTH - 1), 0), (0, 0)))
    return pl.pallas_call(
        functools.partial(_conv_kernel, nb=nb, t=t, n_tiles=n_tiles, n_parts=n_parts,
                          row_blk=row_blk, lane_blk=lane_blk),
        grid=(nseq // nb, n_tiles),
        in_specs=[
            xspec,
            pl.BlockSpec((None, 3, nb, 1, d), lambda b, i: (layer, 1, b, 0, 0)),
            pl.BlockSpec((None, None, 1, d), lambda b, i: (layer, 1, 0, 0)),
            pl.BlockSpec((nb, HIST_PAD, c), lambda b, i: (b, 0, 0)),
            _resident((None, d, 2 * c), const),
            pl.BlockSpec((None, 1, 2 * c), const),
            pl.BlockSpec((None, CONV_WIDTH, SUBLANES, c), lambda b, i: (jm, 0, 0, 0)),
            pl.BlockSpec((None, 1, c), const),
            pl.BlockSpec((None, 1, c), const),
            pl.BlockSpec((None, 1, c), const),
            _resident((None, c, d), const),
            pl.BlockSpec((None, 1, d), const),
        ],
        out_specs=[xspec, pl.BlockSpec((nb, CONV_WIDTH - 1, c), lambda b, i: (b, 0, 0))],
        out_shape=[jax.ShapeDtypeStruct(x.shape, F32),
                   jax.ShapeDtypeStruct((nseq, CONV_WIDTH - 1, c), F32)],
        scratch_shapes=[pltpu.VMEM((nb, HIST_PAD + t, c), F32), pltpu.VMEM((nb, t, c), F32),
                        pltpu.VMEM((SUBLANES - 1, nb, HIST_PAD - SUBLANES + part, lane_blk), F32)],
        compiler_params=_cparams(("arbitrary", "arbitrary")),
        name="conv_mixer",
    )(x, mod, norm_g, hist32, w1, vec(b1),
      jnp.broadcast_to(wdw[:, :, None, :], wdw.shape[:2] + (SUBLANES, c)),
      vec(bdw), vec(lng), vec(lnb), w2, vec(b2))


def _qkv_kernel(x_ref, mod_ref, g_ref, w_ref, q_ref, k_ref, v_ref, *, nb, t, q_scale):
    x = x_ref[...]
    d = x.shape[-1]
    h = _mod_norm(x, g_ref[...], mod_ref[0], mod_ref[1])
    hb = h.reshape(nb * t, d).astype(BF16)
    q = _dot(hb, w_ref[:, 0:d]) * q_scale
    q_ref[...] = q.reshape(nb, t, d).astype(q_ref.dtype)
    k_ref[...] = _dot(hb, w_ref[:, d:2 * d]).reshape(nb, t, d)
    v_ref[...] = _dot(hb, w_ref[:, 2 * d:3 * d]).reshape(nb, t, d)


def _qkv(x, mod, norm_g, w, *, layer, jm, nb, t, q_dtype, q_scale):
    nseq, s, d = x.shape
    xspec = pl.BlockSpec((nb, t, d), lambda b, i: (b, i, 0))
    return pl.pallas_call(
        functools.partial(_qkv_kernel, nb=nb, t=t, q_scale=q_scale),
        grid=(nseq // nb, s // t),
        in_specs=[
            xspec,
            pl.BlockSpec((None, 3, nb, 1, d), lambda b, i: (layer, 1, b, 0, 0)),
            pl.BlockSpec((None, None, 1, d), lambda b, i: (layer, 1, 0, 0)),
            _resident((None, d, 3 * d), lambda b, i: (jm, 0, 0)),
        ],
        out_specs=[xspec, xspec, xspec],
        out_shape=[jax.ShapeDtypeStruct(x.shape, q_dtype), jax.ShapeDtypeStruct(x.shape, F32),
                   jax.ShapeDtypeStruct(x.shape, F32)],
        compiler_params=_cparams(("arbitrary", "arbitrary")),
        name="qkv",
    )(x, mod, norm_g, w)


def _bucket_of_distance(n_dist):
    dist = np.arange(n_dist)
    max_exact = NUM_BUCKETS // 2
    nf = np.maximum(dist, max_exact).astype(np.float32)
    large = max_exact + (np.log(nf / max_exact) / math.log(MAX_DISTANCE / max_exact)
                         * (NUM_BUCKETS - max_exact)).astype(np.int32)
    large = np.minimum(large, NUM_BUCKETS - 1)
    return np.where(dist < max_exact, dist, large).astype(np.int32)


_BUCKETS = _bucket_of_distance(4 * MAX_DISTANCE)
assert _BUCKETS[-1] == NUM_BUCKETS - 1 and np.all(np.diff(_BUCKETS) >= 0)
_BUCKET_START = [int(np.argmax(_BUCKETS >= b)) for b in range(NUM_BUCKETS)]
assert _BUCKET_START[NUM_BUCKETS - 1] <= MOBA_BLOCK


def _bias_of_distance(dist, table):
    bias = jnp.full(dist.shape, table[0], F32)
    for b in range(1, NUM_BUCKETS):
        bias = jnp.where(dist >= _BUCKET_START[b], table[b], bias)
    return bias


def _rank_select(gate, n_valid, first):
    idx = lax.broadcasted_iota(jnp.int32, gate.shape, 1)
    cnt = jnp.zeros(gate.shape, F32)
    for j in range(first, first + n_valid):
        gj = gate[:, j:j + 1]
        tie = jnp.where(idx > j, 1.0, 0.0)
        cnt = cnt + jnp.where(gj > gate, 1.0, jnp.where(gj == gate, tie, 0.0))
    return cnt < float(MOBA_TOP_K)


def _top_k_lanes(gate, valid):
    lane_f = lax.broadcasted_iota(jnp.int32, gate.shape, 1).astype(F32)
    g = jnp.where(valid, gate, -jnp.inf)
    picks = []
    for _ in range(MOBA_TOP_K):
        top = jnp.max(g, axis=1, keepdims=True)
        pick = jnp.min(jnp.where(g == top, lane_f, float(LANES)), axis=1, keepdims=True)
        picks.append(pick)
        g = jnp.where(lane_f == pick, -jnp.inf, g)
    return picks


def _is_picked(picks, n):
    hit = jnp.where(picks[0] == n, 1.0, 0.0)
    for pick in picks[1:]:
        hit = jnp.where(pick == n, 1.0, hit)
    return hit > 0.0


def _moba_prompt_kernel(rb_ref, q_ref, k_ref, v_ref, o_ref, bias_ref, *, n_blk, n_heads):
    pair = pl.program_id(0)
    blk = MOBA_BLOCK
    kt = PROMPT_KEY_TILE

    @pl.when(pl.program_id(1) == 0)
    def _():
        dist = (lax.broadcasted_iota(jnp.int32, (blk, blk), 1)
                - lax.broadcasted_iota(jnp.int32, (blk, blk), 0))
        for hh in range(2):
            table = [rb_ref[b * n_heads + 2 * pair + hh] for b in range(NUM_BUCKETS)]
            far = table[NUM_BUCKETS - 1]
            own = (_bias_of_distance(dist, table) - far) * LOG2E
            bias_ref[hh, 0] = jnp.where(dist >= 0, own, -jnp.inf)
            bias_ref[hh, 1] = (_bias_of_distance(dist + blk, table) - far) * LOG2E

    kf = k_ref[...]
    vf = v_ref[...]
    s_len = kf.shape[0]
    lane = lax.broadcasted_iota(jnp.int32, (1, LANES), 1)
    key_lane = lax.broadcasted_iota(jnp.int32, (s_len, LANES), 1)
    key_blk = lax.broadcasted_iota(jnp.int32, (s_len, LANES), 0) // blk
    km_row = lax.broadcasted_iota(jnp.int32, (LANES, LANES), 0)

    heads = []
    for hh in range(2):
        off = (1 - hh) * HEAD_DIM
        hmask = (lane // HEAD_DIM) == hh
        k_aug = jnp.where(hmask, kf, jnp.where(key_lane == key_blk + off, 1.0, 0.0)).astype(BF16)
        v_aug = jnp.transpose(jnp.where(hmask, vf, 1.0)).astype(BF16)
        km = jnp.zeros((LANES, LANES), F32)
        for j in range(n_blk):
            km = jnp.where(km_row == off + j,
                           jnp.mean(kf[j * blk:(j + 1) * blk, :], axis=0, keepdims=True), km)
        heads.append((hh, off, hmask, k_aug, v_aug, km.astype(BF16)))

    def logits(i, head):
        hh, off, hmask, k_aug, _, km = head
        qi = q_ref[i * blk:(i + 1) * blk, :]
        if i > MOBA_TOP_K:
            gate = _dot_nt(jnp.where(hmask, qi, jnp.zeros_like(qi)), km)
            keep = _rank_select(gate, i, first=off)
            past = jnp.logical_and(lane >= off, lane < off + i)
            drop = jnp.where(past, jnp.where(keep, 0.0, MASKED), 0.0)
            q_aug = jnp.where(hmask, qi, drop.astype(BF16))
        else:
            q_aug = jnp.where(hmask, qi, jnp.zeros_like(qi))
        for j in range(i + 1):
            for r in range(0, blk, kt):
                s = _dot_nt(k_aug[j * blk + r:j * blk + r + kt, :], q_aug)
                if j == i:
                    s = s + bias_ref[hh, 0, r:r + kt, :]
                elif j == i - 1:
                    s = s + bias_ref[hh, 1, r:r + kt, :]
                yield s

    def weighted_values(s_list, head):
        v_aug = head[4]
        m = jnp.max(functools.reduce(jnp.maximum, s_list), axis=0, keepdims=True)
        acc = jnp.zeros((LANES, blk), F32)
        for n, s in enumerate(s_list):
            acc = acc + _dot(v_aug[:, n * kt:(n + 1) * kt], jnp.exp2(s - m).astype(BF16))
        return acc

    units = [(i, head) for i in range(n_blk) for head in heads]
    low = lax.broadcasted_iota(jnp.int32, (LANES, 1), 0) < HEAD_DIM
    ahead = 2
    pending = [list(logits(*unit)) for unit in units[:ahead]]
    accs = []
    for u, (i, head) in enumerate(units):
        if u + ahead < len(units):
            pending.append(list(logits(*units[u + ahead])))
        accs.append(weighted_values(pending.pop(0), head))
        if len(accs) == 2:
            out_t = jnp.where(low, accs[0] * (1.0 / accs[0][HEAD_DIM:HEAD_DIM + 1, :]),
                              accs[1] * (1.0 / accs[1][0:1, :]))
            o_ref[i * blk:(i + 1) * blk, :] = jnp.transpose(out_t).astype(o_ref.dtype)
            accs = []


def _moba_prompt(q, k, v, rel_bias):
    b, s, d = q.shape
    n_heads = d // HEAD_DIM
    blk = MOBA_BLOCK
    n_blk = s // blk
    assert s % blk == 0 and n_heads % 2 == 0 and 2 * HEAD_DIM == LANES
    hspec = pl.BlockSpec((None, s, LANES), lambda p, bi, rb_ref: (bi, 0, p))
    return pl.pallas_call(
        functools.partial(_moba_prompt_kernel, n_blk=n_blk, n_heads=n_heads),
        grid_spec=pltpu.PrefetchScalarGridSpec(
            num_scalar_prefetch=1,
            grid=(n_heads // 2, b),
            in_specs=[hspec, hspec, hspec],
            out_specs=hspec,
            scratch_shapes=[pltpu.VMEM((2, 2, blk, blk), F32)],
        ),
        out_shape=jax.ShapeDtypeStruct((b, s, d), BF16),
        compiler_params=_cparams(("arbitrary", "arbitrary")),
        name="moba_prompt",
    )(rel_bias.reshape(-1), q, k, v)


def _moba_sample_kernel(pt_ref, rb_ref, *refs, n_groups, n_blk, n_heads, t):
    pp = PAGES_PER_STEP
    (ck_hbm, cv_hbm, q_ref, kn_ref, vn_ref, o_ref, ring_ref, sem, a_ref, s_ref, p_ref, pown_ref,
     l_ref, acc_ref, blast_ref, bown_ref, gate_ref, bmax_ref) = refs
    seq = pl.program_id(0)
    step = pl.program_id(1)
    blk = MOBA_BLOCK
    ppb = blk // PAGE_SIZE
    bps = pp // ppb
    d = q_ref.shape[-1]
    cols = n_heads * t
    row_head = lax.broadcasted_iota(jnp.int32, (cols, d), 0) // t
    lane_head = lax.broadcasted_iota(jnp.int32, (cols, d), 1) // HEAD_DIM
    diag = row_head == lane_head
    lane = lax.broadcasted_iota(jnp.int32, (cols, LANES), 1)

    steps = 2 * n_groups
    n_pages = n_groups * pp
    group = seq * steps + step
    n_stream = pl.num_programs(0) * steps

    def group_copies(m, src_hbm, first_page):
        slot = m % RING_GROUPS
        return [pltpu.make_async_copy(
            src_hbm.at[pt_ref[(m // steps) * n_pages + first_page + r]],
            ring_ref.at[slot, r], sem.at[slot]) for r in range(pp)]

    def start_group(m):
        st = m % steps

        @pl.when(st < n_groups)
        def _():
            for copy in group_copies(m, ck_hbm, st * pp):
                copy.start()

        @pl.when(st >= n_groups)
        def _():
            for copy in group_copies(m, cv_hbm, (st - n_groups) * pp):
                copy.start()

    @pl.when(group == 0)
    def _():
        for m in range(RING_GROUPS - 1):
            start_group(jnp.int32(m))

    @pl.when(group + RING_GROUPS - 1 < n_stream)
    def _():
        start_group(group + RING_GROUPS - 1)

    for copy in group_copies(group, ck_hbm, 0):
        copy.wait()
    pages_ref = ring_ref.at[group % RING_GROUPS]

    def new_rows(ref):
        return jnp.concatenate([ref[...], jnp.zeros((PAGE_SIZE - t, ref.shape[-1]), F32)],
                               axis=0).astype(BF16)

    def block_of(r):
        return jnp.concatenate([pages_ref[ppb * r + i] for i in range(ppb)], axis=1).astype(BF16)

    @pl.when(jnp.logical_and(seq == 0, step == 0))
    def _():
        qi = lax.broadcasted_iota(jnp.int32, (t, blk), 0)
        ki = lax.broadcasted_iota(jnp.int32, (t, blk), 1)
        qo = lax.broadcasted_iota(jnp.int32, (t, PAGE_SIZE), 0)
        ko = lax.broadcasted_iota(jnp.int32, (t, PAGE_SIZE), 1)
        for h in range(n_heads):
            table = [rb_ref[b * n_heads + h] for b in range(NUM_BUCKETS)]
            far = table[NUM_BUCKETS - 1]
            rows = slice(h * t, (h + 1) * t)
            blast_ref[rows, :] = _bias_of_distance(blk + qi - ki, table) - far
            bown_ref[rows, :] = jnp.where(ko <= qo, _bias_of_distance(qo - ko, table) - far,
                                          -jnp.inf)

    @pl.when(step == 0)
    def _():
        qt = jnp.tile(q_ref[...], (n_heads, 1))
        a_ref[...] = jnp.where(diag, qt, 0.0).astype(BF16)
        acc_ref[...] = jnp.zeros_like(acc_ref)
        gate_ref[...] = jnp.full_like(gate_ref, -jnp.inf)
        bmax_ref[...] = jnp.full_like(bmax_ref, -jnp.inf)

    @pl.when(step < n_groups)
    def _():
        a = a_ref[...]
        gate = gate_ref[...]
        bmax = bmax_ref[...]
        for r in range(bps):
            n = step * bps + r
            s = _dot(a, block_of(r))
            s_ref[n] = s
            gate = jnp.where(lane == n, jnp.sum(s, axis=1, keepdims=True), gate)
            bmax = jnp.where(lane == n, jnp.max(s, axis=1, keepdims=True), bmax)
        gate_ref[...] = gate
        bmax_ref[...] = bmax

    @pl.when(step == n_groups)
    def _():
        picks = _top_k_lanes(gate_ref[...], lane < n_blk)
        picked = functools.partial(_is_picked, picks)
        lane_f = lane.astype(F32)
        last = n_blk - 1
        s_own = _dot_nt(a_ref[...], new_rows(kn_ref)) + bown_ref[...]
        bmax = jnp.where(lane == last,
                         jnp.max(s_ref[last] + blast_ref[...], axis=1, keepdims=True), bmax_ref[...])
        m = jnp.maximum(jnp.max(s_own, axis=1, keepdims=True),
                        jnp.max(jnp.where(picked(lane_f), bmax, -jnp.inf), axis=1, keepdims=True))
        p_own = jnp.exp(s_own - m)
        lsum = jnp.zeros((cols, blk), F32)
        for n in range(n_blk):
            sn = s_ref[n] + blast_ref[...] if n == last else s_ref[n]
            p = jnp.where(picked(float(n)), jnp.exp(sn - m), 0.0)
            lsum = lsum + p
            p_ref[n] = p.astype(BF16)
        l = jnp.sum(lsum, axis=1, keepdims=True) + jnp.sum(p_own, axis=1, keepdims=True)
        pown_ref[...] = p_own.astype(BF16)
        l_ref[...] = jnp.broadcast_to(l, (cols, LANES))

    @pl.when(step >= n_groups)
    def _():
        acc = acc_ref[...]
        for r in range(bps):
            acc = acc + _dot_nt(p_ref[(step - n_groups) * bps + r], block_of(r))
        acc_ref[...] = acc

    @pl.when(step == 2 * n_groups - 1)
    def _():
        acc = acc_ref[...] + _dot(pown_ref[...], new_rows(vn_ref))
        acc = acc * jnp.tile(1.0 / l_ref[...], (1, d // LANES))
        acc = jnp.where(diag, acc, 0.0)
        o_ref[...] = jnp.sum(acc.reshape(n_heads, t, d), axis=0).astype(o_ref.dtype)


def _moba_sample(q, k_new, v_new, cache_k, cache_v, page_table, rel_bias):
    nseq, t, d = q.shape
    n_heads = d // HEAD_DIM
    n_pages = page_table.shape[1]
    past = n_pages * PAGE_SIZE
    blk = MOBA_BLOCK
    n_blk = past // blk
    pp = PAGES_PER_STEP
    cols = n_heads * t
    assert past % blk == 0 and blk % PAGE_SIZE == 0 and pp % (blk // PAGE_SIZE) == 0
    assert n_pages % pp == 0 and cols % SUBLANES == 0 and t % SUBLANES == 0 and t <= PAGE_SIZE
    assert MOBA_TOP_K <= n_blk <= LANES
    g = n_pages // pp
    assert nseq * 2 * g >= RING_GROUPS

    seq = pl.BlockSpec((None, t, d), lambda b, s, pt, rb: (b, 0, 0))
    hbm = pl.BlockSpec(memory_space=pl.ANY)
    in_specs = [hbm, hbm, seq, seq, seq]
    return pl.pallas_call(
        functools.partial(_moba_sample_kernel, n_groups=g, n_blk=n_blk, n_heads=n_heads, t=t),
        grid_spec=pltpu.PrefetchScalarGridSpec(
            num_scalar_prefetch=2,
            grid=(nseq, 2 * g),
            in_specs=in_specs,
            out_specs=seq,
            scratch_shapes=[
                pltpu.VMEM((RING_GROUPS, pp, d, PAGE_SIZE), F32),
                pltpu.SemaphoreType.DMA((RING_GROUPS,)),
                pltpu.VMEM((cols, d), BF16),
                pltpu.VMEM((n_blk, cols, blk), F32),
                pltpu.VMEM((n_blk, cols, blk), BF16),
                pltpu.VMEM((cols, PAGE_SIZE), BF16),
                pltpu.VMEM((cols, LANES), F32),
                pltpu.VMEM((cols, d), F32),
                pltpu.VMEM((cols, blk), F32),
                pltpu.VMEM((cols, PAGE_SIZE), F32),
                pltpu.VMEM((cols, LANES), F32),
                pltpu.VMEM((cols, LANES), F32),
            ],
        ),
        out_shape=jax.ShapeDtypeStruct((nseq, t, d), BF16),
        compiler_params=_cparams(("arbitrary", "arbitrary")),
        name="moba_sample",
    )(page_table.reshape(-1), rel_bias.reshape(-1), cache_k, cache_v, q, k_new, v_new)


def _tile_rows(s):
    for t in (512, 256, 128, 64, 32, 16, 8):
        if s % t == 0:
            return t
    raise ValueError(f"sequence length {s} is not a multiple of {SUBLANES}")


def _trunk(xs, mods, hists_in, past, page_table, w, *, t):
    d = xs[0].shape[-1]
    n_heads = d // HEAD_DIM
    depth = w["norm_g"].shape[0]
    tiles = [(1, t), (SUBLANES, xs[1].shape[1])]
    qkv_tiles = [(1, t), xs[1].shape[:2]]
    hists, new_k, new_v = [[], []], [[], []], [[], []]
    for layer in range(depth):
        jm = layer // 2
        ffn = functools.partial(_ffn, mods=mods, norm_g=w["norm_g"], wg=w["wg"], wu=w["wu"],
                                wd=w["wd"], layer=layer, t=t)
        xs = list(ffn(xs, sub=0))
        attns = None
        if layer % 2 == 0:
            for k in range(2):
                xs[k], st = _conv_mixer(xs[k], mods[k], w["norm_g"], hists_in[k][jm], w["pw1"],
                                        w["b_pw1"], w["dw"], w["b_dw"], w["ln_g"], w["ln_b"],
                                        w["pw2"], w["b_pw2"], layer=layer, jm=jm,
                                        nb=tiles[k][0], t=tiles[k][1])
                hists[k].append(st)
        else:
            scale = 1.0 / math.sqrt(HEAD_DIM)
            attns = []
            for k in range(2):
                nseq, s, _ = xs[k].shape
                q, kk, vv = _qkv(xs[k], mods[k], w["norm_g"], w["qkv"], layer=layer, jm=jm,
                                 nb=qkv_tiles[k][0], t=qkv_tiles[k][1],
                                 q_dtype=BF16 if k == 0 else F32,
                                 q_scale=scale * LOG2E if k == 0 else scale)
                if k == 0:
                    attns.append(_moba_prompt(q, kk, vv, w["rel_bias"]))
                else:
                    ck, cv, n_phys = past
                    attns.append(_moba_sample(q, kk, vv, ck, cv, page_table + jm * n_phys,
                                              w["rel_bias"]))
                new_k[k].append(kk.reshape(nseq, s, n_heads, HEAD_DIM))
                new_v[k].append(vv.reshape(nseq, s, n_heads, HEAD_DIM))
        last = layer == depth - 1
        xs = ffn(xs, sub=2, attns=attns, wo=w["wo"] if attns is not None else None, wo_idx=jm,
                 final_g=w["final_g"] if last else None)
    return xs, [jnp.stack(h) for h in hists], [jnp.stack(a) for a in new_k], \
        [jnp.stack(a) for a in new_v]


def kernel(x_prompt, x_sample, state_conv, cache_k, cache_v, page_table, c_prompt, c_sample, norm_g, ada_w, ada_b, ffn_w_gate, ffn_w_up, ffn_w_down, conv_w_pw1, conv_b_pw1, conv_w_dw, conv_b_dw, conv_ln_g, conv_ln_b, conv_w_pw2, conv_b_pw2, attn_w_qkv, attn_w_o, rel_bias, final_norm_g):
    b, s, d = x_prompt.shape
    depth = norm_g.shape[0]
    w = dict(
        norm_g=norm_g.reshape(depth, 3, 1, d),
        wg=ffn_w_gate, wu=ffn_w_up, wd=ffn_w_down,
        pw1=conv_w_pw1.astype(BF16), b_pw1=conv_b_pw1, dw=conv_w_dw, b_dw=conv_b_dw,
        ln_g=conv_ln_g, ln_b=conv_ln_b, pw2=conv_w_pw2.astype(BF16), b_pw2=conv_b_pw2,
        qkv=attn_w_qkv.astype(BF16), wo=attn_w_o.astype(BF16), rel_bias=rel_bias,
        final_g=final_norm_g,
    )
    mod_p, mod_s = _ada_mod(c_prompt, c_sample, ada_w, ada_b)

    n_layers, n_phys = cache_k.shape[:2]
    pages = lambda c: jnp.transpose(c, (0, 1, 3, 4, 2)).reshape(n_layers * n_phys, d, PAGE_SIZE)

    hist0 = jnp.zeros((conv_w_dw.shape[0], b, CONV_WIDTH - 1, conv_w_dw.shape[2]), F32)
    ys, states, ks, vs = _trunk([x_prompt, x_sample], [mod_p, mod_s], [hist0, state_conv],
                                (pages(cache_k), pages(cache_v), n_phys), page_table, w,
                                t=_tile_rows(s))
    return (ys[0], ys[1], states[0], states[1], ks[0], vs[0], ks[1], vs[1])
```

```python
import functools
import math

import numpy as np
import jax
import jax.numpy as jnp
from jax import lax
from jax.experimental import pallas as pl
from jax.experimental.pallas import tpu as pltpu

F32 = jnp.float32
BF16 = jnp.bfloat16

HEAD_DIM = 64
CONV_WIDTH = 31
MOBA_BLOCK = 256
MOBA_TOP_K = 3
PAGE_SIZE = 128
NUM_BUCKETS = 32
MAX_DISTANCE = 128
NORM_EPS = 1e-6

LANES = 128
SUBLANES = 8
HIST_PAD = 32
VMEM_LIMIT = 56 * 1024 * 1024
PAGES_PER_STEP = 16
RING_GROUPS = 4
PROMPT_KEY_TILE = 256
LOG2E = math.log2(math.e)
MASKED = -1e30


def _cparams(sem):
    return pltpu.CompilerParams(dimension_semantics=sem, vmem_limit_bytes=VMEM_LIMIT)


def _resident(block_shape, index_map):
    return pl.BlockSpec(block_shape, index_map, pipeline_mode=pl.Buffered(1))


def _mod_norm(x, g, shift, scale):
    ms = jnp.mean(x * x, axis=-1, keepdims=True)
    n = x * lax.rsqrt(ms + NORM_EPS)
    return n * g * (1.0 + scale) + shift


def _dot(a, b):
    return jnp.dot(a, b, preferred_element_type=F32)


def _dot_nt(a, b):
    return lax.dot_general(a, b, (((1,), (1,)), ((), ())), preferred_element_type=F32)


def _dot_tn(a, b):
    return lax.dot_general(a, b, (((0,), (0,)), ((), ())), preferred_element_type=F32)


def _mod_kernel(cp_ref, cs_ref, w_ref, b_ref, op_ref, os_ref):
    d = cp_ref.shape[1]
    scp = jax.nn.silu(cp_ref[...]).astype(BF16)
    scs = jax.nn.silu(cs_ref[...]).astype(BF16)
    for m in range(op_ref.shape[0]):
        w = w_ref[:, m * d:(m + 1) * d].astype(BF16)
        op_ref[m] = _dot(scp, w) + b_ref[m]
        os_ref[m] = _dot(scs, w) + b_ref[m]


def _ada_mod(c_prompt, c_sample, ada_w, ada_b):
    d = c_prompt.shape[1]
    depth = ada_w.shape[0]
    n_mod = ada_w.shape[2] // d
    per_step = 3
    assert n_mod % per_step == 0
    outs = pl.pallas_call(
        _mod_kernel,
        grid=(depth, n_mod // per_step),
        in_specs=[
            pl.BlockSpec(c_prompt.shape, lambda l, j: (0, 0)),
            pl.BlockSpec(c_sample.shape, lambda l, j: (0, 0)),
            pl.BlockSpec((None, d, per_step * d), lambda l, j: (l, 0, j)),
            pl.BlockSpec((None, per_step, 1, d), lambda l, j: (l, j, 0, 0)),
        ],
        out_specs=[pl.BlockSpec((None, per_step) + c.shape, lambda l, j: (l, j, 0, 0))
                   for c in (c_prompt, c_sample)],
        out_shape=[jax.ShapeDtypeStruct((depth, n_mod) + c.shape, F32)
                   for c in (c_prompt, c_sample)],
        compiler_params=_cparams(("arbitrary", "arbitrary")),
        name="ada_mod",
    )(c_prompt, c_sample, ada_w, ada_b.reshape(depth, n_mod, 1, d))
    return [o.reshape(depth, n_mod, o.shape[2], 1, d) for o in outs]


def _ffn_kernel(*refs, shapes, ff_chunk, pre_attn, final_norm, layer, half):
    it = iter(refs)
    groups = []
    for _ in shapes:
        x_ref, mod_ref = next(it), next(it)
        a_ref, gate1_ref = (next(it), next(it)) if pre_attn else (None, None)
        groups.append((x_ref, mod_ref, a_ref, gate1_ref))
    g_ref = next(it)
    wo_ref = next(it) if pre_attn else None
    wg_hbm, wu_hbm, wd_hbm = next(it), next(it), next(it)
    fg_ref = next(it) if final_norm else None
    o_refs = [next(it) for _ in shapes]
    act_ref, wg_ref, wu_ref, wd_ref, gu_stage, wd_stage, gu_sem, wd_sem = (next(it) for _ in range(8))
    d_ff = wg_ref.shape[-1]
    n_chunks = d_ff // ff_chunk
    gu_slots, wd_slots = gu_stage.shape[0], wd_stage.shape[0]

    def gu_copy(c):
        cols = slice((c // 2) * ff_chunk, (c // 2 + 1) * ff_chunk)
        src = (wg_hbm if c % 2 == 0 else wu_hbm).at[layer, half, :, cols]
        return pltpu.make_async_copy(src, gu_stage.at[c % gu_slots], gu_sem.at[c % gu_slots])

    def wd_copy(j):
        src = wd_hbm.at[layer, half, j * ff_chunk:(j + 1) * ff_chunk, :]
        return pltpu.make_async_copy(src, wd_stage.at[j % wd_slots], wd_sem.at[j % wd_slots])

    def stage_chunk(j):
        cols = slice(j * ff_chunk, (j + 1) * ff_chunk)
        for c, dst in ((2 * j, wg_ref), (2 * j + 1, wu_ref)):
            gu_copy(c).wait()
            dst[:, cols] = gu_stage[c % gu_slots].astype(BF16)
            if c + gu_slots < 2 * n_chunks:
                gu_copy(c + gu_slots).start()
        wd_copy(j).wait()
        wd_ref[cols, :] = wd_stage[j % wd_slots].astype(BF16)
        if j + wd_slots < n_chunks:
            wd_copy(j + wd_slots).start()

    def run(k, stage):
        nb, t = shapes[k]
        x_ref, mod_ref, a_ref, gate1_ref = groups[k]
        if stage:
            for c in range(gu_slots):
                gu_copy(c).start()
            for j in range(wd_slots):
                wd_copy(j).start()
        x = x_ref[...]
        d = x.shape[-1]
        if pre_attn:
            y = _dot(a_ref[...].reshape(nb * t, d), wo_ref[...])
            x = x + gate1_ref[0] * y.reshape(nb, t, d)
        h = _mod_norm(x, g_ref[...], mod_ref[0], mod_ref[1])
        hb = h.reshape(nb * t, d).astype(BF16)
        for j in range(n_chunks):
            if stage:
                stage_chunk(j)
            sl = slice(j * ff_chunk, (j + 1) * ff_chunk)
            gj = _dot(hb, wg_ref[:, sl])
            uj = _dot(hb, wu_ref[:, sl])
            act_ref[0:nb * t, sl] = (jax.nn.silu(gj) * uj).astype(BF16)
        y = _dot(act_ref[0:nb * t, :], wd_ref[...])
        out = x + 0.5 * mod_ref[2] * y.reshape(nb, t, d)
        if final_norm:
            ms = jnp.mean(out * out, axis=-1, keepdims=True)
            out = out * lax.rsqrt(ms + NORM_EPS) * fg_ref[...]
        o_refs[k][...] = out

    first_step = jnp.logical_and(pl.program_id(0) == 0, pl.program_id(1) == 0)

    @pl.when(first_step)
    def _():
        run(0, stage=True)
        for k in range(1, len(shapes)):
            run(k, stage=False)

    @pl.when(jnp.logical_not(first_step))
    def _():
        run(0, stage=False)


def _ffn(xs, mods, norm_g, wg, wu, wd, *, layer, sub, t, attns=None, wo=None, wo_idx=0,
         final_g=None):
    nseq, s, d = xs[0].shape
    d_ff = wg.shape[-1]
    half = sub // 2
    pre_attn = attns is not None
    final_norm = final_g is not None
    shapes = [(1, t)] + [x.shape[:2] for x in xs[1:]]
    tile = lambda b, i: (b, i, 0)
    whole = lambda b, i: (0, 0, 0)
    in_specs, args, out_specs = [], [], []
    for k, (x, mod) in enumerate(zip(xs, mods)):
        nb, rows = shapes[k]
        seq0 = (lambda b: b) if k == 0 else (lambda b: 0)
        xspec = pl.BlockSpec((nb, rows, d), tile if k == 0 else whole)
        in_specs += [xspec, pl.BlockSpec((None, 3, nb, 1, d),
                                         lambda b, i, seq0=seq0: (layer, sub, seq0(b), 0, 0))]
        args += [x, mod]
        if pre_attn:
            gate1 = 3 * (sub - 1) + 2
            in_specs += [xspec, pl.BlockSpec((None, 1, nb, 1, d),
                                             lambda b, i, seq0=seq0: (layer, gate1, seq0(b), 0, 0))]
            args += [attns[k], mod]
        out_specs.append(xspec)
    in_specs.append(pl.BlockSpec((None, None, 1, d), lambda b, i: (layer, sub, 0, 0)))
    args.append(norm_g)
    if pre_attn:
        in_specs.append(_resident((None, d, d), lambda b, i: (wo_idx, 0, 0)))
        args.append(wo)
    in_specs += [pl.BlockSpec(memory_space=pl.ANY)] * 3
    args += [wg, wu, wd]
    if final_norm:
        in_specs.append(pl.BlockSpec((1, d), lambda b, i: (0, 0)))
        args.append(final_g.reshape(1, d))
    ff_chunk = 256 if d_ff % 256 == 0 else LANES
    n_chunks = d_ff // ff_chunk
    gu_slots = min(4, 2 * n_chunks)
    wd_slots = min(2, n_chunks)
    return pl.pallas_call(
        functools.partial(_ffn_kernel, shapes=shapes, ff_chunk=ff_chunk, pre_attn=pre_attn,
                          final_norm=final_norm, layer=layer, half=half),
        grid=(nseq, s // t),
        in_specs=in_specs,
        out_specs=out_specs,
        out_shape=[jax.ShapeDtypeStruct(x.shape, F32) for x in xs],
        scratch_shapes=[
            pltpu.VMEM((max(nb * rows for nb, rows in shapes), d_ff), BF16),
            pltpu.VMEM((d, d_ff), BF16), pltpu.VMEM((d, d_ff), BF16),
            pltpu.VMEM((d_ff, d), BF16),
            pltpu.VMEM((gu_slots, d, ff_chunk), F32),
            pltpu.VMEM((wd_slots, ff_chunk, d), F32),
            pltpu.SemaphoreType.DMA((gu_slots,)), pltpu.SemaphoreType.DMA((wd_slots,)),
        ],
        compiler_params=_cparams(("arbitrary", "arbitrary")),
        name="ffn",
    )(*args)


def _conv_kernel(x_ref, mod_ref, g_ref, hist_ref, w1_ref, b1_ref, wdw_ref, bdw_ref, lng_ref,
                 lnb_ref, w2_ref, b2_ref, o_ref, st_ref, full_ref, y_ref, shift_ref, *, nb, t,
                 n_tiles, n_parts, row_blk, lane_blk):
    c = w2_ref.shape[0]
    d = x_ref.shape[-1]
    first = HIST_PAD - (CONV_WIDTH - 1)
    shift_rows = shift_ref.shape[2]

    @pl.when(pl.program_id(1) == 0)
    def _():
        full_ref[:, 0:HIST_PAD, :] = hist_ref[...]

    part = t // n_parts

    def glu(p):
        xp = x_ref[:, p * part:(p + 1) * part, :]
        h = _mod_norm(xp, g_ref[...], mod_ref[0], mod_ref[1])
        a = _dot(h.reshape(nb * part, d).astype(BF16), w1_ref[...]) + b1_ref[...]
        u = a[:, :c] * jax.nn.sigmoid(a[:, c:])
        full_ref[:, HIST_PAD + p * part:HIST_PAD + (p + 1) * part, :] = u.reshape(nb, part, c)

    def depthwise(p):
        row0 = p * part
        for c0 in range(0, c, lane_blk):
            lanes = slice(c0, c0 + lane_blk)
            for s in range(1, SUBLANES):
                shift_ref[s - 1] = full_ref[:, row0 + s:row0 + s + shift_rows, lanes]
            for r0 in range(0, part, row_blk):
                acc = jnp.zeros((nb * row_blk // SUBLANES, SUBLANES, lane_blk), F32)
                for w in range(CONV_WIDTH):
                    base, s = divmod(first + w, SUBLANES)
                    lo = base * SUBLANES + r0
                    src = (full_ref[:, row0 + lo:row0 + lo + row_blk, lanes] if s == 0
                           else shift_ref[s - 1, :, lo:lo + row_blk, :])
                    acc = acc + src.reshape(acc.shape) * wdw_ref[w, :, lanes][None]
                y_ref[:, row0 + r0:row0 + r0 + row_blk, lanes] = (
                    acc.reshape(nb, row_blk, lane_blk) + bdw_ref[:, lanes])

    def project(p):
        rows = slice(p * part, (p + 1) * part)
        y = y_ref[:, rows, :].reshape(nb * part, c)
        mu = jnp.mean(y, axis=-1, keepdims=True)
        yc = y - mu
        var = jnp.mean(yc * yc, axis=-1, keepdims=True)
        yn = yc * lax.rsqrt(var + NORM_EPS) * lng_ref[...] + lnb_ref[...]
        out = _dot(jax.nn.silu(yn).astype(BF16), w2_ref[...]) + b2_ref[...]
        o_ref[:, rows, :] = x_ref[:, rows, :] + mod_ref[2] * out.reshape(nb, part, d)

    glu(0)
    for p in range(n_parts):
        if p + 1 < n_parts:
            glu(p + 1)
        depthwise(p)
        project(p)

    st_ref[...] = full_ref[:, t + first:t + HIST_PAD, :]
    if n_tiles > 1:
        full_ref[:, 0:HIST_PAD, :] = full_ref[:, t:t + HIST_PAD, :]


def _conv_mixer(x, mod, norm_g, hist, w1, b1, wdw, bdw, lng, lnb, w2, b2, *, layer, jm, nb, t):
    nseq, s, d = x.shape
    c = w2.shape[1]
    n_tiles = s // t
    acc_vregs = 16
    row_blk = min(t, 64)
    n_parts = 2 if t % (2 * row_blk) == 0 else 1
    part = t // n_parts
    lane_blk = max(LANES, min(c, acc_vregs * SUBLANES * LANES // (nb * row_blk) // LANES * LANES))
    xspec = pl.BlockSpec((nb, t, d), lambda b, i: (b, i, 0))
    const = lambda b, i: (jm, 0, 0)
    vec = lambda a: a.reshape(a.shape[0], 1, a.shape[1])
    hist32 = jnp.pad(hist, ((0, 0), (HIST_PAD - (CONV_WIDTH - 1), 0), (0, 0)))
    return pl.pallas_call(
        functools.partial(_conv_kernel, nb=nb, t=t, n_tiles=n_tiles, n_parts=n_parts,
                          row_blk=row_blk, lane_blk=lane_blk),
        grid=(nseq // nb, n_tiles),
        in_specs=[
            xspec,
            pl.BlockSpec((None, 3, nb, 1, d), lambda b, i: (layer, 1, b, 0, 0)),
            pl.BlockSpec((None, None, 1, d), lambda b, i: (layer, 1, 0, 0)),
            pl.BlockSpec((nb, HIST_PAD, c), lambda b, i: (b, 0, 0)),
            _resident((None, d, 2 * c), const),
            pl.BlockSpec((None, 1, 2 * c), const),
            pl.BlockSpec((None, CONV_WIDTH, SUBLANES, c), lambda b, i: (jm, 0, 0, 0)),
            pl.BlockSpec((None, 1, c), const),
            pl.BlockSpec((None, 1, c), const),
            pl.BlockSpec((None, 1, c), const),
            _resident((None, c, d), const),
            pl.BlockSpec((None, 1, d), const),
        ],
        out_specs=[xspec, pl.BlockSpec((nb, CONV_WIDTH - 1, c), lambda b, i: (b, 0, 0))],
        out_shape=[jax.ShapeDtypeStruct(x.shape, F32),
                   jax.ShapeDtypeStruct((nseq, CONV_WIDTH - 1, c), F32)],
        scratch_shapes=[pltpu.VMEM((nb, HIST_PAD + t, c), F32), pltpu.VMEM((nb, t, c), F32),
                        pltpu.VMEM((SUBLANES - 1, nb, HIST_PAD - SUBLANES + part, lane_blk), F32)],
        compiler_params=_cparams(("arbitrary", "arbitrary")),
        name="conv_mixer",
    )(x, mod, norm_g, hist32, w1, vec(b1),
      jnp.broadcast_to(wdw[:, :, None, :], wdw.shape[:2] + (SUBLANES, c)),
      vec(bdw), vec(lng), vec(lnb), w2, vec(b2))


def _qkv_kernel(x_ref, mod_ref, g_ref, w_ref, q_ref, k_ref, v_ref, *, nb, t, q_scale):
    x = x_ref[...]
    d = x.shape[-1]
    h = _mod_norm(x, g_ref[...], mod_ref[0], mod_ref[1])
    hb = h.reshape(nb * t, d).astype(BF16)
    q = _dot(hb, w_ref[:, 0:d]) * q_scale
    q_ref[...] = q.reshape(nb, t, d).astype(q_ref.dtype)
    k_ref[...] = _dot(hb, w_ref[:, d:2 * d]).reshape(nb, t, d)
    v_ref[...] = _dot(hb, w_ref[:, 2 * d:3 * d]).reshape(nb, t, d)


def _qkv(x, mod, norm_g, w, *, layer, jm, nb, t, q_dtype, q_scale):
    nseq, s, d = x.shape
    xspec = pl.BlockSpec((nb, t, d), lambda b, i: (b, i, 0))
    return pl.pallas_call(
        functools.partial(_qkv_kernel, nb=nb, t=t, q_scale=q_scale),
        grid=(nseq // nb, s // t),
        in_specs=[
            xspec,
            pl.BlockSpec((None, 3, nb, 1, d), lambda b, i: (layer, 1, b, 0, 0)),
            pl.BlockSpec((None, None, 1, d), lambda b, i: (layer, 1, 0, 0)),
            _resident((None, d, 3 * d), lambda b, i: (jm, 0, 0)),
        ],
        out_specs=[xspec, xspec, xspec],
        out_shape=[jax.ShapeDtypeStruct(x.shape, q_dtype), jax.ShapeDtypeStruct(x.shape, F32),
                   jax.ShapeDtypeStruct(x.shape, F32)],
        compiler_params=_cparams(("arbitrary", "arbitrary")),
        name="qkv",
    )(x, mod, norm_g, w)


def _bucket_of_distance(n_dist):
    dist = np.arange(n_dist)
    max_exact = NUM_BUCKETS // 2
    nf = np.maximum(dist, max_exact).astype(np.float32)
    large = max_exact + (np.log(nf / max_exact) / math.log(MAX_DISTANCE / max_exact)
                         * (NUM_BUCKETS - max_exact)).astype(np.int32)
    large = np.minimum(large, NUM_BUCKETS - 1)
    return np.where(dist < max_exact, dist, large).astype(np.int32)


_BUCKETS = _bucket_of_distance(4 * MAX_DISTANCE)
assert _BUCKETS[-1] == NUM_BUCKETS - 1 and np.all(np.diff(_BUCKETS) >= 0)
_BUCKET_START = [int(np.argmax(_BUCKETS >= b)) for b in range(NUM_BUCKETS)]
assert _BUCKET_START[NUM_BUCKETS - 1] <= MOBA_BLOCK


def _bias_of_distance(dist, table):
    bias = jnp.full(dist.shape, table[0], F32)
    for b in range(1, NUM_BUCKETS):
        bias = jnp.where(dist >= _BUCKET_START[b], table[b], bias)
    return bias


def _rank_select(gate, n_valid, first):
    idx = lax.broadcasted_iota(jnp.int32, gate.shape, 1)
    cnt = jnp.zeros(gate.shape, F32)
    for j in range(first, first + n_valid):
        gj = gate[:, j:j + 1]
        tie = jnp.where(idx > j, 1.0, 0.0)
        cnt = cnt + jnp.where(gj > gate, 1.0, jnp.where(gj == gate, tie, 0.0))
    return cnt < float(MOBA_TOP_K)


def _top_k_lanes(gate, valid):
    lane_f = lax.broadcasted_iota(jnp.int32, gate.shape, 1).astype(F32)
    g = jnp.where(valid, gate, -jnp.inf)
    picks = []
    for _ in range(MOBA_TOP_K):
        top = jnp.max(g, axis=1, keepdims=True)
        pick = jnp.min(jnp.where(g == top, lane_f, float(LANES)), axis=1, keepdims=True)
        picks.append(pick)
        g = jnp.where(lane_f == pick, -jnp.inf, g)
    return picks


def _is_picked(picks, n):
    hit = jnp.where(picks[0] == n, 1.0, 0.0)
    for pick in picks[1:]:
        hit = jnp.where(pick == n, 1.0, hit)
    return hit > 0.0


def _moba_prompt_kernel(rb_ref, q_ref, k_ref, v_ref, o_ref, bias_ref, *, n_blk, n_heads):
    pair = pl.program_id(0)
    blk = MOBA_BLOCK
    kt = PROMPT_KEY_TILE

    @pl.when(pl.program_id(1) == 0)
    def _():
        dist = (lax.broadcasted_iota(jnp.int32, (blk, blk), 1)
                - lax.broadcasted_iota(jnp.int32, (blk, blk), 0))
        for hh in range(2):
            table = [rb_ref[b * n_heads + 2 * pair + hh] for b in range(NUM_BUCKETS)]
            far = table[NUM_BUCKETS - 1]
            own = (_bias_of_distance(dist, table) - far) * LOG2E
            bias_ref[hh, 0] = jnp.where(dist >= 0, own, -jnp.inf)
            bias_ref[hh, 1] = (_bias_of_distance(dist + blk, table) - far) * LOG2E

    kf = k_ref[...]
    vf = v_ref[...]
    s_len = kf.shape[0]
    lane = lax.broadcasted_iota(jnp.int32, (1, LANES), 1)
    key_lane = lax.broadcasted_iota(jnp.int32, (s_len, LANES), 1)
    key_blk = lax.broadcasted_iota(jnp.int32, (s_len, LANES), 0) // blk
    km_row = lax.broadcasted_iota(jnp.int32, (LANES, LANES), 0)

    heads = []
    for hh in range(2):
        off = (1 - hh) * HEAD_DIM
        hmask = (lane // HEAD_DIM) == hh
        k_aug = jnp.where(hmask, kf, jnp.where(key_lane == key_blk + off, 1.0, 0.0)).astype(BF16)
        v_aug = jnp.transpose(jnp.where(hmask, vf, 1.0)).astype(BF16)
        km = jnp.zeros((LANES, LANES), F32)
        for j in range(n_blk):
            km = jnp.where(km_row == off + j,
                           jnp.mean(kf[j * blk:(j + 1) * blk, :], axis=0, keepdims=True), km)
        heads.append((hh, off, hmask, k_aug, v_aug, km.astype(BF16)))

    def logits(i, head):
        hh, off, hmask, k_aug, _, km = head
        qi = q_ref[i * blk:(i + 1) * blk, :]
        if i > MOBA_TOP_K:
            gate = _dot_nt(jnp.where(hmask, qi, jnp.zeros_like(qi)), km)
            keep = _rank_select(gate, i, first=off)
            past = jnp.logical_and(lane >= off, lane < off + i)
            drop = jnp.where(past, jnp.where(keep, 0.0, MASKED), 0.0)
            q_aug = jnp.where(hmask, qi, drop.astype(BF16))
        else:
            q_aug = jnp.where(hmask, qi, jnp.zeros_like(qi))
        for j in range(i + 1):
            for r in range(0, blk, kt):
                s = _dot_nt(k_aug[j * blk + r:j * blk + r + kt, :], q_aug)
                if j == i:
                    s = s + bias_ref[hh, 0, r:r + kt, :]
                elif j == i - 1:
                    s = s + bias_ref[hh, 1, r:r + kt, :]
                yield s

    def weighted_values(s_list, head):
        v_aug = head[4]
        m = jnp.max(functools.reduce(jnp.maximum, s_list), axis=0, keepdims=True)
        acc = jnp.zeros((LANES, blk), F32)
        for n, s in enumerate(s_list):
            acc = acc + _dot(v_aug[:, n * kt:(n + 1) * kt], jnp.exp2(s - m).astype(BF16))
        return acc

    units = [(i, head) for i in range(n_blk) for head in heads]
    low = lax.broadcasted_iota(jnp.int32, (LANES, 1), 0) < HEAD_DIM
    ahead = 3
    pending = [list(logits(*unit)) for unit in units[:ahead]]
    accs = []
    for u, (i, head) in enumerate(units):
        if u + ahead < len(units):
            pending.append(list(logits(*units[u + ahead])))
        accs.append(weighted_values(pending.pop(0), head))
        if len(accs) == 2:
            out_t = jnp.where(low, accs[0] * (1.0 / accs[0][HEAD_DIM:HEAD_DIM + 1, :]),
                              accs[1] * (1.0 / accs[1][0:1, :]))
            o_ref[i * blk:(i + 1) * blk, :] = jnp.transpose(out_t).astype(o_ref.dtype)
            accs = []


def _moba_prompt(q, k, v, rel_bias):
    b, s, d = q.shape
    n_heads = d // HEAD_DIM
    blk = MOBA_BLOCK
    n_blk = s // blk
    assert s % blk == 0 and n_heads % 2 == 0 and 2 * HEAD_DIM == LANES
    hspec = pl.BlockSpec((None, s, LANES), lambda p, bi, rb_ref: (bi, 0, p))
    return pl.pallas_call(
        functools.partial(_moba_prompt_kernel, n_blk=n_blk, n_heads=n_heads),
        grid_spec=pltpu.PrefetchScalarGridSpec(
            num_scalar_prefetch=1,
            grid=(n_heads // 2, b),
            in_specs=[hspec, hspec, hspec],
            out_specs=hspec,
            scratch_shapes=[pltpu.VMEM((2, 2, blk, blk), F32)],
        ),
        out_shape=jax.ShapeDtypeStruct((b, s, d), BF16),
        compiler_params=_cparams(("arbitrary", "arbitrary")),
        name="moba_prompt",
    )(rel_bias.reshape(-1), q, k, v)


def _moba_sample_kernel(pt_ref, rb_ref, *refs, n_groups, n_blk, n_heads, t):
    pp = PAGES_PER_STEP
    (ck_hbm, cv_hbm, q_ref, kn_ref, vn_ref, o_ref, ring_ref, sem, a_ref, s_ref, p_ref, pown_ref,
     l_ref, acc_ref, blast_ref, bown_ref, gate_ref, bmax_ref) = refs
    seq = pl.program_id(0)
    step = pl.program_id(1)
    blk = MOBA_BLOCK
    ppb = blk // PAGE_SIZE
    bps = pp // ppb
    d = q_ref.shape[-1]
    cols = n_heads * t
    row_head = lax.broadcasted_iota(jnp.int32, (cols, d), 0) // t
    lane_head = lax.broadcasted_iota(jnp.int32, (cols, d), 1) // HEAD_DIM
    diag = row_head == lane_head
    lane = lax.broadcasted_iota(jnp.int32, (cols, LANES), 1)

    steps = 2 * n_groups
    n_pages = n_groups * pp
    group = seq * steps + step
    n_stream = pl.num_programs(0) * steps

    def group_copies(m, src_hbm, first_page):
        slot = m % RING_GROUPS
        return [pltpu.make_async_copy(
            src_hbm.at[pt_ref[(m // steps) * n_pages + first_page + r]],
            ring_ref.at[slot, r], sem.at[slot]) for r in range(pp)]

    def start_group(m):
        st = m % steps

        @pl.when(st < n_groups)
        def _():
            for copy in group_copies(m, ck_hbm, st * pp):
                copy.start()

        @pl.when(st >= n_groups)
        def _():
            for copy in group_copies(m, cv_hbm, (st - n_groups) * pp):
                copy.start()

    @pl.when(group == 0)
    def _():
        for m in range(RING_GROUPS - 1):
            start_group(jnp.int32(m))

    @pl.when(group + RING_GROUPS - 1 < n_stream)
    def _():
        start_group(group + RING_GROUPS - 1)

    for copy in group_copies(group, ck_hbm, 0):
        copy.wait()
    pages_ref = ring_ref.at[group % RING_GROUPS]

    def new_rows(ref):
        return jnp.concatenate([ref[...], jnp.zeros((PAGE_SIZE - t, ref.shape[-1]), F32)],
                               axis=0).astype(BF16)

    def block_of(r):
        return jnp.concatenate([pages_ref[ppb * r + i] for i in range(ppb)], axis=1).astype(BF16)

    @pl.when(jnp.logical_and(seq == 0, step == 0))
    def _():
        qi = lax.broadcasted_iota(jnp.int32, (t, blk), 0)
        ki = lax.broadcasted_iota(jnp.int32, (t, blk), 1)
        qo = lax.broadcasted_iota(jnp.int32, (t, PAGE_SIZE), 0)
        ko = lax.broadcasted_iota(jnp.int32, (t, PAGE_SIZE), 1)
        for h in range(n_heads):
            table = [rb_ref[b * n_heads + h] for b in range(NUM_BUCKETS)]
            far = table[NUM_BUCKETS - 1]
            rows = slice(h * t, (h + 1) * t)
            blast_ref[rows, :] = _bias_of_distance(blk + qi - ki, table) - far
            bown_ref[rows, :] = jnp.where(ko <= qo, _bias_of_distance(qo - ko, table) - far,
                                          -jnp.inf)

    @pl.when(step == 0)
    def _():
        qt = jnp.tile(q_ref[...], (n_heads, 1))
        a_ref[...] = jnp.where(diag, qt, 0.0).astype(BF16)
        acc_ref[...] = jnp.zeros_like(acc_ref)
        gate_ref[...] = jnp.full_like(gate_ref, -jnp.inf)
        bmax_ref[...] = jnp.full_like(bmax_ref, -jnp.inf)

    @pl.when(step < n_groups)
    def _():
        a = a_ref[...]
        gate = gate_ref[...]
        bmax = bmax_ref[...]
        for r in range(bps):
            n = step * bps + r
            s = _dot(a, block_of(r))
            s_ref[n] = s
            gate = jnp.where(lane == n, jnp.sum(s, axis=1, keepdims=True), gate)
            bmax = jnp.where(lane == n, jnp.max(s, axis=1, keepdims=True), bmax)
        gate_ref[...] = gate
        bmax_ref[...] = bmax

    @pl.when(step == n_groups)
    def _():
        picks = _top_k_lanes(gate_ref[...], lane < n_blk)
        picked = functools.partial(_is_picked, picks)
        lane_f = lane.astype(F32)
        last = n_blk - 1
        s_own = _dot_nt(a_ref[...], new_rows(kn_ref)) + bown_ref[...]
        bmax = jnp.where(lane == last,
                         jnp.max(s_ref[last] + blast_ref[...], axis=1, keepdims=True), bmax_ref[...])
        m = jnp.maximum(jnp.max(s_own, axis=1, keepdims=True),
                        jnp.max(jnp.where(picked(lane_f), bmax, -jnp.inf), axis=1, keepdims=True))
        p_own = jnp.exp(s_own - m)
        lsum = jnp.zeros((cols, blk), F32)
        for n in range(n_blk):
            sn = s_ref[n] + blast_ref[...] if n == last else s_ref[n]
            p = jnp.where(picked(float(n)), jnp.exp(sn - m), 0.0)
            lsum = lsum + p
            p_ref[n] = p.astype(BF16)
        l = jnp.sum(lsum, axis=1, keepdims=True) + jnp.sum(p_own, axis=1, keepdims=True)
        pown_ref[...] = p_own.astype(BF16)
        l_ref[...] = jnp.broadcast_to(l, (cols, LANES))

    @pl.when(step >= n_groups)
    def _():
        acc = acc_ref[...]
        for r in range(bps):
            acc = acc + _dot_nt(p_ref[(step - n_groups) * bps + r], block_of(r))
        acc_ref[...] = acc

    @pl.when(step == 2 * n_groups - 1)
    def _():
        acc = acc_ref[...] + _dot(pown_ref[...], new_rows(vn_ref))
        acc = acc * jnp.tile(1.0 / l_ref[...], (1, d // LANES))
        acc = jnp.where(diag, acc, 0.0)
        o_ref[...] = jnp.sum(acc.reshape(n_heads, t, d), axis=0).astype(o_ref.dtype)


def _moba_sample(q, k_new, v_new, cache_k, cache_v, page_table, rel_bias):
    nseq, t, d = q.shape
    n_heads = d // HEAD_DIM
    n_pages = page_table.shape[1]
    past = n_pages * PAGE_SIZE
    blk = MOBA_BLOCK
    n_blk = past // blk
    pp = PAGES_PER_STEP
    cols = n_heads * t
    assert past % blk == 0 and blk % PAGE_SIZE == 0 and pp % (blk // PAGE_SIZE) == 0
    assert n_pages % pp == 0 and cols % SUBLANES == 0 and t % SUBLANES == 0 and t <= PAGE_SIZE
    assert MOBA_TOP_K <= n_blk <= LANES
    g = n_pages // pp
    assert nseq * 2 * g >= RING_GROUPS

    seq = pl.BlockSpec((None, t, d), lambda b, s, pt, rb: (b, 0, 0))
    hbm = pl.BlockSpec(memory_space=pl.ANY)
    in_specs = [hbm, hbm, seq, seq, seq]
    return pl.pallas_call(
        functools.partial(_moba_sample_kernel, n_groups=g, n_blk=n_blk, n_heads=n_heads, t=t),
        grid_spec=pltpu.PrefetchScalarGridSpec(
            num_scalar_prefetch=2,
            grid=(nseq, 2 * g),
            in_specs=in_specs,
            out_specs=seq,
            scratch_shapes=[
                pltpu.VMEM((RING_GROUPS, pp, d, PAGE_SIZE), F32),
                pltpu.SemaphoreType.DMA((RING_GROUPS,)),
                pltpu.VMEM((cols, d), BF16),
                pltpu.VMEM((n_blk, cols, blk), F32),
                pltpu.VMEM((n_blk, cols, blk), BF16),
                pltpu.VMEM((cols, PAGE_SIZE), BF16),
                pltpu.VMEM((cols, LANES), F32),
                pltpu.VMEM((cols, d), F32),
                pltpu.VMEM((cols, blk), F32),
                pltpu.VMEM((cols, PAGE_SIZE), F32),
                pltpu.VMEM((cols, LANES), F32),
                pltpu.VMEM((cols, LANES), F32),
            ],
        ),
        out_shape=jax.ShapeDtypeStruct((nseq, t, d), BF16),
        compiler_params=_cparams(("arbitrary", "arbitrary")),
        name="moba_sample",
    )(page_table.reshape(-1), rel_bias.reshape(-1), cache_k, cache_v, q, k_new, v_new)


def _tile_rows(s):
    for t in (512, 256, 128, 64, 32, 16, 8):
        if s % t == 0:
            return t
    raise ValueError(f"sequence length {s} is not a multiple of {SUBLANES}")


def _trunk(xs, mods, hists_in, past, page_table, w, *, t):
    d = xs[0].shape[-1]
    n_heads = d // HEAD_DIM
    depth = w["norm_g"].shape[0]
    tiles = [(1, t), (SUBLANES, xs[1].shape[1])]
    qkv_tiles = [(1, t), xs[1].shape[:2]]
    hists, new_k, new_v = [[], []], [[], []], [[], []]
    for layer in range(depth):
        jm = layer // 2
        ffn = functools.partial(_ffn, mods=mods, norm_g=w["norm_g"], wg=w["wg"], wu=w["wu"],
                                wd=w["wd"], layer=layer, t=t)
        xs = list(ffn(xs, sub=0))
        attns = None
        if layer % 2 == 0:
            for k in range(2):
                xs[k], st = _conv_mixer(xs[k], mods[k], w["norm_g"], hists_in[k][jm], w["pw1"],
                                        w["b_pw1"], w["dw"], w["b_dw"], w["ln_g"], w["ln_b"],
                                        w["pw2"], w["b_pw2"], layer=layer, jm=jm,
                                        nb=tiles[k][0], t=tiles[k][1])
                hists[k].append(st)
        else:
            scale = 1.0 / math.sqrt(HEAD_DIM)
            attns = []
            for k in range(2):
                nseq, s, _ = xs[k].shape
                q, kk, vv = _qkv(xs[k], mods[k], w["norm_g"], w["qkv"], layer=layer, jm=jm,
                                 nb=qkv_tiles[k][0], t=qkv_tiles[k][1],
                                 q_dtype=BF16 if k == 0 else F32,
                                 q_scale=scale * LOG2E if k == 0 else scale)
                if k == 0:
                    attns.append(_moba_prompt(q, kk, vv, w["rel_bias"]))
                else:
                    ck, cv, n_phys = past
                    attns.append(_moba_sample(q, kk, vv, ck, cv, page_table + jm * n_phys,
                                              w["rel_bias"]))
                new_k[k].append(kk.reshape(nseq, s, n_heads, HEAD_DIM))
                new_v[k].append(vv.reshape(nseq, s, n_heads, HEAD_DIM))
        last = layer == depth - 1
        xs = ffn(xs, sub=2, attns=attns, wo=w["wo"] if attns is not None else None, wo_idx=jm,
                 final_g=w["final_g"] if last else None)
    return xs, [jnp.stack(h) for h in hists], [jnp.stack(a) for a in new_k], \
        [jnp.stack(a) for a in new_v]


def kernel(x_prompt, x_sample, state_conv, cache_k, cache_v, page_table, c_prompt, c_sample, norm_g, ada_w, ada_b, ffn_w_gate, ffn_w_up, ffn_w_down, conv_w_pw1, conv_b_pw1, conv_w_dw, conv_b_dw, conv_ln_g, conv_ln_b, conv_w_pw2, conv_b_pw2, attn_w_qkv, attn_w_o, rel_bias, final_norm_g):
    b, s, d = x_prompt.shape
    depth = norm_g.shape[0]
    w = dict(
        norm_g=norm_g.reshape(depth, 3, 1, d),
        wg=ffn_w_gate, wu=ffn_w_up, wd=ffn_w_down,
        pw1=conv_w_pw1.astype(BF16), b_pw1=conv_b_pw1, dw=conv_w_dw, b_dw=conv_b_dw,
        ln_g=conv_ln_g, ln_b=conv_ln_b, pw2=conv_w_pw2.astype(BF16), b_pw2=conv_b_pw2,
        qkv=attn_w_qkv.astype(BF16), wo=attn_w_o.astype(BF16), rel_bias=rel_bias,
        final_g=final_norm_g,
    )
    mod_p, mod_s = _ada_mod(c_prompt, c_sample, ada_w, ada_b)

    n_layers, n_phys = cache_k.shape[:2]
    pages = lambda c: jnp.transpose(c, (0, 1, 3, 4, 2)).reshape(n_layers * n_phys, d, PAGE_SIZE)

    hist0 = jnp.zeros((conv_w_dw.shape[0], b, CONV_WIDTH - 1, conv_w_dw.shape[2]), F32)
    ys, states, ks, vs = _trunk([x_prompt, x_sample], [mod_p, mod_s], [hist0, state_conv],
                                (pages(cache_k), pages(cache_v), n_phys), page_table, w,
                                t=_tile_rows(s))
    return (ys[0], ys[1], states[0], states[1], ks[0], vs[0], ks[1], vs[1])
```
